```python
import jax, jax.numpy as jnp
from jax import lax
import numpy as np

D_MODEL = 2048
BATCH = 4
SEQ = 8192
DEPTH = 1

D_MIX = D_MODEL
D_GMLP = D_MIX // 2
D_ATTN = D_MIX - D_GMLP
GMLP_HEADS = 8
GMLP_HEAD_DIM = D_GMLP // GMLP_HEADS
GMLP_CHUNK = 128
ATTN_HEADS = 8
ATTN_HEAD_DIM = D_ATTN // ATTN_HEADS
DILATION_PATTERNS = ((128, 1), (512, 4), (2048, 16))
ATTN_BLOCK = 128
D_IN = 2 * D_GMLP + 3 * D_ATTN
N_GROUPS = 4
EXPERTS_PER_GROUP = 8
N_EXPERTS = N_GROUPS * EXPERTS_PER_GROUP
TOP_K = 2
D_EXPERT = D_MODEL // 2
MOE_BLOCK = 128
N_MOD = 6
EPS = 1e-6

kernel_name = "hybrid_gmlp_dilated_attn_hmoe"


def rms_norm(x, g):
    xf = x.astype(jnp.float32)
    y = xf * lax.rsqrt(jnp.mean(xf * xf, axis=-1, keepdims=True) + EPS)
    return (y * g.astype(jnp.float32)).astype(x.dtype)


def layer_norm(x, g):
    xf = x.astype(jnp.float32)
    xc = xf - jnp.mean(xf, axis=-1, keepdims=True)
    y = xc * lax.rsqrt(jnp.mean(xc * xc, axis=-1, keepdims=True) + EPS)
    return (y * g.astype(jnp.float32)).astype(x.dtype)


def gmlp_spatial_gating(u, v, w_s, b_s):
    B, S, _ = u.shape
    n_chunks = S // GMLP_CHUNK
    vc = v.reshape(B, n_chunks, GMLP_CHUNK, GMLP_HEADS, GMLP_HEAD_DIM)
    causal = jnp.tril(jnp.ones((GMLP_CHUNK, GMLP_CHUNK), dtype=bool))
    wm = jnp.where(causal[None], w_s, jnp.zeros_like(w_s)).astype(v.dtype)
    sv = jnp.einsum('hts,bcshd->bcthd', wm, vc) + b_s.T[:, :, None].astype(v.dtype)
    return u * sv.reshape(B, S, D_GMLP)


def dilated_window_attention(q, k, v, window, dilation):
    B, S, H, E = q.shape
    band = window // dilation
    L = S // dilation
    nb = -(-L // ATTN_BLOCK)
    Lp = nb * ATTN_BLOCK
    blk = ATTN_BLOCK

    def to_blocks(t):
        t = t.reshape(B, L, dilation, H, E)
        t = jnp.pad(t, ((0, 0), (0, Lp - L), (0, 0), (0, 0), (0, 0)))
        return t.reshape(B, nb, blk, dilation, H, E)

    qb, kb, vb = to_blocks(q), to_blocks(k), to_blocks(v)

    def with_prev(t):
        prev = jnp.concatenate([jnp.zeros_like(t[:, :1]), t[:, :-1]], axis=1)
        return jnp.concatenate([prev, t], axis=2)

    kk, vv = with_prev(kb), with_prev(vb)
    s = jnp.einsum('bnqrhe,bnkrhe->bnrhqk', qb, kk,
                   preferred_element_type=jnp.float32) * (E ** -0.5)
    qi = jnp.arange(blk)[:, None]
    kj = jnp.arange(2 * blk)[None, :]
    dist = blk + qi - kj
    in_band = (dist >= 0) & (dist <= band)
    has_prev = (jnp.arange(nb)[:, None, None] > 0) | (kj[None] >= blk)
    mask = (in_band[None] & has_prev)[None, :, None, None]
    s = jnp.where(mask, s, -jnp.inf)
    m = jnp.max(s, axis=-1, keepdims=True)
    p = jnp.exp(s - m)
    l = jnp.sum(p, axis=-1)
    o = jnp.einsum('bnrhqk,bnkrhe->bnqrhe', p, vv.astype(jnp.float32))
    l_t = jnp.transpose(l, (0, 1, 4, 2, 3))
    lse_t = jnp.transpose(m[..., 0], (0, 1, 4, 2, 3)) + jnp.log(l_t)
    o = o / l_t[..., None]
    o = o.reshape(B, Lp, dilation, H, E)[:, :L].reshape(B, S, H, E)
    lse = lse_t.reshape(B, Lp, dilation, H)[:, :L].reshape(B, S, H)
    return o, lse


def dilated_mixture_attention(q, k, v):
    outs, lses = [], []
    for window, dilation in DILATION_PATTERNS:
        o, lse = dilated_window_attention(q, k, v, window, dilation)
        outs.append(o)
        lses.append(lse)
    w = jax.nn.softmax(jnp.stack(lses, axis=0), axis=0)
    o = jnp.sum(w[..., None] * jnp.stack(outs, axis=0), axis=0)
    return o.astype(q.dtype)


def hierarchical_moe(h, w_rg, b_rg, w_re, b_re, w_gate, w_up, w_down):
    B, S, D = h.shape
    N = B * S
    hf = h.reshape(N, D)
    pg = jax.nn.softmax((hf @ w_rg + b_rg).astype(jnp.float32), axis=-1)
    gp, gi = lax.top_k(pg, 1)
    le = (jnp.einsum('nd,gde->nge', hf, w_re) + b_re).astype(jnp.float32)
    le = jnp.take_along_axis(le, gi[:, :, None], axis=1)[:, 0]
    pe = jax.nn.softmax(le, axis=-1)
    ep, ei = lax.top_k(pe, TOP_K)
    ep = ep / jnp.sum(ep, axis=-1, keepdims=True)
    weights = gp * ep
    expert_id = gi * EXPERTS_PER_GROUP + ei

    A = N * TOP_K
    e_flat = expert_id.reshape(A).astype(jnp.int32)
    tok = jnp.repeat(jnp.arange(N, dtype=jnp.int32), TOP_K)
    wt = weights.reshape(A)
    order = jnp.argsort(e_flat)
    sorted_e = e_flat[order]
    counts = jnp.bincount(e_flat, length=N_EXPERTS).astype(jnp.int32)
    padded = ((counts + MOE_BLOCK - 1) // MOE_BLOCK) * MOE_BLOCK
    pend = jnp.cumsum(padded)
    pstart = pend - padded
    cstart = jnp.cumsum(counts) - counts
    rank = jnp.arange(A, dtype=jnp.int32) - cstart[sorted_e]
    dest = pstart[sorted_e] + rank
    n_blocks = (A + N_EXPERTS * (MOE_BLOCK - 1) + MOE_BLOCK - 1) // MOE_BLOCK
    P = n_blocks * MOE_BLOCK
    row_tok = jnp.zeros((P,), jnp.int32).at[dest].set(tok[order])
    row_w = jnp.zeros((P,), jnp.float32).at[dest].set(wt[order])
    block_starts = jnp.arange(n_blocks, dtype=jnp.int32) * MOE_BLOCK
    block_e = jnp.minimum(jnp.searchsorted(pend, block_starts, side='right'),
                          N_EXPERTS - 1).astype(jnp.int32)
    xs = hf[row_tok].reshape(n_blocks, MOE_BLOCK, D)

    def expert_block(args):
        xb, e = args
        return (jax.nn.silu(xb @ w_gate[e]) * (xb @ w_up[e])) @ w_down[e]

    ys = lax.map(expert_block, (xs, block_e)).reshape(P, D)
    ys = ys * row_w[:, None].astype(ys.dtype)
    out = jax.ops.segment_sum(ys, row_tok, num_segments=N)
    return out.reshape(B, S, D).astype(h.dtype)


def setup_inputs(seed: int = 0) -> dict:
    key = jax.random.key(seed)
    ks = jax.random.split(key, 24)
    f32 = jnp.float32
    n = lambda k, shape, s: (jax.random.normal(k, shape, f32) * s)
    gain = lambda k, shape: 1.0 + 0.05 * jax.random.normal(k, shape, f32)
    L = DEPTH
    return {
        "x": n(ks[0], (BATCH, SEQ, D_MODEL), 1.0),
        "c": n(ks[1], (BATCH, D_MODEL), 1.0),
        "w_mod": n(ks[2], (L, D_MODEL, N_MOD * D_MODEL), 0.2 * D_MODEL ** -0.5),
        "b_mod": n(ks[3], (L, N_MOD * D_MODEL), 0.01),
        "g_pre_mix": gain(ks[4], (L, D_MODEL)),
        "g_post_mix": gain(ks[5], (L, D_MODEL)),
        "w_in": n(ks[6], (L, D_MODEL, D_IN), D_MODEL ** -0.5),
        "g_gmlp_v": gain(ks[7], (L, D_GMLP)),
        "w_spatial": n(ks[8], (L, GMLP_HEADS, GMLP_CHUNK, GMLP_CHUNK), 0.5 * GMLP_CHUNK ** -0.5),
        "b_spatial": gain(ks[9], (L, GMLP_HEADS, GMLP_CHUNK)),
        "g_out_gmlp": gain(ks[10], (L, D_GMLP)),
        "g_out_attn": gain(ks[11], (L, D_ATTN)),
        "w_out": n(ks[12], (L, D_MIX, D_MODEL), D_MIX ** -0.5),
        "g_pre_ffn": gain(ks[13], (L, D_MODEL)),
        "g_post_ffn": gain(ks[14], (L, D_MODEL)),
        "w_router_group": n(ks[15], (L, D_MODEL, N_GROUPS), D_MODEL ** -0.5),
        "b_router_group": n(ks[16], (L, N_GROUPS), 0.01),
        "w_router_expert": n(ks[17], (L, N_GROUPS, D_MODEL, EXPERTS_PER_GROUP), D_MODEL ** -0.5),
        "b_router_expert": n(ks[18], (L, N_GROUPS, EXPERTS_PER_GROUP), 0.01),
        "w_gate": n(ks[19], (L, N_EXPERTS, D_MODEL, D_EXPERT), D_MODEL ** -0.5),
        "w_up": n(ks[20], (L, N_EXPERTS, D_MODEL, D_EXPERT), D_MODEL ** -0.5),
        "w_down": n(ks[21], (L, N_EXPERTS, D_EXPERT, D_MODEL), D_EXPERT ** -0.5),
    }


def reference(x, c, w_mod, b_mod, g_pre_mix, g_post_mix, w_in, g_gmlp_v, w_spatial,
              b_spatial, g_out_gmlp, g_out_attn, w_out, g_pre_ffn, g_post_ffn,
              w_router_group, b_router_group, w_router_expert, b_router_expert,
              w_gate, w_up, w_down):
    B, S, D = x.shape
    for l in range(DEPTH):
        mod = jax.nn.silu(c) @ w_mod[l] + b_mod[l]
        shift1, scale1, gate1, shift2, scale2, gate2 = [
            t[:, None, :] for t in jnp.split(mod, N_MOD, axis=-1)]

        h = rms_norm(x, g_pre_mix[l]) * (1 + scale1) + shift1
        proj = h @ w_in[l]
        u, v, q, k, va = jnp.split(
            proj, [D_GMLP, 2 * D_GMLP, 2 * D_GMLP + D_ATTN, 2 * D_GMLP + 2 * D_ATTN], axis=-1)
        u = jax.nn.gelu(u)
        v = layer_norm(jax.nn.gelu(v), g_gmlp_v[l])
        ya = gmlp_spatial_gating(u, v, w_spatial[l], b_spatial[l])
        hs = (B, S, ATTN_HEADS, ATTN_HEAD_DIM)
        yb = dilated_mixture_attention(q.reshape(hs), k.reshape(hs), va.reshape(hs))
        yb = yb.reshape(B, S, D_ATTN)
        y = jnp.concatenate([rms_norm(ya, g_out_gmlp[l]), rms_norm(yb, g_out_attn[l])], axis=-1)
        y = y @ w_out[l]
        x = x + gate1 * rms_norm(y, g_post_mix[l])

        h = rms_norm(x, g_pre_ffn[l]) * (1 + scale2) + shift2
        y = hierarchical_moe(h, w_router_group[l], b_router_group[l], w_router_expert[l],
                             b_router_expert[l], w_gate[l], w_up[l], w_down[l])
        x = x + gate2 * rms_norm(y, g_post_ffn[l])
    return x
```

```python
import functools
import math

import jax
import jax.numpy as jnp
from jax import lax
from jax.experimental import pallas as pl
from jax.experimental.pallas import tpu as pltpu

F32 = jnp.float32
BF16 = jnp.bfloat16
EPS = 1e-6
NEG = -1e30

DILATION_PATTERNS = ((128, 1), (512, 4), (2048, 16))
ATTN_BLOCK = 128
TOP_K = 2
LANES = 128

TM_INPROJ = 512
TM_GMLP = 512
QB_ATTN = 256
TM_OUTPROJ = 256
TM_DISPATCH = 2048
TM_FFN = 256
TM_COMBINE = 256
TN_MOD = 1024
VMEM_LIMIT = 56 * 1024 * 1024


def _cp(*dims):
    return pltpu.CompilerParams(dimension_semantics=dims, vmem_limit_bytes=VMEM_LIMIT)


def _rms(x):
    return x * lax.rsqrt(jnp.mean(x * x, axis=-1, keepdims=True) + EPS)


def _gelu_tanh(x):
    c = math.sqrt(2.0 / math.pi)
    return x * (0.5 * (1.0 + jnp.tanh(c * (x + 0.044715 * (x * x * x)))))


def _sigmoid(x):
    return 1.0 / (1.0 + jnp.exp(-x))


def _mod_kernel(ct_ref, w_ref, b_ref, o_ref):
    ct = ct_ref[...]
    at = ct * _sigmoid(ct)
    w = w_ref[...]
    for b in range(o_ref.shape[0]):
        o_ref[b:b + 1, :] = jnp.sum(w * at[:, b:b + 1], axis=0, keepdims=True) + b_ref[...]


def _modulation(c, w_mod, b_mod):
    B, D = c.shape
    n_out = w_mod.shape[1]
    return pl.pallas_call(
        _mod_kernel,
        grid=(n_out // TN_MOD,),
        in_specs=[pl.BlockSpec((D, B), lambda j: (0, 0)),
                  pl.BlockSpec((D, TN_MOD), lambda j: (0, j)),
                  pl.BlockSpec((1, TN_MOD), lambda j: (0, j))],
        out_specs=pl.BlockSpec((B, TN_MOD), lambda j: (0, j)),
        out_shape=jax.ShapeDtypeStruct((B, n_out), F32),
        compiler_params=_cp("arbitrary"),
        name="mod",
    )(c.T, w_mod, b_mod.reshape(1, n_out))


def _inproj_kernel(x_ref, g_ref, sc_ref, sh_ref, w_ref, gv_ref, o_ref, h_ref):
    j = pl.program_id(1)

    @pl.when(j == 0)
    def _():
        h = (_rms(x_ref[...]) * g_ref[...]) * (1.0 + sc_ref[...]) + sh_ref[...]
        h_ref[...] = h.astype(BF16)

    acc = jnp.dot(h_ref[...], w_ref[...], preferred_element_type=F32)

    @pl.when(j == 0)
    def _():
        o_ref[...] = _gelu_tanh(acc).astype(BF16)

    @pl.when(j == 1)
    def _():
        v = _gelu_tanh(acc)
        vc = v - jnp.mean(v, axis=-1, keepdims=True)
        vn = vc * lax.rsqrt(jnp.mean(vc * vc, axis=-1, keepdims=True) + EPS)
        o_ref[...] = (vn * gv_ref[...]).astype(BF16)

    @pl.when(j >= 2)
    def _():
        o_ref[...] = acc.astype(BF16)


def _inproj(x2, mod3, g_pre, w_in_bf, g_v, seq):
    N, D = x2.shape
    d_in = w_in_bf.shape[1]
    tn = g_v.shape[1]
    tpb = seq // TM_INPROJ
    return pl.pallas_call(
        _inproj_kernel,
        grid=(N // TM_INPROJ, d_in // tn),
        in_specs=[pl.BlockSpec((TM_INPROJ, D), lambda i, j: (i, 0)),
                  pl.BlockSpec((1, D), lambda i, j: (0, 0)),
                  pl.BlockSpec((None, 1, D), lambda i, j: (i // tpb, 0, 1)),
                  pl.BlockSpec((None, 1, D), lambda i, j: (i // tpb, 0, 0)),
                  pl.BlockSpec((D, tn), lambda i, j: (0, j)),
                  pl.BlockSpec((1, tn), lambda i, j: (0, 0))],
        out_specs=pl.BlockSpec((TM_INPROJ, tn), lambda i, j: (i, j)),
        out_shape=jax.ShapeDtypeStruct((N, d_in), BF16),
        scratch_shapes=[pltpu.VMEM((TM_INPROJ, D), BF16)],
        compiler_params=_cp("arbitrary", "arbitrary"),
        name="inproj",
    )(x2, g_pre, mod3, mod3, w_in_bf, g_v)


def _gmlp_kernel(u_ref, v_ref, w_ref, bt_ref, g_ref, o_ref, ya_ref, *, heads, chunk, hd):
    tm = u_ref.shape[0]
    row = lax.broadcasted_iota(jnp.int32, (chunk, chunk), 0)
    col = lax.broadcasted_iota(jnp.int32, (chunk, chunk), 1)
    causal = col <= row
    for h in range(heads):
        wm = jnp.where(causal, w_ref[h], 0.0).astype(BF16)
        bcol = bt_ref[:, h:h + 1]
        cs = slice(h * hd, (h + 1) * hd)
        for c in range(tm // chunk):
            rs = slice(c * chunk, (c + 1) * chunk)
            sv = jnp.dot(wm, v_ref[rs, cs], preferred_element_type=F32) + bcol
            ya_ref[rs, cs] = u_ref[rs, cs].astype(F32) * sv
    o_ref[...] = (_rms(ya_ref[...]) * g_ref[...]).astype(BF16)


def _gmlp(pa, w_spatial, b_spatial, g_out):
    N = pa.shape[0]
    heads, chunk, _ = w_spatial.shape
    dg = g_out.shape[1]
    kern = functools.partial(_gmlp_kernel, heads=heads, chunk=chunk, hd=dg // heads)
    return pl.pallas_call(
        kern,
        grid=(N // TM_GMLP,),
        in_specs=[pl.BlockSpec((TM_GMLP, dg), lambda i: (i, 0)),
                  pl.BlockSpec((TM_GMLP, dg), lambda i: (i, 1)),
                  pl.BlockSpec((heads, chunk, chunk), lambda i: (0, 0, 0)),
                  pl.BlockSpec((chunk, heads), lambda i: (0, 0)),
                  pl.BlockSpec((1, dg), lambda i: (0, 0))],
        out_specs=pl.BlockSpec((TM_GMLP, dg), lambda i: (i, 0)),
        out_shape=jax.ShapeDtypeStruct((N, dg), BF16),
        scratch_shapes=[pltpu.VMEM((TM_GMLP, dg), F32)],
        compiler_params=_cp("arbitrary"),
        name="gmlp",
    )(pa, pa, w_spatial, b_spatial.T, g_out)


def _attn_kernel(q_ref, kc_ref, kp_ref, vc_ref, vp_ref, *rest, heads, hd, first):
    if first:
        o_ref, lse_ref = rest
    else:
        oo_ref, ol_ref, o_ref, lse_ref = rest
    n = pl.program_id(2)
    blk = ATTN_BLOCK
    qb = q_ref.shape[0]
    ri = lax.broadcasted_iota(jnp.int32, (blk, blk), 0)
    ci = lax.broadcasted_iota(jnp.int32, (blk, blk), 1)
    mask_p = ci >= ri
    mask_c = ci <= ri
    scale = hd ** -0.5
    c2 = scale * math.log2(math.e)
    lane = lax.broadcasted_iota(jnp.int32, (blk, LANES), 1)
    nt = (((1,), (1,)), ((), ()))
    for sb in range(qb // blk):
        rs = slice(sb * blk, (sb + 1) * blk)
        ps = slice((sb - 1) * blk, sb * blk)
        lse_tile = jnp.zeros((blk, LANES), F32)
        if not first:
            old_l = ol_ref[rs, :]
        for h in range(heads):
            cs = slice(h * hd, (h + 1) * hd)
            q = q_ref[rs, cs]
            kcur, vcur = kc_ref[rs, cs], vc_ref[rs, cs]
            if sb == 0:
                kprev, vprev = kp_ref[:, cs], vp_ref[:, cs]
            else:
                kprev, vprev = kc_ref[ps, cs], vc_ref[ps, cs]
            s_c = lax.dot_general(q, kcur, nt, preferred_element_type=F32)
            s_p = lax.dot_general(q, kprev, nt, preferred_element_type=F32)
            s_c = jnp.where(mask_c, s_c, NEG)
            s_p = jnp.where(mask_p, s_p, NEG)
            if sb == 0:
                s_p = jnp.where(n > 0, s_p, NEG)
            m = jnp.maximum(jnp.max(s_c, axis=-1, keepdims=True), jnp.max(s_p, axis=-1, keepdims=True))
            p_c = jnp.exp2((s_c - m) * c2)
            p_p = jnp.exp2((s_p - m) * c2)
            l = jnp.sum(p_c, axis=-1, keepdims=True) + jnp.sum(p_p, axis=-1, keepdims=True)
            acc = (jnp.dot(p_c.astype(BF16), vcur, preferred_element_type=F32)
                   + jnp.dot(p_p.astype(BF16), vprev, preferred_element_type=F32))
            o_b = acc / l
            lse_b = m * scale + jnp.log(l)
            if first:
                o_new, lse_new = o_b, lse_b
            else:
                lo = old_l[:, h:h + 1]
                mx = jnp.maximum(lo, lse_b)
                lse_new = mx + jnp.log(jnp.exp(lo - mx) + jnp.exp(lse_b - mx))
                o_new = oo_ref[rs, cs] * jnp.exp(lo - lse_new) + o_b * jnp.exp(lse_b - lse_new)
            o_ref[rs, cs] = o_new
            lse_tile = jnp.where(lane == h, lse_new, lse_tile)
        lse_ref[rs, :] = lse_tile


def _attn_pattern(pa3, prev, dil, heads, d_attn, col0):
    B, S, d_in = pa3.shape
    L = S // dil
    qb = min(QB_ATTN, L)
    nblk = qb // ATTN_BLOCK
    cpr = d_in // d_attn
    view = pa3.reshape(B, L, dil * d_in)
    qc, kc, vc = col0, col0 + 1, col0 + 2

    def cur(c):
        return pl.BlockSpec((None, qb, d_attn), lambda b, r, n: (b, n, r * cpr + c))

    def prv(c):
        return pl.BlockSpec((None, ATTN_BLOCK, d_attn),
                            lambda b, r, n: (b, jnp.maximum(n * nblk - 1, 0), r * cpr + c))

    o_spec = pl.BlockSpec((None, qb, d_attn), lambda b, r, n: (b, n, r))
    l_spec = pl.BlockSpec((None, qb, LANES), lambda b, r, n: (b, n, r))
    in_specs = [cur(qc), cur(kc), prv(kc), cur(vc), prv(vc)]
    args = [view, view, view, view, view]
    aliases = {}
    if prev is not None:
        in_specs += [o_spec, l_spec]
        args += [prev[0].reshape(B, L, dil * d_attn), prev[1].reshape(B, L, dil * LANES)]
        aliases = {5: 0, 6: 1}
    kern = functools.partial(_attn_kernel, heads=heads, hd=d_attn // heads, first=prev is None)
    o, lse = pl.pallas_call(
        kern,
        grid=(B, dil, L // qb),
        in_specs=in_specs,
        out_specs=[o_spec, l_spec],
        out_shape=[jax.ShapeDtypeStruct((B, L, dil * d_attn), F32),
                   jax.ShapeDtypeStruct((B, L, dil * LANES), F32)],
        input_output_aliases=aliases,
        compiler_params=_cp("arbitrary", "arbitrary", "arbitrary"),
        name=f"attn_d{dil}",
    )(*args)
    return o.reshape(B, S, d_attn), lse.reshape(B, S, LANES)


def _outproj_kernel(ya_ref, ob_ref, x_ref, w_ref, gattn_ref, gpost_ref, gate1_ref, gpre_ref,
                    sc2_ref, sh2_ref, wr_ref, br_ref, x1_ref, h2_ref, route_ref, cat_ref,
                    *, n_groups, epg):
    dg = ya_ref.shape[1]
    tm = ya_ref.shape[0]
    cat_ref[:, :dg] = ya_ref[...]
    cat_ref[:, dg:] = (_rms(ob_ref[...]) * gattn_ref[...]).astype(BF16)
    y = jnp.dot(cat_ref[...], w_ref[...], preferred_element_type=F32)
    x1 = x_ref[...] + gate1_ref[...] * (_rms(y) * gpost_ref[...])
    x1_ref[...] = x1
    h2 = (_rms(x1) * gpre_ref[...]) * (1.0 + sc2_ref[...]) + sh2_ref[...]
    h2_ref[...] = h2

    hi = h2.astype(BF16)
    lo = (h2 - hi.astype(F32)).astype(BF16)
    r = (jnp.dot(hi, wr_ref[...], preferred_element_type=F32)
         + jnp.dot(lo, wr_ref[...], preferred_element_type=F32))
    logits = r[:, :LANES] + r[:, LANES:] + br_ref[...]

    lane = lax.broadcasted_iota(jnp.int32, (tm, LANES), 1)
    lane_f = lane.astype(F32)
    big = float(LANES)
    lg = jnp.where(lane < n_groups, logits, NEG)
    mg = jnp.max(lg, axis=-1, keepdims=True)
    gi = jnp.min(jnp.where(lg == mg, lane_f, big), axis=-1, keepdims=True)
    gp = 1.0 / jnp.sum(jnp.exp(lg - mg), axis=-1, keepdims=True)
    e_lo = n_groups + gi * epg
    le = jnp.where(lane_f >= e_lo, jnp.where(lane_f < e_lo + epg, logits, NEG), NEG)
    m1 = jnp.max(le, axis=-1, keepdims=True)
    i1 = jnp.min(jnp.where(le == m1, lane_f, big), axis=-1, keepdims=True)
    le2 = jnp.where(lane_f == i1, NEG, le)
    m2 = jnp.max(le2, axis=-1, keepdims=True)
    i2 = jnp.min(jnp.where(le2 == m2, lane_f, big), axis=-1, keepdims=True)
    t = jnp.exp(m2 - m1)
    w1 = gp / (1.0 + t)
    w2 = gp * t / (1.0 + t)
    route_ref[...] = jnp.where(lane == 0, i1 - n_groups,
                               jnp.where(lane == 1, i2 - n_groups,
                                         jnp.where(lane == 2, w1, jnp.where(lane == 3, w2, 0.0))))


def _outproj(ya_n, ob, x2, w_out_bf, g_attn, g_post, mod3, g_pre, wr, br, seq, n_groups, epg):
    N, D = x2.shape
    dg = ya_n.shape[1]
    tm = TM_OUTPROJ
    tpb = seq // tm
    row = lambda i: (i, 0)
    const = lambda i: (0, 0)
    modc = lambda k: pl.BlockSpec((None, 1, D), lambda i: (i // tpb, 0, k))
    kern = functools.partial(_outproj_kernel, n_groups=n_groups, epg=epg)
    return pl.pallas_call(
        kern,
        grid=(N // tm,),
        in_specs=[pl.BlockSpec((tm, dg), row), pl.BlockSpec((tm, dg), row), pl.BlockSpec((tm, D), row),
                  pl.BlockSpec((D, D), const), pl.BlockSpec((1, dg), const), pl.BlockSpec((1, D), const),
                  modc(2), pl.BlockSpec((1, D), const), modc(4), modc(3),
                  pl.BlockSpec((D, 2 * LANES), const), pl.BlockSpec((1, LANES), const)],
        out_specs=[pl.BlockSpec((tm, D), row), pl.BlockSpec((tm, D), row), pl.BlockSpec((tm, LANES), row)],
        out_shape=[jax.ShapeDtypeStruct((N, D), F32), jax.ShapeDtypeStruct((N, D), F32),
                   jax.ShapeDtypeStruct((N, LANES), F32)],
        scratch_shapes=[pltpu.VMEM((tm, D), BF16)],
        compiler_params=_cp("arbitrary"),
        name="outproj",
    )(ya_n, ob, x2, w_out_bf, g_attn, g_post, mod3, g_pre, mod3, mod3, wr, br)


def _dispatch_kernel(dest_ref, h_hbm, xs_hbm, sem):
    i = pl.program_id(0)
    tm = dest_ref.shape[1] // TOP_K

    def copy(t, d):
        return pltpu.make_async_copy(h_hbm.at[pl.ds(i * tm + t, 1)], xs_hbm.at[pl.ds(d, 1)], sem)

    def start(t, c):
        for s in range(TOP_K):
            copy(t, dest_ref[0, TOP_K * t + s]).start()
        return c

    def wait(t, c):
        for s in range(TOP_K):
            copy(t, dest_ref[0, TOP_K * t + s]).wait()
        return c

    lax.fori_loop(0, tm, start, 0, unroll=8)
    lax.fori_loop(0, tm, wait, 0, unroll=8)


def _dispatch(h2, dest, n_rows):
    N, D = h2.shape
    tm = TM_DISPATCH
    return pl.pallas_call(
        _dispatch_kernel,
        grid=(N // tm,),
        in_specs=[pl.BlockSpec((None, 1, TOP_K * tm), lambda i: (i, 0, 0), memory_space=pltpu.SMEM),
                  pl.BlockSpec(memory_space=pl.ANY)],
        out_specs=pl.BlockSpec(memory_space=pl.ANY),
        out_shape=jax.ShapeDtypeStruct((n_rows, D), F32),
        scratch_shapes=[pltpu.SemaphoreType.DMA(())],
        compiler_params=_cp("arbitrary"),
        name="dispatch",
    )(dest.reshape(N // tm, 1, TOP_K * tm), h2)


def _ffn_kernel(te_ref, tv_ref, ti_ref, to_ref, x_ref, wg_ref, wu_ref, wd_ref, y_ref):
    del te_ref, ti_ref, to_ref
    nv = tv_ref[pl.program_id(0)]

    @pl.when(nv > 0)
    def _():
        rows = lax.broadcasted_iota(jnp.int32, (x_ref.shape[0], 1), 0)
        x = jnp.where(rows < nv, x_ref[...], 0.0).astype(BF16)
        g = jnp.dot(x, wg_ref[...], preferred_element_type=F32)
        u = jnp.dot(x, wu_ref[...], preferred_element_type=F32)
        hm = ((g * _sigmoid(g)) * u).astype(BF16)
        y_ref[...] = jnp.dot(hm, wd_ref[...], preferred_element_type=F32)

    @pl.when(nv == 0)
    def _():
        y_ref[...] = jnp.zeros_like(y_ref)


def _ffn(xs, tile_e, tile_valid, tile_in, tile_out, wg, wu, wd):
    P, D = xs.shape
    de = wg.shape[2]
    tm = TM_FFN
    nt = tile_e.shape[0]
    grid_spec = pltpu.PrefetchScalarGridSpec(
        num_scalar_prefetch=4,
        grid=(nt,),
        in_specs=[pl.BlockSpec((tm, D), lambda t, te, tv, ti, to: (ti[t], 0)),
                  pl.BlockSpec((None, D, de), lambda t, te, tv, ti, to: (te[t], 0, 0)),
                  pl.BlockSpec((None, D, de), lambda t, te, tv, ti, to: (te[t], 0, 0)),
                  pl.BlockSpec((None, de, D), lambda t, te, tv, ti, to: (te[t], 0, 0))],
        out_specs=pl.BlockSpec((tm, D), lambda t, te, tv, ti, to: (to[t], 0)),
    )
    return pl.pallas_call(
        _ffn_kernel,
        grid_spec=grid_spec,
        out_shape=jax.ShapeDtypeStruct((P + tm, D), F32),
        compiler_params=_cp("arbitrary"),
        name="ffn",
    )(tile_e, tile_valid, tile_in, tile_out, xs, wg, wu, wd)


def _combine_kernel(dest_ref, route_ref, x1_ref, gate2_ref, g_ref, ys_hbm, o_ref, buf, sem):
    tm = x1_ref.shape[0]

    def copy(t, s, d):
        return pltpu.make_async_copy(ys_hbm.at[pl.ds(d, 1)], buf.at[s, pl.ds(t, 1)], sem)

    def start(t, c):
        for s in range(TOP_K):
            copy(t, s, dest_ref[0, TOP_K * t + s]).start()
        return c

    def wait(t, c):
        for s in range(TOP_K):
            copy(t, s, dest_ref[0, TOP_K * t + s]).wait()
        return c

    lax.fori_loop(0, tm, start, 0, unroll=8)
    lax.fori_loop(0, tm, wait, 0, unroll=8)
    y = buf[0] * route_ref[:, 2:3] + buf[1] * route_ref[:, 3:4]
    o_ref[...] = x1_ref[...] + gate2_ref[...] * (_rms(y) * g_ref[...])


def _combine(ys, dest, route, x1, mod3, g_post, seq):
    N, D = x1.shape
    tm = TM_COMBINE
    tpb = seq // tm
    return pl.pallas_call(
        _combine_kernel,
        grid=(N // tm,),
        in_specs=[pl.BlockSpec((None, 1, TOP_K * tm), lambda i: (i, 0, 0), memory_space=pltpu.SMEM),
                  pl.BlockSpec((tm, LANES), lambda i: (i, 0)),
                  pl.BlockSpec((tm, D), lambda i: (i, 0)),
                  pl.BlockSpec((None, 1, D), lambda i: (i // tpb, 0, 5)),
                  pl.BlockSpec((1, D), lambda i: (0, 0)),
                  pl.BlockSpec(memory_space=pl.ANY)],
        out_specs=pl.BlockSpec((tm, D), lambda i: (i, 0)),
        out_shape=jax.ShapeDtypeStruct((N, D), F32),
        scratch_shapes=[pltpu.VMEM((TOP_K, tm, D), F32), pltpu.SemaphoreType.DMA(())],
        compiler_params=_cp("arbitrary"),
        name="combine",
    )(dest.reshape(N // tm, 1, TOP_K * tm), route, x1, mod3, g_post, ys)


def _routing_tables(route, n_experts, tm):
    N = route.shape[0]
    A = N * TOP_K
    e_flat = route[:, :TOP_K].astype(jnp.int32).reshape(A)
    onehot = (e_flat[:, None] == jnp.arange(n_experts, dtype=jnp.int32)[None, :]).astype(jnp.int32)
    csum = jnp.cumsum(onehot, axis=0)
    rank = jnp.take_along_axis(csum, e_flat[:, None], axis=1)[:, 0] - 1
    counts = csum[-1]
    padded = ((counts + tm - 1) // tm) * tm
    pend = jnp.cumsum(padded)
    pstart = pend - padded
    dest = pstart[e_flat] + rank
    nt = A // tm + n_experts
    n_used = pend[-1] // tm
    tidx = jnp.arange(nt, dtype=jnp.int32)
    tstart = tidx * tm
    te = jnp.minimum(jnp.searchsorted(pend, tstart, side='right'), n_experts - 1).astype(jnp.int32)
    tv = jnp.clip(counts[te] - (tstart - pstart[te]), 0, tm).astype(jnp.int32)
    used = tidx < n_used
    last = jnp.maximum(n_used - 1, 0)
    te = jnp.where(used, te, te[last]).astype(jnp.int32)
    ti = jnp.where(used, tidx, last).astype(jnp.int32)
    to = jnp.where(used, tidx, nt).astype(jnp.int32)
    return dest.astype(jnp.int32), te, tv, ti, to, nt


def kernel(x, c, w_mod, b_mod, g_pre_mix, g_post_mix, w_in, g_gmlp_v, w_spatial, b_spatial, g_out_gmlp,
           g_out_attn, w_out, g_pre_ffn, g_post_ffn, w_router_group, b_router_group, w_router_expert,
           b_router_expert, w_gate, w_up, w_down):
    B, S, D = x.shape
    N = B * S
    depth = w_mod.shape[0]
    d_gmlp = g_gmlp_v.shape[1]
    d_attn = g_out_attn.shape[1]
    heads = w_spatial.shape[1]
    n_groups, epg = b_router_expert.shape[1], b_router_expert.shape[2]
    n_experts = n_groups * epg
    assert d_gmlp == d_attn and w_in.shape[2] == 2 * d_gmlp + 3 * d_attn
    assert n_groups + n_experts <= LANES

    x2 = x.reshape(N, D)
    for l in range(depth):
        mod3 = _modulation(c, w_mod[l], b_mod[l]).reshape(B, 1, w_mod.shape[2])

        pa = _inproj(x2, mod3, g_pre_mix[l][None], w_in[l].astype(BF16), g_gmlp_v[l][None], S)
        ya_n = _gmlp(pa, w_spatial[l], b_spatial[l], g_out_gmlp[l][None])
        pa3 = pa.reshape(B, S, pa.shape[1])
        acc = None
        for _, dil in DILATION_PATTERNS:
            acc = _attn_pattern(pa3, acc, dil, heads, d_attn, (2 * d_gmlp) // d_attn)
        ob = acc[0].reshape(N, d_attn)

        wr32 = jnp.concatenate([w_router_group[l],
                                jnp.transpose(w_router_expert[l], (1, 0, 2)).reshape(D, n_experts)], axis=1)
        wr32 = jnp.pad(wr32, ((0, 0), (0, LANES - wr32.shape[1])))
        wr_hi = wr32.astype(BF16)
        wr = jnp.concatenate([wr_hi, (wr32 - wr_hi.astype(F32)).astype(BF16)], axis=1)
        br = jnp.pad(jnp.concatenate([b_router_group[l], b_router_expert[l].reshape(n_experts)]),
                     (0, LANES - n_groups - n_experts))[None]

        x1, h2, route = _outproj(ya_n, ob, x2, w_out[l].astype(BF16), g_out_attn[l][None], g_post_mix[l][None],
                                 mod3, g_pre_ffn[l][None], wr, br, S, n_groups, epg)

        dest, te, tv, ti, to, nt = _routing_tables(route, n_experts, TM_FFN)
        xs = _dispatch(h2, dest, nt * TM_FFN)
        ys = _ffn(xs, te, tv, ti, to, w_gate[l].astype(BF16), w_up[l].astype(BF16), w_down[l].astype(BF16))
        x2 = _combine(ys, dest, route, x1, mod3, g_post_ffn[l][None], S)
    return x2.reshape(B, S, D)
```

```python
import functools
import math

import jax
import jax.numpy as jnp
from jax import lax
from jax.experimental import pallas as pl
from jax.experimental.pallas import tpu as pltpu

F32 = jnp.float32
BF16 = jnp.bfloat16
EPS = 1e-6
NEG = -1e30

DILATION_PATTERNS = ((128, 1), (512, 4), (2048, 16))
ATTN_BLOCK = 128
TOP_K = 2
LANES = 128

TM_INPROJ = 512
TM_GMLP = 512
QB_ATTN = 256
TM_OUTPROJ = 256
TM_DISPATCH = 256
TM_FFN = 256
TM_COMBINE = 256
TN_MOD = 1024
VMEM_LIMIT = 56 * 1024 * 1024


def _cp(*dims):
    return pltpu.CompilerParams(dimension_semantics=dims, vmem_limit_bytes=VMEM_LIMIT)


def _rms(x):
    return x * lax.rsqrt(jnp.mean(x * x, axis=-1, keepdims=True) + EPS)


def _gelu_tanh(x):
    c = math.sqrt(2.0 / math.pi)
    return x * (0.5 * (1.0 + jnp.tanh(c * (x + 0.044715 * (x * x * x)))))


def _sigmoid(x):
    return 1.0 / (1.0 + jnp.exp(-x))


def _mod_kernel(ct_ref, w_ref, b_ref, o_ref):
    ct = ct_ref[...]
    at = ct * _sigmoid(ct)
    w = w_ref[...]
    for b in range(o_ref.shape[0]):
        o_ref[b:b + 1, :] = jnp.sum(w * at[:, b:b + 1], axis=0, keepdims=True) + b_ref[...]


def _modulation(c, w_mod, b_mod):
    B, D = c.shape
    n_out = w_mod.shape[1]
    return pl.pallas_call(
        _mod_kernel,
        grid=(n_out // TN_MOD,),
        in_specs=[pl.BlockSpec((D, B), lambda j: (0, 0)),
                  pl.BlockSpec((D, TN_MOD), lambda j: (0, j)),
                  pl.BlockSpec((1, TN_MOD), lambda j: (0, j))],
        out_specs=pl.BlockSpec((B, TN_MOD), lambda j: (0, j)),
        out_shape=jax.ShapeDtypeStruct((B, n_out), F32),
        compiler_params=_cp("arbitrary"),
        name="mod",
    )(c.T, w_mod, b_mod.reshape(1, n_out))


def _inproj_kernel(x_ref, g_ref, sc_ref, sh_ref, w_ref, gv_ref, o_ref, h_ref):
    j = pl.program_id(1)

    @pl.when(j == 0)
    def _():
        h = (_rms(x_ref[...]) * g_ref[...]) * (1.0 + sc_ref[...]) + sh_ref[...]
        h_ref[...] = h.astype(BF16)

    acc = jnp.dot(h_ref[...], w_ref[...], preferred_element_type=F32)

    @pl.when(j == 0)
    def _():
        o_ref[...] = _gelu_tanh(acc).astype(BF16)

    @pl.when(j == 1)
    def _():
        v = _gelu_tanh(acc)
        vc = v - jnp.mean(v, axis=-1, keepdims=True)
        vn = vc * lax.rsqrt(jnp.mean(vc * vc, axis=-1, keepdims=True) + EPS)
        o_ref[...] = (vn * gv_ref[...]).astype(BF16)

    @pl.when(j >= 2)
    def _():
        o_ref[...] = acc.astype(BF16)


def _inproj(x2, mod3, g_pre, w_in_bf, g_v, seq):
    N, D = x2.shape
    d_in = w_in_bf.shape[1]
    tn = g_v.shape[1]
    tpb = seq // TM_INPROJ
    return pl.pallas_call(
        _inproj_kernel,
        grid=(N // TM_INPROJ, d_in // tn),
        in_specs=[pl.BlockSpec((TM_INPROJ, D), lambda i, j: (i, 0)),
                  pl.BlockSpec((1, D), lambda i, j: (0, 0)),
                  pl.BlockSpec((None, 1, D), lambda i, j: (i // tpb, 0, 1)),
                  pl.BlockSpec((None, 1, D), lambda i, j: (i // tpb, 0, 0)),
                  pl.BlockSpec((D, tn), lambda i, j: (0, j)),
                  pl.BlockSpec((1, tn), lambda i, j: (0, 0))],
        out_specs=pl.BlockSpec((TM_INPROJ, tn), lambda i, j: (i, j)),
        out_shape=jax.ShapeDtypeStruct((N, d_in), BF16),
        scratch_shapes=[pltpu.VMEM((TM_INPROJ, D), BF16)],
        compiler_params=_cp("arbitrary", "arbitrary"),
        name="inproj",
    )(x2, g_pre, mod3, mod3, w_in_bf, g_v)


def _gmlp_kernel(u_ref, v_ref, w_ref, bt_ref, g_ref, o_ref, ya_ref, *, heads, chunk, hd):
    tm = u_ref.shape[0]
    row = lax.broadcasted_iota(jnp.int32, (chunk, chunk), 0)
    col = lax.broadcasted_iota(jnp.int32, (chunk, chunk), 1)
    causal = col <= row
    for h in range(heads):
        wm = jnp.where(causal, w_ref[h], 0.0).astype(BF16)
        bcol = bt_ref[:, h:h + 1]
        cs = slice(h * hd, (h + 1) * hd)
        for c in range(tm // chunk):
            rs = slice(c * chunk, (c + 1) * chunk)
            sv = jnp.dot(wm, v_ref[rs, cs], preferred_element_type=F32) + bcol
            ya_ref[rs, cs] = u_ref[rs, cs].astype(F32) * sv
    o_ref[...] = (_rms(ya_ref[...]) * g_ref[...]).astype(BF16)


def _gmlp(pa, w_spatial, b_spatial, g_out):
    N = pa.shape[0]
    heads, chunk, _ = w_spatial.shape
    dg = g_out.shape[1]
    kern = functools.partial(_gmlp_kernel, heads=heads, chunk=chunk, hd=dg // heads)
    return pl.pallas_call(
        kern,
        grid=(N // TM_GMLP,),
        in_specs=[pl.BlockSpec((TM_GMLP, dg), lambda i: (i, 0)),
                  pl.BlockSpec((TM_GMLP, dg), lambda i: (i, 1)),
                  pl.BlockSpec((heads, chunk, chunk), lambda i: (0, 0, 0)),
                  pl.BlockSpec((chunk, heads), lambda i: (0, 0)),
                  pl.BlockSpec((1, dg), lambda i: (0, 0))],
        out_specs=pl.BlockSpec((TM_GMLP, dg), lambda i: (i, 0)),
        out_shape=jax.ShapeDtypeStruct((N, dg), BF16),
        scratch_shapes=[pltpu.VMEM((TM_GMLP, dg), F32)],
        compiler_params=_cp("arbitrary"),
        name="gmlp",
    )(pa, pa, w_spatial, b_spatial.T, g_out)


def _attn_kernel(q_ref, kc_ref, kp_ref, vc_ref, vp_ref, *rest, heads, hd, first):
    if first:
        o_ref, lse_ref = rest
    else:
        oo_ref, ol_ref, o_ref, lse_ref = rest
    n = pl.program_id(2)
    blk = ATTN_BLOCK
    qb = q_ref.shape[0]
    ri = lax.broadcasted_iota(jnp.int32, (blk, blk), 0)
    ci = lax.broadcasted_iota(jnp.int32, (blk, blk), 1)
    mask_p = ci >= ri
    mask_c = ci <= ri
    scale = hd ** -0.5
    c2 = scale * math.log2(math.e)
    lane = lax.broadcasted_iota(jnp.int32, (blk, LANES), 1)
    nt = (((1,), (1,)), ((), ()))
    for sb in range(qb // blk):
        rs = slice(sb * blk, (sb + 1) * blk)
        ps = slice((sb - 1) * blk, sb * blk)
        lse_tile = jnp.zeros((blk, LANES), F32)
        if not first:
            old_l = ol_ref[rs, :]
        for h in range(heads):
            cs = slice(h * hd, (h + 1) * hd)
            q = q_ref[rs, cs]
            kcur, vcur = kc_ref[rs, cs], vc_ref[rs, cs]
            if sb == 0:
                kprev, vprev = kp_ref[:, cs], vp_ref[:, cs]
            else:
                kprev, vprev = kc_ref[ps, cs], vc_ref[ps, cs]
            s_c = lax.dot_general(q, kcur, nt, preferred_element_type=F32)
            s_p = lax.dot_general(q, kprev, nt, preferred_element_type=F32)
            s_c = jnp.where(mask_c, s_c, NEG)
            s_p = jnp.where(mask_p, s_p, NEG)
            if sb == 0:
                s_p = jnp.where(n > 0, s_p, NEG)
            m = jnp.maximum(jnp.max(s_c, axis=-1, keepdims=True), jnp.max(s_p, axis=-1, keepdims=True))
            p_c = jnp.exp2((s_c - m) * c2)
            p_p = jnp.exp2((s_p - m) * c2)
            l = jnp.sum(p_c, axis=-1, keepdims=True) + jnp.sum(p_p, axis=-1, keepdims=True)
            acc = (jnp.dot(p_c.astype(BF16), vcur, preferred_element_type=F32)
                   + jnp.dot(p_p.astype(BF16), vprev, preferred_element_type=F32))
            o_b = acc / l
            lse_b = m * scale + jnp.log(l)
            if first:
                o_new, lse_new = o_b, lse_b
            else:
                lo = old_l[:, h:h + 1]
                mx = jnp.maximum(lo, lse_b)
                lse_new = mx + jnp.log(jnp.exp(lo - mx) + jnp.exp(lse_b - mx))
                o_new = oo_ref[rs, cs] * jnp.exp(lo - lse_new) + o_b * jnp.exp(lse_b - lse_new)
            o_ref[rs, cs] = o_new
            lse_tile = jnp.where(lane == h, lse_new, lse_tile)
        lse_ref[rs, :] = lse_tile


def _attn_pattern(pa3, prev, dil, heads, d_attn, col0):
    B, S, d_in = pa3.shape
    L = S // dil
    qb = min(QB_ATTN, L)
    nblk = qb // ATTN_BLOCK
    cpr = d_in // d_attn
    view = pa3.reshape(B, L, dil * d_in)
    qc, kc, vc = col0, col0 + 1, col0 + 2

    def cur(c):
        return pl.BlockSpec((None, qb, d_attn), lambda b, r, n: (b, n, r * cpr + c))

    def prv(c):
        return pl.BlockSpec((None, ATTN_BLOCK, d_attn),
                            lambda b, r, n: (b, jnp.maximum(n * nblk - 1, 0), r * cpr + c))

    o_spec = pl.BlockSpec((None, qb, d_attn), lambda b, r, n: (b, n, r))
    l_spec = pl.BlockSpec((None, qb, LANES), lambda b, r, n: (b, n, r))
    in_specs = [cur(qc), cur(kc), prv(kc), cur(vc), prv(vc)]
    args = [view, view, view, view, view]
    aliases = {}
    if prev is not None:
        in_specs += [o_spec, l_spec]
        args += [prev[0].reshape(B, L, dil * d_attn), prev[1].reshape(B, L, dil * LANES)]
        aliases = {5: 0, 6: 1}
    kern = functools.partial(_attn_kernel, heads=heads, hd=d_attn // heads, first=prev is None)
    o, lse = pl.pallas_call(
        kern,
        grid=(B, dil, L // qb),
        in_specs=in_specs,
        out_specs=[o_spec, l_spec],
        out_shape=[jax.ShapeDtypeStruct((B, L, dil * d_attn), F32),
                   jax.ShapeDtypeStruct((B, L, dil * LANES), F32)],
        input_output_aliases=aliases,
        compiler_params=_cp("arbitrary", "arbitrary", "arbitrary"),
        name=f"attn_d{dil}",
    )(*args)
    return o.reshape(B, S, d_attn), lse.reshape(B, S, LANES)


def _outproj_kernel(ya_ref, ob_ref, x_ref, w_ref, gattn_ref, gpost_ref, gate1_ref, gpre_ref,
                    sc2_ref, sh2_ref, wr_ref, br_ref, x1_ref, h2_ref, route_ref, cat_ref,
                    *, n_groups, epg):
    dg = ya_ref.shape[1]
    tm = ya_ref.shape[0]
    cat_ref[:, :dg] = ya_ref[...]
    cat_ref[:, dg:] = (_rms(ob_ref[...]) * gattn_ref[...]).astype(BF16)
    y = jnp.dot(cat_ref[...], w_ref[...], preferred_element_type=F32)
    x1 = x_ref[...] + gate1_ref[...] * (_rms(y) * gpost_ref[...])
    x1_ref[...] = x1
    h2 = (_rms(x1) * gpre_ref[...]) * (1.0 + sc2_ref[...]) + sh2_ref[...]
    h2_ref[...] = h2

    hi = h2.astype(BF16)
    lo = (h2 - hi.astype(F32)).astype(BF16)
    r = (jnp.dot(hi, wr_ref[...], preferred_element_type=F32)
         + jnp.dot(lo, wr_ref[...], preferred_element_type=F32))
    logits = r[:, :LANES] + r[:, LANES:] + br_ref[...]

    lane = lax.broadcasted_iota(jnp.int32, (tm, LANES), 1)
    lane_f = lane.astype(F32)
    big = float(LANES)
    lg = jnp.where(lane < n_groups, logits, NEG)
    mg = jnp.max(lg, axis=-1, keepdims=True)
    gi = jnp.min(jnp.where(lg == mg, lane_f, big), axis=-1, keepdims=True)
    gp = 1.0 / jnp.sum(jnp.exp(lg - mg), axis=-1, keepdims=True)
    e_lo = n_groups + gi * epg
    le = jnp.where(lane_f >= e_lo, jnp.where(lane_f < e_lo + epg, logits, NEG), NEG)
    m1 = jnp.max(le, axis=-1, keepdims=True)
    i1 = jnp.min(jnp.where(le == m1, lane_f, big), axis=-1, keepdims=True)
    le2 = jnp.where(lane_f == i1, NEG, le)
    m2 = jnp.max(le2, axis=-1, keepdims=True)
    i2 = jnp.min(jnp.where(le2 == m2, lane_f, big), axis=-1, keepdims=True)
    t = jnp.exp(m2 - m1)
    w1 = gp / (1.0 + t)
    w2 = gp * t / (1.0 + t)
    route_ref[...] = jnp.where(lane == 0, i1 - n_groups,
                               jnp.where(lane == 1, i2 - n_groups,
                                         jnp.where(lane == 2, w1, jnp.where(lane == 3, w2, 0.0))))


def _outproj(ya_n, ob, x2, w_out_bf, g_attn, g_post, mod3, g_pre, wr, br, seq, n_groups, epg):
    N, D = x2.shape
    dg = ya_n.shape[1]
    tm = TM_OUTPROJ
    tpb = seq // tm
    row = lambda i: (i, 0)
    const = lambda i: (0, 0)
    modc = lambda k: pl.BlockSpec((None, 1, D), lambda i: (i // tpb, 0, k))
    kern = functools.partial(_outproj_kernel, n_groups=n_groups, epg=epg)
    return pl.pallas_call(
        kern,
        grid=(N // tm,),
        in_specs=[pl.BlockSpec((tm, dg), row), pl.BlockSpec((tm, dg), row), pl.BlockSpec((tm, D), row),
                  pl.BlockSpec((D, D), const), pl.BlockSpec((1, dg), const), pl.BlockSpec((1, D), const),
                  modc(2), pl.BlockSpec((1, D), const), modc(4), modc(3),
                  pl.BlockSpec((D, 2 * LANES), const), pl.BlockSpec((1, LANES), const)],
        out_specs=[pl.BlockSpec((tm, D), row), pl.BlockSpec((tm, D), row), pl.BlockSpec((tm, LANES), row)],
        out_shape=[jax.ShapeDtypeStruct((N, D), F32), jax.ShapeDtypeStruct((N, D), F32),
                   jax.ShapeDtypeStruct((N, LANES), F32)],
        scratch_shapes=[pltpu.VMEM((tm, D), BF16)],
        compiler_params=_cp("arbitrary"),
        name="outproj",
    )(ya_n, ob, x2, w_out_bf, g_attn, g_post, mod3, g_pre, mod3, mod3, wr, br)


def _dispatch_kernel(dest_ref, h_ref, xs_hbm, sem):
    tm = h_ref.shape[0]

    def copy(t, d):
        return pltpu.make_async_copy(h_ref.at[pl.ds(t, 1)], xs_hbm.at[pl.ds(d, 1)], sem)

    def start(t, c):
        for s in range(TOP_K):
            copy(t, dest_ref[0, TOP_K * t + s]).start()
        return c

    def wait(t, c):
        for s in range(TOP_K):
            copy(t, dest_ref[0, TOP_K * t + s]).wait()
        return c

    lax.fori_loop(0, tm, start, 0, unroll=8)
    lax.fori_loop(0, tm, wait, 0, unroll=8)


def _dispatch(h2, dest, n_rows):
    N, D = h2.shape
    tm = TM_DISPATCH
    return pl.pallas_call(
        _dispatch_kernel,
        grid=(N // tm,),
        in_specs=[pl.BlockSpec((None, 1, TOP_K * tm), lambda i: (i, 0, 0), memory_space=pltpu.SMEM),
                  pl.BlockSpec((tm, D), lambda i: (i, 0))],
        out_specs=pl.BlockSpec(memory_space=pl.ANY),
        out_shape=jax.ShapeDtypeStruct((n_rows, D), F32),
        scratch_shapes=[pltpu.SemaphoreType.DMA(())],
        compiler_params=_cp("arbitrary"),
        name="dispatch",
    )(dest.reshape(N // tm, 1, TOP_K * tm), h2)


def _ffn_kernel(te_ref, tv_ref, ti_ref, to_ref, x_ref, wg_ref, wu_ref, wd_ref, y_ref):
    del te_ref, ti_ref, to_ref
    nv = tv_ref[pl.program_id(0)]

    @pl.when(nv > 0)
    def _():
        rows = lax.broadcasted_iota(jnp.int32, (x_ref.shape[0], 1), 0)
        x = jnp.where(rows < nv, x_ref[...], 0.0).astype(BF16)
        g = jnp.dot(x, wg_ref[...], preferred_element_type=F32)
        u = jnp.dot(x, wu_ref[...], preferred_element_type=F32)
        hm = ((g * _sigmoid(g)) * u).astype(BF16)
        y_ref[...] = jnp.dot(hm, wd_ref[...], preferred_element_type=F32)

    @pl.when(nv == 0)
    def _():
        y_ref[...] = jnp.zeros_like(y_ref)


def _ffn(xs, tile_e, tile_valid, tile_in, tile_out, wg, wu, wd):
    P, D = xs.shape
    de = wg.shape[2]
    tm = TM_FFN
    nt = tile_e.shape[0]
    grid_spec = pltpu.PrefetchScalarGridSpec(
        num_scalar_prefetch=4,
        grid=(nt,),
        in_specs=[pl.BlockSpec((tm, D), lambda t, te, tv, ti, to: (ti[t], 0)),
                  pl.BlockSpec((None, D, de), lambda t, te, tv, ti, to: (te[t], 0, 0)),
                  pl.BlockSpec((None, D, de), lambda t, te, tv, ti, to: (te[t], 0, 0)),
                  pl.BlockSpec((None, de, D), lambda t, te, tv, ti, to: (te[t], 0, 0))],
        out_specs=pl.BlockSpec((tm, D), lambda t, te, tv, ti, to: (to[t], 0)),
    )
    return pl.pallas_call(
        _ffn_kernel,
        grid_spec=grid_spec,
        out_shape=jax.ShapeDtypeStruct((P, D), F32),
        compiler_params=_cp("arbitrary"),
        name="ffn",
    )(tile_e, tile_valid, tile_in, tile_out, xs, wg, wu, wd)


def _combine_kernel(dest_ref, route_ref, x1_ref, gate2_ref, g_ref, ys_hbm, o_ref, buf, sem):
    tm = x1_ref.shape[0]

    def copy(t, s, d):
        return pltpu.make_async_copy(ys_hbm.at[pl.ds(d, 1)], buf.at[s, pl.ds(t, 1)], sem)

    def start(t, c):
        for s in range(TOP_K):
            copy(t, s, dest_ref[0, TOP_K * t + s]).start()
        return c

    def wait(t, c):
        for s in range(TOP_K):
            copy(t, s, dest_ref[0, TOP_K * t + s]).wait()
        return c

    lax.fori_loop(0, tm, start, 0, unroll=8)
    lax.fori_loop(0, tm, wait, 0, unroll=8)
    y = buf[0] * route_ref[:, 2:3] + buf[1] * route_ref[:, 3:4]
    o_ref[...] = x1_ref[...] + gate2_ref[...] * (_rms(y) * g_ref[...])


def _combine(ys, dest, route, x1, mod3, g_post, seq):
    N, D = x1.shape
    tm = TM_COMBINE
    tpb = seq // tm
    return pl.pallas_call(
        _combine_kernel,
        grid=(N // tm,),
        in_specs=[pl.BlockSpec((None, 1, TOP_K * tm), lambda i: (i, 0, 0), memory_space=pltpu.SMEM),
                  pl.BlockSpec((tm, LANES), lambda i: (i, 0)),
                  pl.BlockSpec((tm, D), lambda i: (i, 0)),
                  pl.BlockSpec((None, 1, D), lambda i: (i // tpb, 0, 5)),
                  pl.BlockSpec((1, D), lambda i: (0, 0)),
                  pl.BlockSpec(memory_space=pl.ANY)],
        out_specs=pl.BlockSpec((tm, D), lambda i: (i, 0)),
        out_shape=jax.ShapeDtypeStruct((N, D), F32),
        scratch_shapes=[pltpu.VMEM((TOP_K, tm, D), F32), pltpu.SemaphoreType.DMA(())],
        compiler_params=_cp("arbitrary"),
        name="combine",
    )(dest.reshape(N // tm, 1, TOP_K * tm), route, x1, mod3, g_post, ys)


def _routing_tables(route, n_experts, tm):
    N = route.shape[0]
    A = N * TOP_K
    e_flat = route[:, :TOP_K].astype(jnp.int32).reshape(A)
    onehot = (e_flat[:, None] == jnp.arange(n_experts, dtype=jnp.int32)[None, :]).astype(jnp.int32)
    csum = jnp.cumsum(onehot, axis=0)
    rank = jnp.take_along_axis(csum, e_flat[:, None], axis=1)[:, 0] - 1
    counts = csum[-1]
    padded = ((counts + tm - 1) // tm) * tm
    pend = jnp.cumsum(padded)
    pstart = pend - padded
    dest = pstart[e_flat] + rank
    nt = A // tm + n_experts
    n_used = pend[-1] // tm
    tidx = jnp.arange(nt, dtype=jnp.int32)
    tstart = tidx * tm
    te = jnp.minimum(jnp.searchsorted(pend, tstart, side='right'), n_experts - 1).astype(jnp.int32)
    tv = jnp.clip(counts[te] - (tstart - pstart[te]), 0, tm).astype(jnp.int32)
    used = tidx < n_used
    last = jnp.maximum(n_used - 1, 0)
    te = jnp.where(used, te, te[last]).astype(jnp.int32)
    ti = jnp.where(used, tidx, last).astype(jnp.int32)
    to = tidx
    return dest.astype(jnp.int32), te, tv, ti, to, nt


def kernel(x, c, w_mod, b_mod, g_pre_mix, g_post_mix, w_in, g_gmlp_v, w_spatial, b_spatial, g_out_gmlp,
           g_out_attn, w_out, g_pre_ffn, g_post_ffn, w_router_group, b_router_group, w_router_expert,
           b_router_expert, w_gate, w_up, w_down):
    B, S, D = x.shape
    N = B * S
    depth = w_mod.shape[0]
    d_gmlp = g_gmlp_v.shape[1]
    d_attn = g_out_attn.shape[1]
    heads = w_spatial.shape[1]
    n_groups, epg = b_router_expert.shape[1], b_router_expert.shape[2]
    n_experts = n_groups * epg
    assert d_gmlp == d_attn and w_in.shape[2] == 2 * d_gmlp + 3 * d_attn
    assert n_groups + n_experts <= LANES

    x2 = x.reshape(N, D)
    for l in range(depth):
        mod3 = _modulation(c, w_mod[l], b_mod[l]).reshape(B, 1, w_mod.shape[2])

        pa = _inproj(x2, mod3, g_pre_mix[l][None], w_in[l].astype(BF16), g_gmlp_v[l][None], S)
        ya_n = _gmlp(pa, w_spatial[l], b_spatial[l], g_out_gmlp[l][None])
        pa3 = pa.reshape(B, S, pa.shape[1])
        acc = None
        for _, dil in DILATION_PATTERNS:
            acc = _attn_pattern(pa3, acc, dil, heads, d_attn, (2 * d_gmlp) // d_attn)
        ob = acc[0].reshape(N, d_attn)

        wr32 = jnp.concatenate([w_router_group[l],
                                jnp.transpose(w_router_expert[l], (1, 0, 2)).reshape(D, n_experts)], axis=1)
        wr32 = jnp.pad(wr32, ((0, 0), (0, LANES - wr32.shape[1])))
        wr_hi = wr32.astype(BF16)
        wr = jnp.concatenate([wr_hi, (wr32 - wr_hi.astype(F32)).astype(BF16)], axis=1)
        br = jnp.pad(jnp.concatenate([b_router_group[l], b_router_expert[l].reshape(n_experts)]),
                     (0, LANES - n_groups - n_experts))[None]

        x1, h2, route = _outproj(ya_n, ob, x2, w_out[l].astype(BF16), g_out_attn[l][None], g_post_mix[l][None],
                                 mod3, g_pre_ffn[l][None], wr, br, S, n_groups, epg)

        dest, te, tv, ti, to, nt = _routing_tables(route, n_experts, TM_FFN)
        xs = _dispatch(h2, dest, nt * TM_FFN)
        ys = _ffn(xs, te, tv, ti, to, w_gate[l].astype(BF16), w_up[l].astype(BF16), w_down[l].astype(BF16))
        x2 = _combine(ys, dest, route, x1, mod3, g_post_ffn[l][None], S)
    return x2.reshape(B, S, D)
```

```python
import functools
import math

import jax
import jax.numpy as jnp
from jax import lax
from jax.experimental import pallas as pl
from jax.experimental.pallas import tpu as pltpu

F32 = jnp.float32
BF16 = jnp.bfloat16
EPS = 1e-6
NEG = -1e30

DILATION_PATTERNS = ((128, 1), (512, 4), (2048, 16))
ATTN_BLOCK = 128
TOP_K = 2
LANES = 128

TM_INPROJ = 512
TM_GMLP = 512
ATTN_WINDOW = ATTN_BLOCK * max(d for _, d in DILATION_PATTERNS)
ATTN_GROUP = 256
ATTN_HEADS_PER_STEP = 2
ATTN_UNROLL = 4
TM_OUTPROJ = 256
TM_DISPATCH = 256
TM_FFN = 256
TM_COMBINE = 256
TN_MOD = 1024
VMEM_LIMIT = 56 * 1024 * 1024


def _cp(*dims):
    return pltpu.CompilerParams(dimension_semantics=dims, vmem_limit_bytes=VMEM_LIMIT)


def _rms(x):
    return x * lax.rsqrt(jnp.mean(x * x, axis=-1, keepdims=True) + EPS)


def _gelu_tanh(x):
    c = math.sqrt(2.0 / math.pi)
    return x * (0.5 * (1.0 + jnp.tanh(c * (x + 0.044715 * (x * x * x)))))


def _sigmoid(x):
    return 1.0 / (1.0 + jnp.exp(-x))


def _mod_kernel(ct_ref, w_ref, b_ref, o_ref):
    ct = ct_ref[...]
    at = ct * _sigmoid(ct)
    w = w_ref[...]
    for b in range(o_ref.shape[0]):
        o_ref[b:b + 1, :] = jnp.sum(w * at[:, b:b + 1], axis=0, keepdims=True) + b_ref[...]


def _modulation(c, w_mod, b_mod):
    B, D = c.shape
    n_out = w_mod.shape[1]
    return pl.pallas_call(
        _mod_kernel,
        grid=(n_out // TN_MOD,),
        in_specs=[pl.BlockSpec((D, B), lambda j: (0, 0)),
                  pl.BlockSpec((D, TN_MOD), lambda j: (0, j)),
                  pl.BlockSpec((1, TN_MOD), lambda j: (0, j))],
        out_specs=pl.BlockSpec((B, TN_MOD), lambda j: (0, j)),
        out_shape=jax.ShapeDtypeStruct((B, n_out), F32),
        compiler_params=_cp("arbitrary"),
        name="mod",
    )(c.T, w_mod, b_mod.reshape(1, n_out))


def _inproj_kernel(x_ref, g_ref, sc_ref, sh_ref, w_ref, gv_ref, o_ref, h_ref):
    j = pl.program_id(1)

    @pl.when(j == 0)
    def _():
        h = (_rms(x_ref[...]) * g_ref[...]) * (1.0 + sc_ref[...]) + sh_ref[...]
        h_ref[...] = h.astype(BF16)

    acc = jnp.dot(h_ref[...], w_ref[...], preferred_element_type=F32)

    @pl.when(j == 0)
    def _():
        o_ref[...] = _gelu_tanh(acc).astype(BF16)

    @pl.when(j == 1)
    def _():
        v = _gelu_tanh(acc)
        vc = v - jnp.mean(v, axis=-1, keepdims=True)
        vn = vc * lax.rsqrt(jnp.mean(vc * vc, axis=-1, keepdims=True) + EPS)
        o_ref[...] = (vn * gv_ref[...]).astype(BF16)

    @pl.when(j >= 2)
    def _():
        o_ref[...] = acc.astype(BF16)


def _inproj(x2, mod3, g_pre, w_in_bf, g_v, seq):
    N, D = x2.shape
    d_in = w_in_bf.shape[1]
    tn = g_v.shape[1]
    tpb = seq // TM_INPROJ
    return pl.pallas_call(
        _inproj_kernel,
        grid=(N // TM_INPROJ, d_in // tn),
        in_specs=[pl.BlockSpec((TM_INPROJ, D), lambda i, j: (i, 0)),
                  pl.BlockSpec((1, D), lambda i, j: (0, 0)),
                  pl.BlockSpec((None, 1, D), lambda i, j: (i // tpb, 0, 1)),
                  pl.BlockSpec((None, 1, D), lambda i, j: (i // tpb, 0, 0)),
                  pl.BlockSpec((D, tn), lambda i, j: (0, j)),
                  pl.BlockSpec((1, tn), lambda i, j: (0, 0))],
        out_specs=pl.BlockSpec((TM_INPROJ, tn), lambda i, j: (i, j)),
        out_shape=jax.ShapeDtypeStruct((N, d_in), BF16),
        scratch_shapes=[pltpu.VMEM((TM_INPROJ, D), BF16)],
        compiler_params=_cp("arbitrary", "arbitrary"),
        name="inproj",
    )(x2, g_pre, mod3, mod3, w_in_bf, g_v)


def _gmlp_kernel(u_ref, v_ref, w_ref, bt_ref, g_ref, o_ref, ya_ref, *, heads, chunk, hd):
    tm = u_ref.shape[0]
    row = lax.broadcasted_iota(jnp.int32, (chunk, chunk), 0)
    col = lax.broadcasted_iota(jnp.int32, (chunk, chunk), 1)
    causal = col <= row
    for h in range(heads):
        wm = jnp.where(causal, w_ref[h], 0.0).astype(BF16)
        bcol = bt_ref[:, h:h + 1]
        cs = slice(h * hd, (h + 1) * hd)
        for c in range(tm // chunk):
            rs = slice(c * chunk, (c + 1) * chunk)
            sv = jnp.dot(wm, v_ref[rs, cs], preferred_element_type=F32) + bcol
            ya_ref[rs, cs] = u_ref[rs, cs].astype(F32) * sv
    o_ref[...] = (_rms(ya_ref[...]) * g_ref[...]).astype(BF16)


def _gmlp(pa, w_spatial, b_spatial, g_out):
    N = pa.shape[0]
    heads, chunk, _ = w_spatial.shape
    dg = g_out.shape[1]
    kern = functools.partial(_gmlp_kernel, heads=heads, chunk=chunk, hd=dg // heads)
    return pl.pallas_call(
        kern,
        grid=(N // TM_GMLP,),
        in_specs=[pl.BlockSpec((TM_GMLP, dg), lambda i: (i, 0)),
                  pl.BlockSpec((TM_GMLP, dg), lambda i: (i, 1)),
                  pl.BlockSpec((heads, chunk, chunk), lambda i: (0, 0, 0)),
                  pl.BlockSpec((chunk, heads), lambda i: (0, 0)),
                  pl.BlockSpec((1, dg), lambda i: (0, 0))],
        out_specs=pl.BlockSpec((TM_GMLP, dg), lambda i: (i, 0)),
        out_shape=jax.ShapeDtypeStruct((N, dg), BF16),
        scratch_shapes=[pltpu.VMEM((TM_GMLP, dg), F32)],
        compiler_params=_cp("arbitrary"),
        name="gmlp",
    )(pa, pa, w_spatial, b_spatial.T, g_out)


def _perm_matrix(d, n):
    i = jnp.arange(n)
    src = (i % (n // d)) * d + i // (n // d)
    return (src[:, None] == jnp.arange(n)[None, :]).astype(BF16)


def _attn_kernel(q_ref, kc_ref, kp_ref, vc_ref, vp_ref, p4_ref, p16_ref, o_ref,
                 qd_ref, kd_ref, va_ref, op_ref, lp_ref, bias_ref, *, hd, dils):
    n = pl.program_id(2)
    blk = ATTN_BLOCK
    grp = ATTN_GROUP
    W = q_ref.shape[0]
    nh = q_ref.shape[1] // hd
    scale = hd ** -0.5
    c2 = scale * math.log2(math.e)
    nt = (((1,), (1,)), ((), ()))

    ri = lax.broadcasted_iota(jnp.int32, (blk, 2 * blk), 0)
    ci = lax.broadcasted_iota(jnp.int32, (blk, 2 * blk), 1)
    band = jnp.where(ci >= ri, jnp.where(ci <= ri + blk, 0.0, NEG), NEG)
    bias_ref[0] = band
    bias_ref[1] = jnp.where(n > 0, band, jnp.where(ci < blk, NEG, band))

    def put_v(rows, x):
        ones = jnp.ones((x.shape[0], hd), BF16)
        for hh in range(nh):
            parts = [x[:, hh * hd:(hh + 1) * hd] if c == hh else ones for c in range(nh)]
            va_ref[hh, rows, :] = jnp.concatenate(parts, axis=1)

    def head_blocks(items):
        ss = [lax.dot_general(lq(), lk(), nt, preferred_element_type=F32) + lb()
              for lq, lk, _, _, lb, _ in items]
        ms = [jnp.max(s, axis=-1, keepdims=True) for s in ss]
        ps = [jnp.exp2((s - m) * c2).astype(BF16) for s, m in zip(ss, ms)]
        for p, m, (_, _, lva, hh, _, store) in zip(ps, ms, items):
            r = jnp.dot(p, lva(), preferred_element_type=F32)
            oh = (hh + 1) % nh
            l = r[:, oh * hd:(oh + 1) * hd]
            store(r[:, hh * hd:(hh + 1) * hd] / l, m * scale + jnp.log(l))

    def run_pattern(pi, d):
        ql = W // d
        kl = ql + blk
        nb = ql // blk
        unroll = ATTN_UNROLL
        assert nb % unroll == 0 or unroll % nb == 0

        def aligned(x):
            return x if isinstance(x, int) else pl.multiple_of(x, blk)

        def loop_body(it, carry):
            items = []
            for u in range(unroll):
                if nb >= unroll:
                    bodies_per_class = nb // unroll
                    r = 0 if d == 1 else it // bodies_per_class
                    jb = u if bodies_per_class == 1 else (it % bodies_per_class) * unroll + u
                else:
                    r = it * (unroll // nb) + u // nb
                    jb = u % nb
                qrow = aligned(r * ql + jb * blk)
                krow = aligned(r * kl + jb * blk)
                first_blk = jb == 0
                orow = aligned(jb * blk) if d == 1 else jb * (blk * d) + r
                for hh in range(nh):
                    cs = slice(hh * hd, (hh + 1) * hd)

                    def store(o, lse, hh=hh, orow=orow):
                        if d == 1:
                            rows = pl.ds(orow, blk)
                        else:
                            rows = pl.ds(orow, blk, stride=d)
                        op_ref[pi, hh, rows, :] = o
                        lp_ref[pi, hh, rows, :] = lse

                    items.append((
                        lambda qrow=qrow, cs=cs: (q_ref if d == 1 else qd_ref)[pl.ds(qrow, blk), cs],
                        lambda krow=krow, cs=cs: kd_ref[pl.ds(krow, 2 * blk), cs],
                        lambda krow=krow, hh=hh: va_ref[hh, pl.ds(krow, 2 * blk), :],
                        hh,
                        lambda first_blk=first_blk: bias_ref[
                            int(first_blk) if isinstance(first_blk, bool) else jnp.where(first_blk, 1, 0)],
                        store))
            head_blocks(items)
            return carry

        lax.fori_loop(0, (d * nb) // unroll, loop_body, 0)

    def deinterleave(d, p_ref):
        pc = grp // d
        ql = W // d
        kl = ql + blk
        ng = W // grp
        pm = p_ref[...]

        def split(x):
            return jnp.dot(pm, x, preferred_element_type=F32).astype(BF16)

        for g in range(ng):
            rows = slice(g * grp, (g + 1) * grp)
            yq, yk, yv = split(q_ref[rows, :]), split(kc_ref[rows, :]), split(vc_ref[rows, :])
            for r in range(d):
                piece = slice(r * pc, (r + 1) * pc)
                qd_ref[r * ql + g * pc:r * ql + (g + 1) * pc, :] = yq[piece]
                kd_ref[r * kl + blk + g * pc:r * kl + blk + (g + 1) * pc, :] = yk[piece]
                put_v(slice(r * kl + blk + g * pc, r * kl + blk + (g + 1) * pc), yv[piece])
        g0 = ng - (blk * d) // grp
        for g in range(g0, ng):
            rows = slice(g * grp, (g + 1) * grp)
            yk, yv = split(kp_ref[rows, :]), split(vp_ref[rows, :])
            for r in range(d):
                piece = slice(r * pc, (r + 1) * pc)
                dst = r * kl + (g - g0) * pc
                kd_ref[dst:dst + pc, :] = yk[piece]
                put_v(slice(dst, dst + pc), yv[piece])

    for pi, d in enumerate(dils):
        if d == 1:
            kd_ref[0:blk, :] = kp_ref[W - blk:W, :]
            kd_ref[blk:blk + W, :] = kc_ref[...]
            put_v(slice(0, blk), vp_ref[W - blk:W, :])
            put_v(slice(blk, blk + W), vc_ref[...])
        else:
            deinterleave(d, p4_ref if d == 4 else p16_ref)
        run_pattern(pi, d)

    fin = 2 * blk

    def fin_body(c, carry):
        rows = pl.ds(pl.multiple_of(c * fin, fin), fin)
        for hh in range(nh):
            ls = [lp_ref[pi, hh, rows, :] for pi in range(len(dils))]
            mx = functools.reduce(jnp.maximum, ls)
            ws = [jnp.exp(l - mx) for l in ls]
            num = sum(w * op_ref[pi, hh, rows, :] for pi, w in enumerate(ws))
            o_ref[rows, hh * hd:(hh + 1) * hd] = num / sum(ws)
        return carry

    lax.fori_loop(0, W // fin, fin_body, 0)


def _attention(pa3, heads, d_attn, col0):
    B, S, _ = pa3.shape
    hd = d_attn // heads
    dils = tuple(d for _, d in DILATION_PATTERNS)
    assert all(w == ATTN_BLOCK * d for w, d in DILATION_PATTERNS) and dils == (1, 4, 16)
    W = ATTN_WINDOW
    lanes = ATTN_HEADS_PER_STEP * hd
    cb = d_attn // lanes
    assert S % W == 0 and W % ATTN_GROUP == 0 and hd == LANES

    def cur(c):
        return pl.BlockSpec((None, W, lanes), lambda b, hp, n: (b, n, (col0 + c) * cb + hp))

    def prv(c):
        return pl.BlockSpec((None, W, lanes), lambda b, hp, n: (b, jnp.maximum(n - 1, 0), (col0 + c) * cb + hp))

    perm = pl.BlockSpec((ATTN_GROUP, ATTN_GROUP), lambda b, hp, n: (0, 0))
    kern = functools.partial(_attn_kernel, hd=hd, dils=dils)
    npat = len(dils)
    return pl.pallas_call(
        kern,
        grid=(B, cb, S // W),
        in_specs=[cur(0), cur(1), prv(1), cur(2), prv(2), perm, perm],
        out_specs=pl.BlockSpec((None, W, lanes), lambda b, hp, n: (b, n, hp)),
        out_shape=jax.ShapeDtypeStruct((B, S, d_attn), F32),
        scratch_shapes=[pltpu.VMEM((W, lanes), BF16),
                        pltpu.VMEM((2 * W, lanes), BF16),
                        pltpu.VMEM((ATTN_HEADS_PER_STEP, 2 * W, lanes), BF16),
                        pltpu.VMEM((npat, ATTN_HEADS_PER_STEP, W, hd), F32),
                        pltpu.VMEM((npat, ATTN_HEADS_PER_STEP, W, hd), F32),
                        pltpu.VMEM((2, ATTN_BLOCK, 2 * ATTN_BLOCK), F32)],
        compiler_params=_cp("arbitrary", "arbitrary", "arbitrary"),
        name="attn",
    )(pa3, pa3, pa3, pa3, pa3, _perm_matrix(4, ATTN_GROUP), _perm_matrix(16, ATTN_GROUP))


def _outproj_kernel(ya_ref, ob_ref, x_ref, w_ref, gattn_ref, gpost_ref, gate1_ref, gpre_ref,
                    sc2_ref, sh2_ref, wr_ref, br_ref, x1_ref, h2_ref, route_ref, cat_ref,
                    *, n_groups, epg):
    dg = ya_ref.shape[1]
    tm = ya_ref.shape[0]
    cat_ref[:, :dg] = ya_ref[...]
    cat_ref[:, dg:] = (_rms(ob_ref[...]) * gattn_ref[...]).astype(BF16)
    y = jnp.dot(cat_ref[...], w_ref[...], preferred_element_type=F32)
    x1 = x_ref[...] + gate1_ref[...] * (_rms(y) * gpost_ref[...])
    x1_ref[...] = x1
    h2 = (_rms(x1) * gpre_ref[...]) * (1.0 + sc2_ref[...]) + sh2_ref[...]
    h2_ref[...] = h2

    hi = h2.astype(BF16)
    lo = (h2 - hi.astype(F32)).astype(BF16)
    r = (jnp.dot(hi, wr_ref[...], preferred_element_type=F32)
         + jnp.dot(lo, wr_ref[...], preferred_element_type=F32))
    logits = r[:, :LANES] + r[:, LANES:] + br_ref[...]

    lane = lax.broadcasted_iota(jnp.int32, (tm, LANES), 1)
    lane_f = lane.astype(F32)
    big = float(LANES)
    lg = jnp.where(lane < n_groups, logits, NEG)
    mg = jnp.max(lg, axis=-1, keepdims=True)
    gi = jnp.min(jnp.where(lg == mg, lane_f, big), axis=-1, keepdims=True)
    gp = 1.0 / jnp.sum(jnp.exp(lg - mg), axis=-1, keepdims=True)
    e_lo = n_groups + gi * epg
    le = jnp.where(lane_f >= e_lo, jnp.where(lane_f < e_lo + epg, logits, NEG), NEG)
    m1 = jnp.max(le, axis=-1, keepdims=True)
    i1 = jnp.min(jnp.where(le == m1, lane_f, big), axis=-1, keepdims=True)
    le2 = jnp.where(lane_f == i1, NEG, le)
    m2 = jnp.max(le2, axis=-1, keepdims=True)
    i2 = jnp.min(jnp.where(le2 == m2, lane_f, big), axis=-1, keepdims=True)
    t = jnp.exp(m2 - m1)
    w1 = gp / (1.0 + t)
    w2 = gp * t / (1.0 + t)
    route_ref[...] = jnp.where(lane == 0, i1 - n_groups,
                               jnp.where(lane == 1, i2 - n_groups,
                                         jnp.where(lane == 2, w1, jnp.where(lane == 3, w2, 0.0))))


def _outproj(ya_n, ob, x2, w_out_bf, g_attn, g_post, mod3, g_pre, wr, br, seq, n_groups, epg):
    N, D = x2.shape
    dg = ya_n.shape[1]
    tm = TM_OUTPROJ
    tpb = seq // tm
    row = lambda i: (i, 0)
    const = lambda i: (0, 0)
    modc = lambda k: pl.BlockSpec((None, 1, D), lambda i: (i // tpb, 0, k))
    kern = functools.partial(_outproj_kernel, n_groups=n_groups, epg=epg)
    return pl.pallas_call(
        kern,
        grid=(N // tm,),
        in_specs=[pl.BlockSpec((tm, dg), row), pl.BlockSpec((tm, dg), row), pl.BlockSpec((tm, D), row),
                  pl.BlockSpec((D, D), const), pl.BlockSpec((1, dg), const), pl.BlockSpec((1, D), const),
                  modc(2), pl.BlockSpec((1, D), const), modc(4), modc(3),
                  pl.BlockSpec((D, 2 * LANES), const), pl.BlockSpec((1, LANES), const)],
        out_specs=[pl.BlockSpec((tm, D), row), pl.BlockSpec((tm, D), row), pl.BlockSpec((tm, LANES), row)],
        out_shape=[jax.ShapeDtypeStruct((N, D), F32), jax.ShapeDtypeStruct((N, D), F32),
                   jax.ShapeDtypeStruct((N, LANES), F32)],
        scratch_shapes=[pltpu.VMEM((tm, D), BF16)],
        compiler_params=_cp("arbitrary"),
        name="outproj",
    )(ya_n, ob, x2, w_out_bf, g_attn, g_post, mod3, g_pre, mod3, mod3, wr, br)


def _dispatch_kernel(dest_ref, h_ref, xs_hbm, sem):
    tm = h_ref.shape[0]

    def copy(t, d):
        return pltpu.make_async_copy(h_ref.at[pl.ds(t, 1)], xs_hbm.at[pl.ds(d, 1)], sem)

    def start(t, c):
        for s in range(TOP_K):
            copy(t, dest_ref[0, TOP_K * t + s]).start()
        return c

    def wait(t, c):
        for s in range(TOP_K):
            copy(t, dest_ref[0, TOP_K * t + s]).wait()
        return c

    lax.fori_loop(0, tm, start, 0, unroll=8)
    lax.fori_loop(0, tm, wait, 0, unroll=8)


def _dispatch(h2, dest, n_rows):
    N, D = h2.shape
    tm = TM_DISPATCH
    return pl.pallas_call(
        _dispatch_kernel,
        grid=(N // tm,),
        in_specs=[pl.BlockSpec((None, 1, TOP_K * tm), lambda i: (i, 0, 0), memory_space=pltpu.SMEM),
                  pl.BlockSpec((tm, D), lambda i: (i, 0))],
        out_specs=pl.BlockSpec(memory_space=pl.ANY),
        out_shape=jax.ShapeDtypeStruct((n_rows, D), F32),
        scratch_shapes=[pltpu.SemaphoreType.DMA(())],
        compiler_params=_cp("arbitrary"),
        name="dispatch",
    )(dest.reshape(N // tm, 1, TOP_K * tm), h2)


def _ffn_kernel(te_ref, tv_ref, ti_ref, to_ref, x_ref, wg_ref, wu_ref, wd_ref, y_ref):
    del te_ref, ti_ref, to_ref
    nv = tv_ref[pl.program_id(0)]

    @pl.when(nv > 0)
    def _():
        rows = lax.broadcasted_iota(jnp.int32, (x_ref.shape[0], 1), 0)
        x = jnp.where(rows < nv, x_ref[...], 0.0).astype(BF16)
        g = jnp.dot(x, wg_ref[...], preferred_element_type=F32)
        u = jnp.dot(x, wu_ref[...], preferred_element_type=F32)
        hm = ((g * _sigmoid(g)) * u).astype(BF16)
        y_ref[...] = jnp.dot(hm, wd_ref[...], preferred_element_type=F32)

    @pl.when(nv == 0)
    def _():
        y_ref[...] = jnp.zeros_like(y_ref)


def _ffn(xs, tile_e, tile_valid, tile_in, tile_out, wg, wu, wd):
    P, D = xs.shape
    de = wg.shape[2]
    tm = TM_FFN
    nt = tile_e.shape[0]
    grid_spec = pltpu.PrefetchScalarGridSpec(
        num_scalar_prefetch=4,
        grid=(nt,),
        in_specs=[pl.BlockSpec((tm, D), lambda t, te, tv, ti, to: (ti[t], 0)),
                  pl.BlockSpec((None, D, de), lambda t, te, tv, ti, to: (te[t], 0, 0)),
                  pl.BlockSpec((None, D, de), lambda t, te, tv, ti, to: (te[t], 0, 0)),
                  pl.BlockSpec((None, de, D), lambda t, te, tv, ti, to: (te[t], 0, 0))],
        out_specs=pl.BlockSpec((tm, D), lambda t, te, tv, ti, to: (to[t], 0)),
    )
    return pl.pallas_call(
        _ffn_kernel,
        grid_spec=grid_spec,
        out_shape=jax.ShapeDtypeStruct((P, D), F32),
        compiler_params=_cp("arbitrary"),
        name="ffn",
    )(tile_e, tile_valid, tile_in, tile_out, xs, wg, wu, wd)


def _combine_kernel(dest_ref, route_ref, x1_ref, gate2_ref, g_ref, ys_hbm, o_ref, buf, sem):
    tm = x1_ref.shape[0]

    def copy(t, s, d):
        return pltpu.make_async_copy(ys_hbm.at[pl.ds(d, 1)], buf.at[s, pl.ds(t, 1)], sem)

    def start(t, c):
        for s in range(TOP_K):
            copy(t, s, dest_ref[0, TOP_K * t + s]).start()
        return c

    def wait(t, c):
        for s in range(TOP_K):
            copy(t, s, dest_ref[0, TOP_K * t + s]).wait()
        return c

    lax.fori_loop(0, tm, start, 0, unroll=8)
    lax.fori_loop(0, tm, wait, 0, unroll=8)
    y = buf[0] * route_ref[:, 2:3] + buf[1] * route_ref[:, 3:4]
    o_ref[...] = x1_ref[...] + gate2_ref[...] * (_rms(y) * g_ref[...])


def _combine(ys, dest, route, x1, mod3, g_post, seq):
    N, D = x1.shape
    tm = TM_COMBINE
    tpb = seq // tm
    return pl.pallas_call(
        _combine_kernel,
        grid=(N // tm,),
        in_specs=[pl.BlockSpec((None, 1, TOP_K * tm), lambda i: (i, 0, 0), memory_space=pltpu.SMEM),
                  pl.BlockSpec((tm, LANES), lambda i: (i, 0)),
                  pl.BlockSpec((tm, D), lambda i: (i, 0)),
                  pl.BlockSpec((None, 1, D), lambda i: (i // tpb, 0, 5)),
                  pl.BlockSpec((1, D), lambda i: (0, 0)),
                  pl.BlockSpec(memory_space=pl.ANY)],
        out_specs=pl.BlockSpec((tm, D), lambda i: (i, 0)),
        out_shape=jax.ShapeDtypeStruct((N, D), F32),
        scratch_shapes=[pltpu.VMEM((TOP_K, tm, D), F32), pltpu.SemaphoreType.DMA(())],
        compiler_params=_cp("arbitrary"),
        name="combine",
    )(dest.reshape(N // tm, 1, TOP_K * tm), route, x1, mod3, g_post, ys)


def _routing_tables(route, n_experts, tm):
    N = route.shape[0]
    A = N * TOP_K
    e_flat = route[:, :TOP_K].astype(jnp.int32).reshape(A)
    onehot = (e_flat[:, None] == jnp.arange(n_experts, dtype=jnp.int32)[None, :]).astype(jnp.int32)
    csum = jnp.cumsum(onehot, axis=0)
    rank = jnp.take_along_axis(csum, e_flat[:, None], axis=1)[:, 0] - 1
    counts = csum[-1]
    padded = ((counts + tm - 1) // tm) * tm
    pend = jnp.cumsum(padded)
    pstart = pend - padded
    dest = pstart[e_flat] + rank
    nt = A // tm + n_experts
    n_used = pend[-1] // tm
    tidx = jnp.arange(nt, dtype=jnp.int32)
    tstart = tidx * tm
    te = jnp.minimum(jnp.searchsorted(pend, tstart, side='right'), n_experts - 1).astype(jnp.int32)
    tv = jnp.clip(counts[te] - (tstart - pstart[te]), 0, tm).astype(jnp.int32)
    used = tidx < n_used
    last = jnp.maximum(n_used - 1, 0)
    te = jnp.where(used, te, te[last]).astype(jnp.int32)
    ti = jnp.where(used, tidx, last).astype(jnp.int32)
    to = tidx
    return dest.astype(jnp.int32), te, tv, ti, to, nt


def kernel(x, c, w_mod, b_mod, g_pre_mix, g_post_mix, w_in, g_gmlp_v, w_spatial, b_spatial, g_out_gmlp,
           g_out_attn, w_out, g_pre_ffn, g_post_ffn, w_router_group, b_router_group, w_router_expert,
           b_router_expert, w_gate, w_up, w_down):
    B, S, D = x.shape
    N = B * S
    depth = w_mod.shape[0]
    d_gmlp = g_gmlp_v.shape[1]
    d_attn = g_out_attn.shape[1]
    heads = w_spatial.shape[1]
    n_groups, epg = b_router_expert.shape[1], b_router_expert.shape[2]
    n_experts = n_groups * epg
    assert d_gmlp == d_attn and w_in.shape[2] == 2 * d_gmlp + 3 * d_attn
    assert n_groups + n_experts <= LANES

    x2 = x.reshape(N, D)
    for l in range(depth):
        mod3 = _modulation(c, w_mod[l], b_mod[l]).reshape(B, 1, w_mod.shape[2])

        pa = _inproj(x2, mod3, g_pre_mix[l][None], w_in[l].astype(BF16), g_gmlp_v[l][None], S)
        ya_n = _gmlp(pa, w_spatial[l], b_spatial[l], g_out_gmlp[l][None])
        ob = _attention(pa.reshape(B, S, pa.shape[1]), heads, d_attn, (2 * d_gmlp) // d_attn).reshape(N, d_attn)

        wr32 = jnp.concatenate([w_router_group[l],
                                jnp.transpose(w_router_expert[l], (1, 0, 2)).reshape(D, n_experts)], axis=1)
        wr32 = jnp.pad(wr32, ((0, 0), (0, LANES - wr32.shape[1])))
        wr_hi = wr32.astype(BF16)
        wr = jnp.concatenate([wr_hi, (wr32 - wr_hi.astype(F32)).astype(BF16)], axis=1)
        br = jnp.pad(jnp.concatenate([b_router_group[l], b_router_expert[l].reshape(n_experts)]),
                     (0, LANES - n_groups - n_experts))[None]

        x1, h2, route = _outproj(ya_n, ob, x2, w_out[l].astype(BF16), g_out_attn[l][None], g_post_mix[l][None],
                                 mod3, g_pre_ffn[l][None], wr, br, S, n_groups, epg)

        dest, te, tv, ti, to, nt = _routing_tables(route, n_experts, TM_FFN)
        xs = _dispatch(h2, dest, nt * TM_FFN)
        ys = _ffn(xs, te, tv, ti, to, w_gate[l].astype(BF16), w_up[l].astype(BF16), w_down[l].astype(BF16))
        x2 = _combine(ys, dest, route, x1, mod3, g_post_ffn[l][None], S)
    return x2.reshape(B, S, D)
```

```python
import functools
import math

import jax
import jax.numpy as jnp
from jax import lax
from jax.experimental import pallas as pl
from jax.experimental.pallas import tpu as pltpu

F32 = jnp.float32
BF16 = jnp.bfloat16
EPS = 1e-6
NEG = -1e30

DILATION_PATTERNS = ((128, 1), (512, 4), (2048, 16))
ATTN_BLOCK = 128
TOP_K = 2
LANES = 128

TM_INPROJ = 512
TM_GMLP = 512
ATTN_WINDOW = ATTN_BLOCK * max(d for _, d in DILATION_PATTERNS)
ATTN_GROUP = 256
ATTN_HEADS_PER_STEP = 2
ATTN_UNROLL = 4
TM_OUTPROJ = 512
SUB_OUTPROJ = 256
TM_RANK = 512
TM_DISPATCH = 512
TM_FFN = 256
TM_COMBINE = 256
TN_MOD = 1024
ROW_SLAB = 8
VMEM_LIMIT = 56 * 1024 * 1024


def _cp(*dims):
    return pltpu.CompilerParams(dimension_semantics=dims, vmem_limit_bytes=VMEM_LIMIT)


def _rms(x):
    return x * lax.rsqrt(jnp.mean(x * x, axis=-1, keepdims=True) + EPS)


def _gelu_tanh(x):
    c = math.sqrt(2.0 / math.pi)
    return x * (0.5 * (1.0 + jnp.tanh(c * (x + 0.044715 * (x * x * x)))))


def _sigmoid(x):
    return 1.0 / (1.0 + jnp.exp(-x))


def _pack_rows(x):
    half = x.shape[1] // 2
    lo = pltpu.bitcast(x[:, :half].astype(BF16).astype(F32), jnp.uint32)
    hi = pltpu.bitcast(x[:, half:].astype(BF16).astype(F32), jnp.uint32)
    return lax.shift_right_logical(lo, jnp.uint32(16)) | (hi & jnp.uint32(0xFFFF0000))


def _unpack_lo(w):
    return pltpu.bitcast(lax.shift_left(w, jnp.uint32(16)), F32)


def _unpack_hi(w):
    return pltpu.bitcast(w & jnp.uint32(0xFFFF0000), F32)


def _store_slabs(ref, words, row0=0):
    rows = words.shape[0]
    for j in range(ROW_SLAB):
        ref[pl.ds(row0 * ROW_SLAB + j, rows, stride=ROW_SLAB), :] = words[:, j * LANES:(j + 1) * LANES]


def _mod_kernel(ct_ref, w_ref, b_ref, o_ref):
    ct = ct_ref[...]
    at = ct * _sigmoid(ct)
    w = w_ref[...]
    for b in range(o_ref.shape[0]):
        o_ref[b:b + 1, :] = jnp.sum(w * at[:, b:b + 1], axis=0, keepdims=True) + b_ref[...]


def _modulation(c, w_mod, b_mod):
    B, D = c.shape
    n_out = w_mod.shape[1]
    return pl.pallas_call(
        _mod_kernel,
        grid=(n_out // TN_MOD,),
        in_specs=[pl.BlockSpec((D, B), lambda j: (0, 0)),
                  pl.BlockSpec((D, TN_MOD), lambda j: (0, j)),
                  pl.BlockSpec((1, TN_MOD), lambda j: (0, j))],
        out_specs=pl.BlockSpec((B, TN_MOD), lambda j: (0, j)),
        out_shape=jax.ShapeDtypeStruct((B, n_out), F32),
        compiler_params=_cp("arbitrary"),
        name="mod",
    )(c.T, w_mod, b_mod.reshape(1, n_out))


def _inproj_kernel(x_ref, g_ref, sc_ref, sh_ref, w_ref, gv_ref, o_ref, h_ref):
    j = pl.program_id(1)

    @pl.when(j == 0)
    def _():
        h = (_rms(x_ref[...]) * g_ref[...]) * (1.0 + sc_ref[...]) + sh_ref[...]
        h_ref[...] = h.astype(BF16)

    acc = jnp.dot(h_ref[...], w_ref[...], preferred_element_type=F32)

    @pl.when(j == 0)
    def _():
        o_ref[...] = _gelu_tanh(acc).astype(BF16)

    @pl.when(j == 1)
    def _():
        v = _gelu_tanh(acc)
        vc = v - jnp.mean(v, axis=-1, keepdims=True)
        vn = vc * lax.rsqrt(jnp.mean(vc * vc, axis=-1, keepdims=True) + EPS)
        o_ref[...] = (vn * gv_ref[...]).astype(BF16)

    @pl.when(j >= 2)
    def _():
        o_ref[...] = acc.astype(BF16)


def _inproj(x2, mod3, g_pre, w_in_bf, g_v, seq):
    N, D = x2.shape
    d_in = w_in_bf.shape[1]
    tn = g_v.shape[1]
    tpb = seq // TM_INPROJ
    return pl.pallas_call(
        _inproj_kernel,
        grid=(N // TM_INPROJ, d_in // tn),
        in_specs=[pl.BlockSpec((TM_INPROJ, D), lambda i, j: (i, 0)),
                  pl.BlockSpec((1, D), lambda i, j: (0, 0)),
                  pl.BlockSpec((None, 1, D), lambda i, j: (i // tpb, 0, 1)),
                  pl.BlockSpec((None, 1, D), lambda i, j: (i // tpb, 0, 0)),
                  pl.BlockSpec((D, tn), lambda i, j: (0, j)),
                  pl.BlockSpec((1, tn), lambda i, j: (0, 0))],
        out_specs=pl.BlockSpec((TM_INPROJ, tn), lambda i, j: (i, j)),
        out_shape=jax.ShapeDtypeStruct((N, d_in), BF16),
        scratch_shapes=[pltpu.VMEM((TM_INPROJ, D), BF16)],
        compiler_params=_cp("arbitrary", "arbitrary"),
        name="inproj",
    )(x2, g_pre, mod3, mod3, w_in_bf, g_v)


def _gmlp_kernel(u_ref, v_ref, w_ref, bt_ref, g_ref, o_ref, ya_ref, *, heads, chunk, hd):
    tm = u_ref.shape[0]
    row = lax.broadcasted_iota(jnp.int32, (chunk, chunk), 0)
    col = lax.broadcasted_iota(jnp.int32, (chunk, chunk), 1)
    causal = col <= row
    for h in range(heads):
        wm = jnp.where(causal, w_ref[h], 0.0).astype(BF16)
        bcol = bt_ref[:, h:h + 1]
        cs = slice(h * hd, (h + 1) * hd)
        for c in range(tm // chunk):
            rs = slice(c * chunk, (c + 1) * chunk)
            sv = jnp.dot(wm, v_ref[rs, cs], preferred_element_type=F32) + bcol
            ya_ref[rs, cs] = u_ref[rs, cs].astype(F32) * sv
    o_ref[...] = (_rms(ya_ref[...]) * g_ref[...]).astype(BF16)


def _gmlp(pa, w_spatial, b_spatial, g_out):
    N = pa.shape[0]
    heads, chunk, _ = w_spatial.shape
    dg = g_out.shape[1]
    kern = functools.partial(_gmlp_kernel, heads=heads, chunk=chunk, hd=dg // heads)
    return pl.pallas_call(
        kern,
        grid=(N // TM_GMLP,),
        in_specs=[pl.BlockSpec((TM_GMLP, dg), lambda i: (i, 0)),
                  pl.BlockSpec((TM_GMLP, dg), lambda i: (i, 1)),
                  pl.BlockSpec((heads, chunk, chunk), lambda i: (0, 0, 0)),
                  pl.BlockSpec((chunk, heads), lambda i: (0, 0)),
                  pl.BlockSpec((1, dg), lambda i: (0, 0))],
        out_specs=pl.BlockSpec((TM_GMLP, dg), lambda i: (i, 0)),
        out_shape=jax.ShapeDtypeStruct((N, dg), BF16),
        scratch_shapes=[pltpu.VMEM((TM_GMLP, dg), F32)],
        compiler_params=_cp("arbitrary"),
        name="gmlp",
    )(pa, pa, w_spatial, b_spatial.T, g_out)


def _perm_matrix(d, n):
    i = jnp.arange(n)
    src = (i % (n // d)) * d + i // (n // d)
    return (src[:, None] == jnp.arange(n)[None, :]).astype(BF16)


def _attn_kernel(q_ref, kc_ref, kp_ref, vc_ref, vp_ref, p4_ref, p16_ref, o_ref,
                 qd_ref, kd_ref, va_ref, op_ref, lp_ref, bias_ref, *, hd, dils):
    n = pl.program_id(2)
    blk = ATTN_BLOCK
    grp = ATTN_GROUP
    W = q_ref.shape[0]
    nh = q_ref.shape[1] // hd
    scale = hd ** -0.5
    c2 = scale * math.log2(math.e)
    nt = (((1,), (1,)), ((), ()))

    ri = lax.broadcasted_iota(jnp.int32, (blk, 2 * blk), 0)
    ci = lax.broadcasted_iota(jnp.int32, (blk, 2 * blk), 1)
    band = jnp.where(ci >= ri, jnp.where(ci <= ri + blk, 0.0, NEG), NEG)
    bias_ref[0] = band
    bias_ref[1] = jnp.where(n > 0, band, jnp.where(ci < blk, NEG, band))

    def put_v(rows, x):
        ones = jnp.ones((x.shape[0], hd), BF16)
        for hh in range(nh):
            parts = [x[:, hh * hd:(hh + 1) * hd] if c == hh else ones for c in range(nh)]
            va_ref[hh, rows, :] = jnp.concatenate(parts, axis=1)

    def head_blocks(items):
        ss = [lax.dot_general(lq(), lk(), nt, preferred_element_type=F32) + lb()
              for lq, lk, _, _, lb, _ in items]
        ms = [jnp.max(s, axis=-1, keepdims=True) for s in ss]
        ps = [jnp.exp2((s - m) * c2).astype(BF16) for s, m in zip(ss, ms)]
        for p, m, (_, _, lva, hh, _, store) in zip(ps, ms, items):
            r = jnp.dot(p, lva(), preferred_element_type=F32)
            oh = (hh + 1) % nh
            l = r[:, oh * hd:(oh + 1) * hd]
            store(r[:, hh * hd:(hh + 1) * hd] / l, m * scale + jnp.log(l))

    def run_pattern(pi, d):
        ql = W // d
        kl = ql + blk
        nb = ql // blk
        unroll = ATTN_UNROLL
        assert nb % unroll == 0 or unroll % nb == 0

        def aligned(x):
            return x if isinstance(x, int) else pl.multiple_of(x, blk)

        def loop_body(it, carry):
            items = []
            for u in range(unroll):
                if nb >= unroll:
                    bodies_per_class = nb // unroll
                    r = 0 if d == 1 else it // bodies_per_class
                    jb = u if bodies_per_class == 1 else (it % bodies_per_class) * unroll + u
                else:
                    r = it * (unroll // nb) + u // nb
                    jb = u % nb
                qrow = aligned(r * ql + jb * blk)
                krow = aligned(r * kl + jb * blk)
                first_blk = jb == 0
                orow = aligned(jb * blk) if d == 1 else jb * (blk * d) + r
                for hh in range(nh):
                    cs = slice(hh * hd, (hh + 1) * hd)

                    def store(o, lse, hh=hh, orow=orow):
                        if d == 1:
                            rows = pl.ds(orow, blk)
                        else:
                            rows = pl.ds(orow, blk, stride=d)
                        op_ref[pi, hh, rows, :] = o
                        lp_ref[pi, hh, rows, :] = lse

                    items.append((
                        lambda qrow=qrow, cs=cs: (q_ref if d == 1 else qd_ref)[pl.ds(qrow, blk), cs],
                        lambda krow=krow, cs=cs: kd_ref[pl.ds(krow, 2 * blk), cs],
                        lambda krow=krow, hh=hh: va_ref[hh, pl.ds(krow, 2 * blk), :],
                        hh,
                        lambda first_blk=first_blk: bias_ref[
                            int(first_blk) if isinstance(first_blk, bool) else jnp.where(first_blk, 1, 0)],
                        store))
            head_blocks(items)
            return carry

        lax.fori_loop(0, (d * nb) // unroll, loop_body, 0)

    def deinterleave(d, p_ref):
        pc = grp // d
        ql = W // d
        kl = ql + blk
        ng = W // grp
        pm = p_ref[...]

        def split(x):
            return jnp.dot(pm, x, preferred_element_type=F32).astype(BF16)

        for g in range(ng):
            rows = slice(g * grp, (g + 1) * grp)
            yq, yk, yv = split(q_ref[rows, :]), split(kc_ref[rows, :]), split(vc_ref[rows, :])
            for r in range(d):
                piece = slice(r * pc, (r + 1) * pc)
                qd_ref[r * ql + g * pc:r * ql + (g + 1) * pc, :] = yq[piece]
                kd_ref[r * kl + blk + g * pc:r * kl + blk + (g + 1) * pc, :] = yk[piece]
                put_v(slice(r * kl + blk + g * pc, r * kl + blk + (g + 1) * pc), yv[piece])
        g0 = ng - (blk * d) // grp
        for g in range(g0, ng):
            rows = slice(g * grp, (g + 1) * grp)
            yk, yv = split(kp_ref[rows, :]), split(vp_ref[rows, :])
            for r in range(d):
                piece = slice(r * pc, (r + 1) * pc)
                dst = r * kl + (g - g0) * pc
                kd_ref[dst:dst + pc, :] = yk[piece]
                put_v(slice(dst, dst + pc), yv[piece])

    for pi, d in enumerate(dils):
        if d == 1:
            kd_ref[0:blk, :] = kp_ref[W - blk:W, :]
            kd_ref[blk:blk + W, :] = kc_ref[...]
            put_v(slice(0, blk), vp_ref[W - blk:W, :])
            put_v(slice(blk, blk + W), vc_ref[...])
        else:
            deinterleave(d, p4_ref if d == 4 else p16_ref)
        run_pattern(pi, d)

    fin = 2 * blk

    def fin_body(c, carry):
        rows = pl.ds(pl.multiple_of(c * fin, fin), fin)
        for hh in range(nh):
            ls = [lp_ref[pi, hh, rows, :] for pi in range(len(dils))]
            mx = functools.reduce(jnp.maximum, ls)
            ws = [jnp.exp(l - mx) for l in ls]
            num = sum(w * op_ref[pi, hh, rows, :] for pi, w in enumerate(ws))
            o_ref[rows, hh * hd:(hh + 1) * hd] = num / sum(ws)
        return carry

    lax.fori_loop(0, W // fin, fin_body, 0)


def _attention(pa3, heads, d_attn, col0):
    B, S, _ = pa3.shape
    hd = d_attn // heads
    dils = tuple(d for _, d in DILATION_PATTERNS)
    assert all(w == ATTN_BLOCK * d for w, d in DILATION_PATTERNS) and dils == (1, 4, 16)
    W = ATTN_WINDOW
    lanes = ATTN_HEADS_PER_STEP * hd
    cb = d_attn // lanes
    assert S % W == 0 and W % ATTN_GROUP == 0 and hd == LANES

    def cur(c):
        return pl.BlockSpec((None, W, lanes), lambda b, hp, n: (b, n, (col0 + c) * cb + hp))

    def prv(c):
        return pl.BlockSpec((None, W, lanes), lambda b, hp, n: (b, jnp.maximum(n - 1, 0), (col0 + c) * cb + hp))

    perm = pl.BlockSpec((ATTN_GROUP, ATTN_GROUP), lambda b, hp, n: (0, 0))
    kern = functools.partial(_attn_kernel, hd=hd, dils=dils)
    npat = len(dils)
    return pl.pallas_call(
        kern,
        grid=(B, cb, S // W),
        in_specs=[cur(0), cur(1), prv(1), cur(2), prv(2), perm, perm],
        out_specs=pl.BlockSpec((None, W, lanes), lambda b, hp, n: (b, n, hp)),
        out_shape=jax.ShapeDtypeStruct((B, S, d_attn), F32),
        scratch_shapes=[pltpu.VMEM((W, lanes), BF16),
                        pltpu.VMEM((2 * W, lanes), BF16),
                        pltpu.VMEM((ATTN_HEADS_PER_STEP, 2 * W, lanes), BF16),
                        pltpu.VMEM((npat, ATTN_HEADS_PER_STEP, W, hd), F32),
                        pltpu.VMEM((npat, ATTN_HEADS_PER_STEP, W, hd), F32),
                        pltpu.VMEM((2, ATTN_BLOCK, 2 * ATTN_BLOCK), F32)],
        compiler_params=_cp("arbitrary", "arbitrary", "arbitrary"),
        name="attn",
    )(pa3, pa3, pa3, pa3, pa3, _perm_matrix(4, ATTN_GROUP), _perm_matrix(16, ATTN_GROUP))


def _outproj_kernel(ya_ref, ob_ref, x_ref, w_ref, gattn_ref, gpost_ref, gate1_ref, gpre_ref,
                    sc2_ref, sh2_ref, wr_ref, br_ref, x1_ref, xl_ref, route_ref, cat_ref, h2_ref,
                    *, n_groups, epg, sub):
    dg = ya_ref.shape[1]
    half = x_ref.shape[1] // 2
    subs = [slice(k * sub, (k + 1) * sub) for k in range(ya_ref.shape[0] // sub)]
    for rs in subs:
        cat_ref[rs, :dg] = ya_ref[rs, :]
        cat_ref[rs, dg:] = (_rms(ob_ref[rs, :]) * gattn_ref[...]).astype(BF16)
    ys = [jnp.dot(cat_ref[rs, :], w_ref[...], preferred_element_type=F32) for rs in subs]
    for rs, y in zip(subs, ys):
        x1 = x_ref[rs, :] + gate1_ref[...] * (_rms(y) * gpost_ref[...])
        x1_ref[rs, :] = x1
        h2_ref[rs, :] = (_rms(x1) * gpre_ref[...]) * (1.0 + sc2_ref[...]) + sh2_ref[...]
    for k, rs in enumerate(subs):
        _route_rows(h2_ref[rs, :], wr_ref, br_ref, xl_ref, route_ref, rs, k * sub, half, n_groups, epg)


def _route_rows(h2, wr_ref, br_ref, xl_ref, route_ref, rs, row0, half, n_groups, epg):
    tm = h2.shape[0]
    hb = h2.astype(BF16)
    hb32 = hb.astype(F32)
    words = (lax.shift_right_logical(pltpu.bitcast(hb32[:, :half], jnp.uint32), jnp.uint32(16))
             | (pltpu.bitcast(hb32[:, half:], jnp.uint32) & jnp.uint32(0xFFFF0000)))
    _store_slabs(xl_ref, words, row0)

    lo = (h2 - hb32).astype(BF16)
    r = (jnp.dot(hb, wr_ref[...], preferred_element_type=F32)
         + jnp.dot(lo, wr_ref[...], preferred_element_type=F32))
    logits = r[:, :LANES] + r[:, LANES:] + br_ref[...]

    lane = lax.broadcasted_iota(jnp.int32, (tm, LANES), 1)
    lane_f = lane.astype(F32)
    big = float(LANES)
    lg = jnp.where(lane < n_groups, logits, NEG)
    mg = jnp.max(lg, axis=-1, keepdims=True)
    gi = jnp.min(jnp.where(lg == mg, lane_f, big), axis=-1, keepdims=True)
    gp = 1.0 / jnp.sum(jnp.exp(lg - mg), axis=-1, keepdims=True)
    e_lo = n_groups + gi * epg
    le = jnp.where(lane_f >= e_lo, jnp.where(lane_f < e_lo + epg, logits, NEG), NEG)
    m1 = jnp.max(le, axis=-1, keepdims=True)
    i1 = jnp.min(jnp.where(le == m1, lane_f, big), axis=-1, keepdims=True)
    le2 = jnp.where(lane_f == i1, NEG, le)
    m2 = jnp.max(le2, axis=-1, keepdims=True)
    i2 = jnp.min(jnp.where(le2 == m2, lane_f, big), axis=-1, keepdims=True)
    t = jnp.exp(m2 - m1)
    w1 = gp / (1.0 + t)
    w2 = gp * t / (1.0 + t)
    route_ref[rs, :] = jnp.where(lane == 0, i1 - n_groups,
                                 jnp.where(lane == 1, i2 - n_groups,
                                           jnp.where(lane == 2, w1, jnp.where(lane == 3, w2, 0.0))))


def _outproj(ya_n, ob, x2, w_out_bf, g_attn, g_post, mod3, g_pre, wr, br, seq, n_groups, epg):
    N, D = x2.shape
    dg = ya_n.shape[1]
    tm = TM_OUTPROJ
    tpb = seq // tm
    assert D == 2 * ROW_SLAB * LANES
    row = lambda i: (i, 0)
    const = lambda i: (0, 0)
    modc = lambda k: pl.BlockSpec((None, 1, D), lambda i: (i // tpb, 0, k))
    resident = dict(pipeline_mode=pl.Buffered(1))
    kern = functools.partial(_outproj_kernel, n_groups=n_groups, epg=epg, sub=SUB_OUTPROJ)
    return pl.pallas_call(
        kern,
        grid=(N // tm,),
        in_specs=[pl.BlockSpec((tm, dg), row), pl.BlockSpec((tm, dg), row), pl.BlockSpec((tm, D), row),
                  pl.BlockSpec((D, D), const, **resident), pl.BlockSpec((1, dg), const), pl.BlockSpec((1, D), const),
                  modc(2), pl.BlockSpec((1, D), const), modc(4), modc(3),
                  pl.BlockSpec((D, 2 * LANES), const, **resident), pl.BlockSpec((1, LANES), const)],
        out_specs=[pl.BlockSpec((tm, D), row), pl.BlockSpec((tm * ROW_SLAB, LANES), row),
                   pl.BlockSpec((tm, LANES), row)],
        out_shape=[jax.ShapeDtypeStruct((N, D), F32), jax.ShapeDtypeStruct((N * ROW_SLAB, LANES), jnp.uint32),
                   jax.ShapeDtypeStruct((N, LANES), F32)],
        scratch_shapes=[pltpu.VMEM((tm, D), BF16), pltpu.VMEM((tm, D), F32)],
        compiler_params=_cp("arbitrary"),
        name="outproj",
    )(ya_n, ob, x2, w_out_bf, g_attn, g_post, mod3, g_pre, mod3, mod3, wr, br)


def _rank_kernel(route_ref, rank_ref, cnt_ref, carry_ref):
    tr = route_ref.shape[0]

    @pl.when(pl.program_id(0) == 0)
    def _():
        carry_ref[...] = jnp.zeros_like(carry_ref)

    lane = lax.broadcasted_iota(jnp.int32, (tr, LANES), 1)
    lane_f = lane.astype(F32)
    oh0 = lane_f == route_ref[:, 0:1]
    oh1 = lane_f == route_ref[:, 1:2]
    oh = jnp.where(oh0, 1.0, jnp.where(oh1, 1.0, 0.0))
    ri = lax.broadcasted_iota(jnp.int32, (tr, tr), 0)
    ci = lax.broadcasted_iota(jnp.int32, (tr, tr), 1)
    earlier = jnp.where(ci < ri, 1.0, 0.0).astype(BF16)
    before = jnp.dot(earlier, oh.astype(BF16), preferred_element_type=F32) + carry_ref[...]
    r0 = jnp.sum(jnp.where(oh0, before, 0.0), axis=-1, keepdims=True)
    r1 = jnp.sum(jnp.where(oh1, before, 0.0), axis=-1, keepdims=True)
    rank_ref[...] = jnp.where(lane == 0, r0, jnp.where(lane == 1, r1, 0.0))
    carry_ref[...] = carry_ref[...] + jnp.sum(oh, axis=0, keepdims=True)
    cnt_ref[...] = carry_ref[...]


def _rank(route):
    N = route.shape[0]
    tr = TM_RANK
    return pl.pallas_call(
        _rank_kernel,
        grid=(N // tr,),
        in_specs=[pl.BlockSpec((tr, LANES), lambda i: (i, 0))],
        out_specs=[pl.BlockSpec((tr, LANES), lambda i: (i, 0)), pl.BlockSpec((1, LANES), lambda i: (0, 0))],
        out_shape=[jax.ShapeDtypeStruct((N, LANES), F32), jax.ShapeDtypeStruct((1, LANES), F32)],
        scratch_shapes=[pltpu.VMEM((1, LANES), F32)],
        compiler_params=_cp("arbitrary"),
        name="rank",
    )(route)


def _slab(i):
    return pl.ds(pl.multiple_of(i * ROW_SLAB, ROW_SLAB), ROW_SLAB)


def _dispatch_kernel(dest_ref, xl_ref, xs_hbm, sem):
    tm = xl_ref.shape[0] // ROW_SLAB

    def copy(t, d):
        return pltpu.make_async_copy(xl_ref.at[_slab(t)], xs_hbm.at[_slab(d)], sem)

    def start(t, c):
        for s in range(TOP_K):
            copy(t, dest_ref[0, TOP_K * t + s]).start()
        return c

    def wait(t, c):
        for s in range(TOP_K):
            copy(t, dest_ref[0, TOP_K * t + s]).wait()
        return c

    lax.fori_loop(0, tm, start, 0, unroll=8)
    lax.fori_loop(0, tm, wait, 0, unroll=8)


def _dispatch(xl, dest, n_rows):
    N = xl.shape[0] // ROW_SLAB
    tm = TM_DISPATCH
    return pl.pallas_call(
        _dispatch_kernel,
        grid=(N // tm,),
        in_specs=[pl.BlockSpec((None, 1, TOP_K * tm), lambda i: (i, 0, 0), memory_space=pltpu.SMEM),
                  pl.BlockSpec((tm * ROW_SLAB, LANES), lambda i: (i, 0))],
        out_specs=pl.BlockSpec(memory_space=pl.ANY),
        out_shape=jax.ShapeDtypeStruct((n_rows * ROW_SLAB, LANES), jnp.uint32),
        scratch_shapes=[pltpu.SemaphoreType.DMA(())],
        compiler_params=_cp("arbitrary"),
        name="dispatch",
    )(dest.reshape(N // tm, 1, TOP_K * tm), xl)


def _ffn_kernel(te_ref, tv_ref, ti_ref, to_ref, xs_ref, wg_ref, wu_ref, wd_ref, ys_ref, x_ref):
    del te_ref, ti_ref, to_ref
    nv = tv_ref[pl.program_id(0)]
    tm = x_ref.shape[0]
    half = x_ref.shape[1] // 2

    @pl.when(nv > 0)
    def _():
        keep = lax.broadcasted_iota(jnp.int32, (tm, 1), 0) < nv
        for j in range(ROW_SLAB):
            w = jnp.where(keep, xs_ref[pl.ds(j, tm, stride=ROW_SLAB), :], jnp.uint32(0))
            x_ref[:, j * LANES:(j + 1) * LANES] = _unpack_lo(w).astype(BF16)
            x_ref[:, half + j * LANES:half + (j + 1) * LANES] = _unpack_hi(w).astype(BF16)
        x = x_ref[...]
        g = jnp.dot(x, wg_ref[...], preferred_element_type=F32)
        u = jnp.dot(x, wu_ref[...], preferred_element_type=F32)
        hm = ((g * _sigmoid(g)) * u).astype(BF16)
        _store_slabs(ys_ref, _pack_rows(jnp.dot(hm, wd_ref[...], preferred_element_type=F32)))

    @pl.when(nv == 0)
    def _():
        ys_ref[...] = jnp.zeros_like(ys_ref)


def _ffn(xs, tile_e, tile_valid, tile_in, tile_out, wg, wu, wd):
    D, de = wg.shape[1], wg.shape[2]
    tm = TM_FFN
    nt = tile_e.shape[0]
    grid_spec = pltpu.PrefetchScalarGridSpec(
        num_scalar_prefetch=4,
        grid=(nt,),
        in_specs=[pl.BlockSpec((tm * ROW_SLAB, LANES), lambda t, te, tv, ti, to: (ti[t], 0)),
                  pl.BlockSpec((None, D, de), lambda t, te, tv, ti, to: (te[t], 0, 0)),
                  pl.BlockSpec((None, D, de), lambda t, te, tv, ti, to: (te[t], 0, 0)),
                  pl.BlockSpec((None, de, D), lambda t, te, tv, ti, to: (te[t], 0, 0))],
        out_specs=pl.BlockSpec((tm * ROW_SLAB, LANES), lambda t, te, tv, ti, to: (to[t], 0)),
        scratch_shapes=[pltpu.VMEM((tm, D), BF16)],
    )
    return pl.pallas_call(
        _ffn_kernel,
        grid_spec=grid_spec,
        out_shape=jax.ShapeDtypeStruct(xs.shape, jnp.uint32),
        compiler_params=_cp("arbitrary"),
        name="ffn",
    )(tile_e, tile_valid, tile_in, tile_out, xs, wg, wu, wd)


def _combine_kernel(dcur_ref, dnext_ref, route_ref, x1_ref, gate2_ref, g_ref, ys_hbm, o_ref, buf, y_ref, sems):
    i = pl.program_id(0)
    nsteps = pl.num_programs(0)
    tm = x1_ref.shape[0]
    half = x1_ref.shape[1] // 2

    def copy(slot, t, s, d):
        return pltpu.make_async_copy(ys_hbm.at[_slab(d)], buf.at[slot, s, _slab(t)], sems.at[slot])

    def start_tile(slot, dref):
        def body(t, c):
            for s in range(TOP_K):
                copy(slot, t, s, dref[0, TOP_K * t + s]).start()
            return c
        lax.fori_loop(0, tm, body, 0, unroll=8)

    def wait_tile(slot, dref):
        def body(t, c):
            for s in range(TOP_K):
                copy(slot, t, s, dref[0, TOP_K * t + s]).wait()
            return c
        lax.fori_loop(0, tm, body, 0, unroll=8)

    @pl.when(i == 0)
    def _():
        start_tile(0, dcur_ref)

    for slot in range(2):
        @pl.when(jnp.logical_and(i + 1 < nsteps, (i + 1) % 2 == slot))
        def _():
            start_tile(slot, dnext_ref)

    for slot in range(2):
        @pl.when(i % 2 == slot)
        def _():
            wait_tile(slot, dcur_ref)
            w0 = route_ref[:, 2:3]
            w1 = route_ref[:, 3:4]
            for j in range(ROW_SLAB):
                a = buf[slot, 0, pl.ds(j, tm, stride=ROW_SLAB), :]
                b = buf[slot, 1, pl.ds(j, tm, stride=ROW_SLAB), :]
                y_ref[:, j * LANES:(j + 1) * LANES] = _unpack_lo(a) * w0 + _unpack_lo(b) * w1
                y_ref[:, half + j * LANES:half + (j + 1) * LANES] = _unpack_hi(a) * w0 + _unpack_hi(b) * w1

    o_ref[...] = x1_ref[...] + gate2_ref[...] * (_rms(y_ref[...]) * g_ref[...])


def _combine(ys, dest, route, x1, mod3, g_post, seq):
    N, D = x1.shape
    tm = TM_COMBINE
    tpb = seq // tm
    nsteps = N // tm
    dest3 = dest.reshape(nsteps, 1, TOP_K * tm)
    return pl.pallas_call(
        _combine_kernel,
        grid=(nsteps,),
        in_specs=[pl.BlockSpec((None, 1, TOP_K * tm), lambda i: (i, 0, 0), memory_space=pltpu.SMEM),
                  pl.BlockSpec((None, 1, TOP_K * tm), lambda i: (jnp.minimum(i + 1, nsteps - 1), 0, 0),
                               memory_space=pltpu.SMEM),
                  pl.BlockSpec((tm, LANES), lambda i: (i, 0)),
                  pl.BlockSpec((tm, D), lambda i: (i, 0)),
                  pl.BlockSpec((None, 1, D), lambda i: (i // tpb, 0, 5)),
                  pl.BlockSpec((1, D), lambda i: (0, 0)),
                  pl.BlockSpec(memory_space=pl.ANY)],
        out_specs=pl.BlockSpec((tm, D), lambda i: (i, 0)),
        out_shape=jax.ShapeDtypeStruct((N, D), F32),
        scratch_shapes=[pltpu.VMEM((2, TOP_K, tm * ROW_SLAB, LANES), jnp.uint32), pltpu.VMEM((tm, D), F32),
                        pltpu.SemaphoreType.DMA((2,))],
        compiler_params=_cp("arbitrary"),
        name="combine",
    )(dest3, dest3, route, x1, mod3, g_post, ys)


def _routing_tables(route, rank, cnt, n_experts, tm):
    N = route.shape[0]
    A = N * TOP_K
    counts = cnt[0, :n_experts].astype(jnp.int32)
    padded = ((counts + tm - 1) // tm) * tm
    pend = jnp.cumsum(padded)
    pstart = pend - padded
    dest = jnp.take(pstart, route[:, :TOP_K].astype(jnp.int32)) + rank[:, :TOP_K].astype(jnp.int32)
    nt = A // tm + n_experts
    n_used = pend[-1] // tm
    tidx = jnp.arange(nt, dtype=jnp.int32)
    tstart = tidx * tm
    te = jnp.minimum(jnp.searchsorted(pend, tstart, side='right'), n_experts - 1).astype(jnp.int32)
    tv = jnp.clip(counts[te] - (tstart - pstart[te]), 0, tm).astype(jnp.int32)
    used = tidx < n_used
    last = jnp.maximum(n_used - 1, 0)
    te = jnp.where(used, te, te[last]).astype(jnp.int32)
    ti = jnp.where(used, tidx, last).astype(jnp.int32)
    to = tidx
    return dest.astype(jnp.int32), te, tv, ti, to, nt


def kernel(x, c, w_mod, b_mod, g_pre_mix, g_post_mix, w_in, g_gmlp_v, w_spatial, b_spatial, g_out_gmlp,
           g_out_attn, w_out, g_pre_ffn, g_post_ffn, w_router_group, b_router_group, w_router_expert,
           b_router_expert, w_gate, w_up, w_down):
    B, S, D = x.shape
    N = B * S
    depth = w_mod.shape[0]
    d_gmlp = g_gmlp_v.shape[1]
    d_attn = g_out_attn.shape[1]
    heads = w_spatial.shape[1]
    n_groups, epg = b_router_expert.shape[1], b_router_expert.shape[2]
    n_experts = n_groups * epg
    assert d_gmlp == d_attn and w_in.shape[2] == 2 * d_gmlp + 3 * d_attn
    assert n_groups + n_experts <= LANES

    x2 = x.reshape(N, D)
    for l in range(depth):
        mod3 = _modulation(c, w_mod[l], b_mod[l]).reshape(B, 1, w_mod.shape[2])

        pa = _inproj(x2, mod3, g_pre_mix[l][None], w_in[l].astype(BF16), g_gmlp_v[l][None], S)
        ya_n = _gmlp(pa, w_spatial[l], b_spatial[l], g_out_gmlp[l][None])
        ob = _attention(pa.reshape(B, S, pa.shape[1]), heads, d_attn, (2 * d_gmlp) // d_attn).reshape(N, d_attn)

        wr32 = jnp.concatenate([w_router_group[l],
                                jnp.transpose(w_router_expert[l], (1, 0, 2)).reshape(D, n_experts)], axis=1)
        wr32 = jnp.pad(wr32, ((0, 0), (0, LANES - wr32.shape[1])))
        wr_hi = wr32.astype(BF16)
        wr = jnp.concatenate([wr_hi, (wr32 - wr_hi.astype(F32)).astype(BF16)], axis=1)
        br = jnp.pad(jnp.concatenate([b_router_group[l], b_router_expert[l].reshape(n_experts)]),
                     (0, LANES - n_groups - n_experts))[None]

        x1, xl, route = _outproj(ya_n, ob, x2, w_out[l].astype(BF16), g_out_attn[l][None], g_post_mix[l][None],
                                 mod3, g_pre_ffn[l][None], wr, br, S, n_groups, epg)

        rank, cnt = _rank(route)
        dest, te, tv, ti, to, nt = _routing_tables(route, rank, cnt, n_experts, TM_FFN)
        xs = _dispatch(xl, dest, nt * TM_FFN)
        ys = _ffn(xs, te, tv, ti, to, w_gate[l].astype(BF16), w_up[l].astype(BF16), w_down[l].astype(BF16))
        x2 = _combine(ys, dest, route, x1, mod3, g_post_ffn[l][None], S)
    return x2.reshape(B, S, D)
```

```python
import functools
import math

import jax
import jax.numpy as jnp
from jax import lax
from jax.experimental import pallas as pl
from jax.experimental.pallas import tpu as pltpu

F32 = jnp.float32
BF16 = jnp.bfloat16
EPS = 1e-6
NEG = -1e30

DILATION_PATTERNS = ((128, 1), (512, 4), (2048, 16))
ATTN_BLOCK = 128
TOP_K = 2
LANES = 128

TM_INPROJ = 512
SUB_INPROJ = 256
TM_GMLP = 512
ATTN_WINDOW = ATTN_BLOCK * max(d for _, d in DILATION_PATTERNS)
ATTN_GROUP = 256
ATTN_HEADS_PER_STEP = 2
ATTN_UNROLL = 4
TM_OUTPROJ = 512
SUB_OUTPROJ = 256
TM_RANK = 512
TM_DISPATCH = 512
TM_FFN = 256
TM_COMBINE = 256
TN_MOD = 1024
ROW_SLAB = 8
VMEM_LIMIT = 56 * 1024 * 1024


def _cp(*dims):
    return pltpu.CompilerParams(dimension_semantics=dims, vmem_limit_bytes=VMEM_LIMIT)


def _rms(x):
    return x * lax.rsqrt(jnp.mean(x * x, axis=-1, keepdims=True) + EPS)


def _gelu_tanh(x):
    c = math.sqrt(2.0 / math.pi)
    return x * (0.5 * (1.0 + jnp.tanh(c * (x + 0.044715 * (x * x * x)))))


def _sigmoid(x):
    return 1.0 / (1.0 + jnp.exp(-x))


def _pack_rows(x):
    half = x.shape[1] // 2
    lo = pltpu.bitcast(x[:, :half].astype(BF16).astype(F32), jnp.uint32)
    hi = pltpu.bitcast(x[:, half:].astype(BF16).astype(F32), jnp.uint32)
    return lax.shift_right_logical(lo, jnp.uint32(16)) | (hi & jnp.uint32(0xFFFF0000))


def _unpack_lo(w):
    return pltpu.bitcast(lax.shift_left(w, jnp.uint32(16)), F32)


def _unpack_hi(w):
    return pltpu.bitcast(w & jnp.uint32(0xFFFF0000), F32)


def _store_slabs(ref, words, row0=0):
    rows = words.shape[0]
    for j in range(ROW_SLAB):
        ref[pl.ds(row0 * ROW_SLAB + j, rows, stride=ROW_SLAB), :] = words[:, j * LANES:(j + 1) * LANES]


def _mod_kernel(ct_ref, w_ref, b_ref, o_ref):
    ct = ct_ref[...]
    at = ct * _sigmoid(ct)
    w = w_ref[...]
    for b in range(o_ref.shape[0]):
        o_ref[b:b + 1, :] = jnp.sum(w * at[:, b:b + 1], axis=0, keepdims=True) + b_ref[...]


def _modulation(c, w_mod, b_mod):
    B, D = c.shape
    n_out = w_mod.shape[1]
    return pl.pallas_call(
        _mod_kernel,
        grid=(n_out // TN_MOD,),
        in_specs=[pl.BlockSpec((D, B), lambda j: (0, 0)),
                  pl.BlockSpec((D, TN_MOD), lambda j: (0, j)),
                  pl.BlockSpec((1, TN_MOD), lambda j: (0, j))],
        out_specs=pl.BlockSpec((B, TN_MOD), lambda j: (0, j)),
        out_shape=jax.ShapeDtypeStruct((B, n_out), F32),
        compiler_params=_cp("arbitrary"),
        name="mod",
    )(c.T, w_mod, b_mod.reshape(1, n_out))


def _inproj_kernel(x_ref, g_ref, sc_ref, sh_ref, w_ref, gv_ref, o_ref, h_ref, *, sub):
    s = pl.program_id(0)
    last = pl.num_programs(0) - 1
    tm = x_ref.shape[0]
    tn = gv_ref.shape[1]

    def normalise():
        h = (_rms(x_ref[...]) * g_ref[...]) * (1.0 + sc_ref[...]) + sh_ref[...]
        h_ref[pl.ds(pl.multiple_of((s % 2) * tm, tm), tm), :] = h.astype(BF16)

    def project():
        base = pl.multiple_of(((s + 1) % 2) * tm, tm)
        for j in range(w_ref.shape[1] // tn):
            cs = slice(j * tn, (j + 1) * tn)
            for k in range(tm // sub):
                acc = jnp.dot(h_ref[pl.ds(base + k * sub, sub), :], w_ref[:, cs], preferred_element_type=F32)
                if j == 0:
                    acc = _gelu_tanh(acc)
                elif j == 1:
                    v = _gelu_tanh(acc)
                    vc = v - jnp.mean(v, axis=-1, keepdims=True)
                    acc = vc * lax.rsqrt(jnp.mean(vc * vc, axis=-1, keepdims=True) + EPS) * gv_ref[...]
                o_ref[k * sub:(k + 1) * sub, cs] = acc.astype(BF16)

    @pl.when(s == 0)
    def _():
        normalise()

    @pl.when(jnp.logical_and(s > 0, s < last))
    def _():
        normalise()
        project()

    @pl.when(s == last)
    def _():
        project()


def _inproj(x2, mod3, g_pre, w_in_bf, g_v, seq):
    N, D = x2.shape
    d_in = w_in_bf.shape[1]
    tn = g_v.shape[1]
    tm = TM_INPROJ
    tpb = seq // tm
    nt = N // tm
    cur = lambda s: jnp.minimum(s, nt - 1)
    return pl.pallas_call(
        functools.partial(_inproj_kernel, sub=SUB_INPROJ),
        grid=(nt + 1,),
        in_specs=[pl.BlockSpec((tm, D), lambda s: (cur(s), 0)),
                  pl.BlockSpec((1, D), lambda s: (0, 0)),
                  pl.BlockSpec((None, 1, D), lambda s: (cur(s) // tpb, 0, 1)),
                  pl.BlockSpec((None, 1, D), lambda s: (cur(s) // tpb, 0, 0)),
                  pl.BlockSpec((D, d_in), lambda s: (0, 0), pipeline_mode=pl.Buffered(1)),
                  pl.BlockSpec((1, tn), lambda s: (0, 0))],
        out_specs=pl.BlockSpec((tm, d_in), lambda s: (jnp.maximum(s - 1, 0), 0)),
        out_shape=jax.ShapeDtypeStruct((N, d_in), BF16),
        scratch_shapes=[pltpu.VMEM((2 * tm, D), BF16)],
        compiler_params=_cp("arbitrary"),
        name="inproj",
    )(x2, g_pre, mod3, mod3, w_in_bf, g_v)


def _gmlp_kernel(u_ref, v_ref, w_ref, bt_ref, g_ref, o_ref, ya_ref, *, heads, chunk, hd):
    tm = u_ref.shape[0]
    row = lax.broadcasted_iota(jnp.int32, (chunk, chunk), 0)
    col = lax.broadcasted_iota(jnp.int32, (chunk, chunk), 1)
    causal = col <= row
    for h in range(heads):
        wm = jnp.where(causal, w_ref[h], 0.0).astype(BF16)
        bcol = bt_ref[:, h:h + 1]
        cs = slice(h * hd, (h + 1) * hd)
        for c in range(tm // chunk):
            rs = slice(c * chunk, (c + 1) * chunk)
            sv = jnp.dot(wm, v_ref[rs, cs], preferred_element_type=F32) + bcol
            ya_ref[rs, cs] = u_ref[rs, cs].astype(F32) * sv
    o_ref[...] = (_rms(ya_ref[...]) * g_ref[...]).astype(BF16)


def _gmlp(pa, w_spatial, b_spatial, g_out):
    N = pa.shape[0]
    heads, chunk, _ = w_spatial.shape
    dg = g_out.shape[1]
    kern = functools.partial(_gmlp_kernel, heads=heads, chunk=chunk, hd=dg // heads)
    return pl.pallas_call(
        kern,
        grid=(N // TM_GMLP,),
        in_specs=[pl.BlockSpec((TM_GMLP, dg), lambda i: (i, 0)),
                  pl.BlockSpec((TM_GMLP, dg), lambda i: (i, 1)),
                  pl.BlockSpec((heads, chunk, chunk), lambda i: (0, 0, 0)),
                  pl.BlockSpec((chunk, heads), lambda i: (0, 0)),
                  pl.BlockSpec((1, dg), lambda i: (0, 0))],
        out_specs=pl.BlockSpec((TM_GMLP, dg), lambda i: (i, 0)),
        out_shape=jax.ShapeDtypeStruct((N, dg), BF16),
        scratch_shapes=[pltpu.VMEM((TM_GMLP, dg), F32)],
        compiler_params=_cp("arbitrary"),
        name="gmlp",
    )(pa, pa, w_spatial, b_spatial.T, g_out)


def _perm_matrix(d, n):
    i = jnp.arange(n)
    src = (i % (n // d)) * d + i // (n // d)
    return (src[:, None] == jnp.arange(n)[None, :]).astype(BF16)


def _attn_kernel(q_ref, kc_ref, kp_ref, vc_ref, vp_ref, p4_ref, p16_ref, o_ref,
                 qd_ref, kd_ref, va_ref, op_ref, lp_ref, bias_ref, *, hd, dils):
    n = pl.program_id(2)
    blk = ATTN_BLOCK
    grp = ATTN_GROUP
    W = q_ref.shape[0]
    nh = q_ref.shape[1] // hd
    scale = hd ** -0.5
    c2 = scale * math.log2(math.e)
    nt = (((1,), (1,)), ((), ()))

    ri = lax.broadcasted_iota(jnp.int32, (blk, 2 * blk), 0)
    ci = lax.broadcasted_iota(jnp.int32, (blk, 2 * blk), 1)
    band = jnp.where(ci >= ri, jnp.where(ci <= ri + blk, 0.0, NEG), NEG)
    bias_ref[0] = band
    bias_ref[1] = jnp.where(n > 0, band, jnp.where(ci < blk, NEG, band))

    def put_v(rows, x):
        ones = jnp.ones((x.shape[0], hd), BF16)
        for hh in range(nh):
            parts = [x[:, hh * hd:(hh + 1) * hd] if c == hh else ones for c in range(nh)]
            va_ref[hh, rows, :] = jnp.concatenate(parts, axis=1)

    def head_blocks(items):
        ss = [lax.dot_general(lq(), lk(), nt, preferred_element_type=F32) + lb()
              for lq, lk, _, _, lb, _ in items]
        ms = [jnp.max(s, axis=-1, keepdims=True) for s in ss]
        ps = [jnp.exp2((s - m) * c2).astype(BF16) for s, m in zip(ss, ms)]
        for p, m, (_, _, lva, hh, _, store) in zip(ps, ms, items):
            r = jnp.dot(p, lva(), preferred_element_type=F32)
            oh = (hh + 1) % nh
            l = r[:, oh * hd:(oh + 1) * hd]
            store(r[:, hh * hd:(hh + 1) * hd] / l, m * scale + jnp.log(l))

    def run_pattern(pi, d):
        ql = W // d
        kl = ql + blk
        nb = ql // blk
        unroll = ATTN_UNROLL
        assert nb % unroll == 0 or unroll % nb == 0

        def aligned(x):
            return x if isinstance(x, int) else pl.multiple_of(x, blk)

        def loop_body(it, carry):
            items = []
            for u in range(unroll):
                if nb >= unroll:
                    bodies_per_class = nb // unroll
                    r = 0 if d == 1 else it // bodies_per_class
                    jb = u if bodies_per_class == 1 else (it % bodies_per_class) * unroll + u
                else:
                    r = it * (unroll // nb) + u // nb
                    jb = u % nb
                qrow = aligned(r * ql + jb * blk)
                krow = aligned(r * kl + jb * blk)
                first_blk = jb == 0
                orow = aligned(jb * blk) if d == 1 else jb * (blk * d) + r
                for hh in range(nh):
                    cs = slice(hh * hd, (hh + 1) * hd)

                    def store(o, lse, hh=hh, orow=orow):
                        if d == 1:
                            rows = pl.ds(orow, blk)
                        else:
                            rows = pl.ds(orow, blk, stride=d)
                        op_ref[pi, hh, rows, :] = o
                        lp_ref[pi, hh, rows, :] = lse

                    items.append((
                        lambda qrow=qrow, cs=cs: (q_ref if d == 1 else qd_ref)[pl.ds(qrow, blk), cs],
                        lambda krow=krow, cs=cs: kd_ref[pl.ds(krow, 2 * blk), cs],
                        lambda krow=krow, hh=hh: va_ref[hh, pl.ds(krow, 2 * blk), :],
                        hh,
                        lambda first_blk=first_blk: bias_ref[
                            int(first_blk) if isinstance(first_blk, bool) else jnp.where(first_blk, 1, 0)],
                        store))
            head_blocks(items)
            return carry

        lax.fori_loop(0, (d * nb) // unroll, loop_body, 0)

    def deinterleave(d, p_ref):
        pc = grp // d
        ql = W // d
        kl = ql + blk
        ng = W // grp
        pm = p_ref[...]

        def split(x):
            return jnp.dot(pm, x, preferred_element_type=F32).astype(BF16)

        for g in range(ng):
            rows = slice(g * grp, (g + 1) * grp)
            yq, yk, yv = split(q_ref[rows, :]), split(kc_ref[rows, :]), split(vc_ref[rows, :])
            for r in range(d):
                piece = slice(r * pc, (r + 1) * pc)
                qd_ref[r * ql + g * pc:r * ql + (g + 1) * pc, :] = yq[piece]
                kd_ref[r * kl + blk + g * pc:r * kl + blk + (g + 1) * pc, :] = yk[piece]
                put_v(slice(r * kl + blk + g * pc, r * kl + blk + (g + 1) * pc), yv[piece])
        g0 = ng - (blk * d) // grp
        for g in range(g0, ng):
            rows = slice(g * grp, (g + 1) * grp)
            yk, yv = split(kp_ref[rows, :]), split(vp_ref[rows, :])
            for r in range(d):
                piece = slice(r * pc, (r + 1) * pc)
                dst = r * kl + (g - g0) * pc
                kd_ref[dst:dst + pc, :] = yk[piece]
                put_v(slice(dst, dst + pc), yv[piece])

    for pi, d in enumerate(dils):
        if d == 1:
            kd_ref[0:blk, :] = kp_ref[W - blk:W, :]
            kd_ref[blk:blk + W, :] = kc_ref[...]
            put_v(slice(0, blk), vp_ref[W - blk:W, :])
            put_v(slice(blk, blk + W), vc_ref[...])
        else:
            deinterleave(d, p4_ref if d == 4 else p16_ref)
        run_pattern(pi, d)

    fin = 2 * blk

    def fin_body(c, carry):
        rows = pl.ds(pl.multiple_of(c * fin, fin), fin)
        for hh in range(nh):
            ls = [lp_ref[pi, hh, rows, :] for pi in range(len(dils))]
            mx = functools.reduce(jnp.maximum, ls)
            ws = [jnp.exp(l - mx) for l in ls]
            num = sum(w * op_ref[pi, hh, rows, :] for pi, w in enumerate(ws))
            o_ref[rows, hh * hd:(hh + 1) * hd] = num / sum(ws)
        return carry

    lax.fori_loop(0, W // fin, fin_body, 0)


def _attention(pa3, heads, d_attn, col0):
    B, S, _ = pa3.shape
    hd = d_attn // heads
    dils = tuple(d for _, d in DILATION_PATTERNS)
    assert all(w == ATTN_BLOCK * d for w, d in DILATION_PATTERNS) and dils == (1, 4, 16)
    W = ATTN_WINDOW
    lanes = ATTN_HEADS_PER_STEP * hd
    cb = d_attn // lanes
    assert S % W == 0 and W % ATTN_GROUP == 0 and hd == LANES

    def cur(c):
        return pl.BlockSpec((None, W, lanes), lambda b, hp, n: (b, n, (col0 + c) * cb + hp))

    def prv(c):
        return pl.BlockSpec((None, W, lanes), lambda b, hp, n: (b, jnp.maximum(n - 1, 0), (col0 + c) * cb + hp))

    perm = pl.BlockSpec((ATTN_GROUP, ATTN_GROUP), lambda b, hp, n: (0, 0))
    kern = functools.partial(_attn_kernel, hd=hd, dils=dils)
    npat = len(dils)
    return pl.pallas_call(
        kern,
        grid=(B, cb, S // W),
        in_specs=[cur(0), cur(1), prv(1), cur(2), prv(2), perm, perm],
        out_specs=pl.BlockSpec((None, W, lanes), lambda b, hp, n: (b, n, hp)),
        out_shape=jax.ShapeDtypeStruct((B, S, d_attn), F32),
        scratch_shapes=[pltpu.VMEM((W, lanes), BF16),
                        pltpu.VMEM((2 * W, lanes), BF16),
                        pltpu.VMEM((ATTN_HEADS_PER_STEP, 2 * W, lanes), BF16),
                        pltpu.VMEM((npat, ATTN_HEADS_PER_STEP, W, hd), F32),
                        pltpu.VMEM((npat, ATTN_HEADS_PER_STEP, W, hd), F32),
                        pltpu.VMEM((2, ATTN_BLOCK, 2 * ATTN_BLOCK), F32)],
        compiler_params=_cp("arbitrary", "arbitrary", "arbitrary"),
        name="attn",
    )(pa3, pa3, pa3, pa3, pa3, _perm_matrix(4, ATTN_GROUP), _perm_matrix(16, ATTN_GROUP))


def _outproj_kernel(ya_ref, ob_ref, x_ref, w_ref, gattn_ref, gpost_ref, gate1_ref, gpre_ref,
                    sc2_ref, sh2_ref, wr_ref, br_ref, x1_ref, xl_ref, route_ref, cat_ref, h2_ref,
                    *, n_groups, epg, sub):
    dg = ya_ref.shape[1]
    half = x_ref.shape[1] // 2
    subs = [slice(k * sub, (k + 1) * sub) for k in range(ya_ref.shape[0] // sub)]
    for rs in subs:
        cat_ref[rs, :dg] = ya_ref[rs, :]
        cat_ref[rs, dg:] = (_rms(ob_ref[rs, :]) * gattn_ref[...]).astype(BF16)
    ys = [jnp.dot(cat_ref[rs, :], w_ref[...], preferred_element_type=F32) for rs in subs]
    for rs, y in zip(subs, ys):
        x1 = x_ref[rs, :] + gate1_ref[...] * (_rms(y) * gpost_ref[...])
        x1_ref[rs, :] = x1
        h2_ref[rs, :] = (_rms(x1) * gpre_ref[...]) * (1.0 + sc2_ref[...]) + sh2_ref[...]
    for k, rs in enumerate(subs):
        _route_rows(h2_ref[rs, :], wr_ref, br_ref, xl_ref, route_ref, rs, k * sub, half, n_groups, epg)


def _route_rows(h2, wr_ref, br_ref, xl_ref, route_ref, rs, row0, half, n_groups, epg):
    tm = h2.shape[0]
    hb = h2.astype(BF16)
    hb32 = hb.astype(F32)
    words = (lax.shift_right_logical(pltpu.bitcast(hb32[:, :half], jnp.uint32), jnp.uint32(16))
             | (pltpu.bitcast(hb32[:, half:], jnp.uint32) & jnp.uint32(0xFFFF0000)))
    _store_slabs(xl_ref, words, row0)

    lo = (h2 - hb32).astype(BF16)
    r = (jnp.dot(hb, wr_ref[...], preferred_element_type=F32)
         + jnp.dot(lo, wr_ref[...], preferred_element_type=F32))
    logits = r[:, :LANES] + r[:, LANES:] + br_ref[...]

    lane = lax.broadcasted_iota(jnp.int32, (tm, LANES), 1)
    lane_f = lane.astype(F32)
    big = float(LANES)
    lg = jnp.where(lane < n_groups, logits, NEG)
    mg = jnp.max(lg, axis=-1, keepdims=True)
    gi = jnp.min(jnp.where(lg == mg, lane_f, big), axis=-1, keepdims=True)
    gp = 1.0 / jnp.sum(jnp.exp(lg - mg), axis=-1, keepdims=True)
    e_lo = n_groups + gi * epg
    le = jnp.where(lane_f >= e_lo, jnp.where(lane_f < e_lo + epg, logits, NEG), NEG)
    m1 = jnp.max(le, axis=-1, keepdims=True)
    i1 = jnp.min(jnp.where(le == m1, lane_f, big), axis=-1, keepdims=True)
    le2 = jnp.where(lane_f == i1, NEG, le)
    m2 = jnp.max(le2, axis=-1, keepdims=True)
    i2 = jnp.min(jnp.where(le2 == m2, lane_f, big), axis=-1, keepdims=True)
    t = jnp.exp(m2 - m1)
    w1 = gp / (1.0 + t)
    w2 = gp * t / (1.0 + t)
    route_ref[rs, :] = jnp.where(lane == 0, i1 - n_groups,
                                 jnp.where(lane == 1, i2 - n_groups,
                                           jnp.where(lane == 2, w1, jnp.where(lane == 3, w2, 0.0))))


def _outproj(ya_n, ob, x2, w_out_bf, g_attn, g_post, mod3, g_pre, wr, br, seq, n_groups, epg):
    N, D = x2.shape
    dg = ya_n.shape[1]
    tm = TM_OUTPROJ
    tpb = seq // tm
    assert D == 2 * ROW_SLAB * LANES
    row = lambda i: (i, 0)
    const = lambda i: (0, 0)
    modc = lambda k: pl.BlockSpec((None, 1, D), lambda i: (i // tpb, 0, k))
    resident = dict(pipeline_mode=pl.Buffered(1))
    kern = functools.partial(_outproj_kernel, n_groups=n_groups, epg=epg, sub=SUB_OUTPROJ)
    return pl.pallas_call(
        kern,
        grid=(N // tm,),
        in_specs=[pl.BlockSpec((tm, dg), row), pl.BlockSpec((tm, dg), row), pl.BlockSpec((tm, D), row),
                  pl.BlockSpec((D, D), const, **resident), pl.BlockSpec((1, dg), const), pl.BlockSpec((1, D), const),
                  modc(2), pl.BlockSpec((1, D), const), modc(4), modc(3),
                  pl.BlockSpec((D, 2 * LANES), const, **resident), pl.BlockSpec((1, LANES), const)],
        out_specs=[pl.BlockSpec((tm, D), row), pl.BlockSpec((tm * ROW_SLAB, LANES), row),
                   pl.BlockSpec((tm, LANES), row)],
        out_shape=[jax.ShapeDtypeStruct((N, D), F32), jax.ShapeDtypeStruct((N * ROW_SLAB, LANES), jnp.uint32),
                   jax.ShapeDtypeStruct((N, LANES), F32)],
        scratch_shapes=[pltpu.VMEM((tm, D), BF16), pltpu.VMEM((tm, D), F32)],
        compiler_params=_cp("arbitrary"),
        name="outproj",
    )(ya_n, ob, x2, w_out_bf, g_attn, g_post, mod3, g_pre, mod3, mod3, wr, br)


def _rank_kernel(route_ref, rank_ref, cnt_ref, carry_ref):
    tr = route_ref.shape[0]

    @pl.when(pl.program_id(0) == 0)
    def _():
        carry_ref[...] = jnp.zeros_like(carry_ref)

    lane = lax.broadcasted_iota(jnp.int32, (tr, LANES), 1)
    lane_f = lane.astype(F32)
    oh0 = lane_f == route_ref[:, 0:1]
    oh1 = lane_f == route_ref[:, 1:2]
    oh = jnp.where(oh0, 1.0, jnp.where(oh1, 1.0, 0.0))
    ri = lax.broadcasted_iota(jnp.int32, (tr, tr), 0)
    ci = lax.broadcasted_iota(jnp.int32, (tr, tr), 1)
    earlier = jnp.where(ci < ri, 1.0, 0.0).astype(BF16)
    before = jnp.dot(earlier, oh.astype(BF16), preferred_element_type=F32) + carry_ref[...]
    r0 = jnp.sum(jnp.where(oh0, before, 0.0), axis=-1, keepdims=True)
    r1 = jnp.sum(jnp.where(oh1, before, 0.0), axis=-1, keepdims=True)
    rank_ref[...] = jnp.where(lane == 0, r0, jnp.where(lane == 1, r1, 0.0))
    carry_ref[...] = carry_ref[...] + jnp.sum(oh, axis=0, keepdims=True)
    cnt_ref[...] = carry_ref[...]


def _rank(route):
    N = route.shape[0]
    tr = TM_RANK
    return pl.pallas_call(
        _rank_kernel,
        grid=(N // tr,),
        in_specs=[pl.BlockSpec((tr, LANES), lambda i: (i, 0))],
        out_specs=[pl.BlockSpec((tr, LANES), lambda i: (i, 0)), pl.BlockSpec((1, LANES), lambda i: (0, 0))],
        out_shape=[jax.ShapeDtypeStruct((N, LANES), F32), jax.ShapeDtypeStruct((1, LANES), F32)],
        scratch_shapes=[pltpu.VMEM((1, LANES), F32)],
        compiler_params=_cp("arbitrary"),
        name="rank",
    )(route)


def _slab(i):
    return pl.ds(pl.multiple_of(i * ROW_SLAB, ROW_SLAB), ROW_SLAB)


def _dispatch_kernel(dest_ref, xl_ref, xs_hbm, sem):
    tm = xl_ref.shape[0] // ROW_SLAB

    def copy(t, d):
        return pltpu.make_async_copy(xl_ref.at[_slab(t)], xs_hbm.at[_slab(d)], sem)

    def start(t, c):
        for s in range(TOP_K):
            copy(t, dest_ref[0, TOP_K * t + s]).start(priority=s % 2)
        return c

    def wait(t, c):
        for s in range(TOP_K):
            copy(t, dest_ref[0, TOP_K * t + s]).wait()
        return c

    lax.fori_loop(0, tm, start, 0, unroll=8)
    lax.fori_loop(0, tm, wait, 0, unroll=8)


def _dispatch(xl, dest, n_rows):
    N = xl.shape[0] // ROW_SLAB
    tm = TM_DISPATCH
    return pl.pallas_call(
        _dispatch_kernel,
        grid=(N // tm,),
        in_specs=[pl.BlockSpec((None, 1, TOP_K * tm), lambda i: (i, 0, 0), memory_space=pltpu.SMEM),
                  pl.BlockSpec((tm * ROW_SLAB, LANES), lambda i: (i, 0))],
        out_specs=pl.BlockSpec(memory_space=pl.ANY),
        out_shape=jax.ShapeDtypeStruct((n_rows * ROW_SLAB, LANES), jnp.uint32),
        scratch_shapes=[pltpu.SemaphoreType.DMA(())],
        compiler_params=_cp("arbitrary"),
        name="dispatch",
    )(dest.reshape(N // tm, 1, TOP_K * tm), xl)


def _ffn_kernel(te_ref, tv_ref, ti_ref, to_ref, xs_ref, wg_ref, wu_ref, wd_ref, ys_ref, x_ref):
    del te_ref, ti_ref, to_ref
    nv = tv_ref[pl.program_id(0)]
    tm = x_ref.shape[0]
    half = x_ref.shape[1] // 2

    @pl.when(nv > 0)
    def _():
        keep = lax.broadcasted_iota(jnp.int32, (tm, 1), 0) < nv
        for j in range(ROW_SLAB):
            w = jnp.where(keep, xs_ref[pl.ds(j, tm, stride=ROW_SLAB), :], jnp.uint32(0))
            x_ref[:, j * LANES:(j + 1) * LANES] = _unpack_lo(w).astype(BF16)
            x_ref[:, half + j * LANES:half + (j + 1) * LANES] = _unpack_hi(w).astype(BF16)
        x = x_ref[...]
        g = jnp.dot(x, wg_ref[...], preferred_element_type=F32)
        u = jnp.dot(x, wu_ref[...], preferred_element_type=F32)
        hm = ((g * _sigmoid(g)) * u).astype(BF16)
        _store_slabs(ys_ref, _pack_rows(jnp.dot(hm, wd_ref[...], preferred_element_type=F32)))

    @pl.when(nv == 0)
    def _():
        ys_ref[...] = jnp.zeros_like(ys_ref)


def _ffn(xs, tile_e, tile_valid, tile_in, tile_out, wg, wu, wd):
    D, de = wg.shape[1], wg.shape[2]
    tm = TM_FFN
    nt = tile_e.shape[0]
    grid_spec = pltpu.PrefetchScalarGridSpec(
        num_scalar_prefetch=4,
        grid=(nt,),
        in_specs=[pl.BlockSpec((tm * ROW_SLAB, LANES), lambda t, te, tv, ti, to: (ti[t], 0)),
                  pl.BlockSpec((None, D, de), lambda t, te, tv, ti, to: (te[t], 0, 0)),
                  pl.BlockSpec((None, D, de), lambda t, te, tv, ti, to: (te[t], 0, 0)),
                  pl.BlockSpec((None, de, D), lambda t, te, tv, ti, to: (te[t], 0, 0))],
        out_specs=pl.BlockSpec((tm * ROW_SLAB, LANES), lambda t, te, tv, ti, to: (to[t], 0)),
        scratch_shapes=[pltpu.VMEM((tm, D), BF16)],
    )
    return pl.pallas_call(
        _ffn_kernel,
        grid_spec=grid_spec,
        out_shape=jax.ShapeDtypeStruct(xs.shape, jnp.uint32),
        compiler_params=_cp("arbitrary"),
        name="ffn",
    )(tile_e, tile_valid, tile_in, tile_out, xs, wg, wu, wd)


def _combine_kernel(dcur_ref, dnext_ref, route_ref, x1_ref, gate2_ref, g_ref, ys_hbm, o_ref, buf, y_ref, sems):
    i = pl.program_id(0)
    nsteps = pl.num_programs(0)
    tm = x1_ref.shape[0]
    half = x1_ref.shape[1] // 2

    def copy(slot, t, s, d):
        return pltpu.make_async_copy(ys_hbm.at[_slab(d)], buf.at[slot, s, _slab(t)], sems.at[slot])

    def start_tile(slot, dref):
        def body(t, c):
            for s in range(TOP_K):
                copy(slot, t, s, dref[0, TOP_K * t + s]).start(priority=s % 2)
            return c
        lax.fori_loop(0, tm, body, 0, unroll=8)

    def wait_tile(slot, dref):
        def body(t, c):
            for s in range(TOP_K):
                copy(slot, t, s, dref[0, TOP_K * t + s]).wait()
            return c
        lax.fori_loop(0, tm, body, 0, unroll=8)

    @pl.when(i == 0)
    def _():
        start_tile(0, dcur_ref)

    for slot in range(2):
        @pl.when(jnp.logical_and(i + 1 < nsteps, (i + 1) % 2 == slot))
        def _():
            start_tile(slot, dnext_ref)

    for slot in range(2):
        @pl.when(i % 2 == slot)
        def _():
            wait_tile(slot, dcur_ref)
            w0 = route_ref[:, 2:3]
            w1 = route_ref[:, 3:4]
            for j in range(ROW_SLAB):
                a = buf[slot, 0, pl.ds(j, tm, stride=ROW_SLAB), :]
                b = buf[slot, 1, pl.ds(j, tm, stride=ROW_SLAB), :]
                y_ref[:, j * LANES:(j + 1) * LANES] = _unpack_lo(a) * w0 + _unpack_lo(b) * w1
                y_ref[:, half + j * LANES:half + (j + 1) * LANES] = _unpack_hi(a) * w0 + _unpack_hi(b) * w1

    o_ref[...] = x1_ref[...] + gate2_ref[...] * (_rms(y_ref[...]) * g_ref[...])


def _combine(ys, dest, route, x1, mod3, g_post, seq):
    N, D = x1.shape
    tm = TM_COMBINE
    tpb = seq // tm
    nsteps = N // tm
    dest3 = dest.reshape(nsteps, 1, TOP_K * tm)
    return pl.pallas_call(
        _combine_kernel,
        grid=(nsteps,),
        in_specs=[pl.BlockSpec((None, 1, TOP_K * tm), lambda i: (i, 0, 0), memory_space=pltpu.SMEM),
                  pl.BlockSpec((None, 1, TOP_K * tm), lambda i: (jnp.minimum(i + 1, nsteps - 1), 0, 0),
                               memory_space=pltpu.SMEM),
                  pl.BlockSpec((tm, LANES), lambda i: (i, 0)),
                  pl.BlockSpec((tm, D), lambda i: (i, 0)),
                  pl.BlockSpec((None, 1, D), lambda i: (i // tpb, 0, 5)),
                  pl.BlockSpec((1, D), lambda i: (0, 0)),
                  pl.BlockSpec(memory_space=pl.ANY)],
        out_specs=pl.BlockSpec((tm, D), lambda i: (i, 0)),
        out_shape=jax.ShapeDtypeStruct((N, D), F32),
        scratch_shapes=[pltpu.VMEM((2, TOP_K, tm * ROW_SLAB, LANES), jnp.uint32), pltpu.VMEM((tm, D), F32),
                        pltpu.SemaphoreType.DMA((2,))],
        compiler_params=_cp("arbitrary"),
        name="combine",
    )(dest3, dest3, route, x1, mod3, g_post, ys)


def _routing_tables(route, rank, cnt, n_experts, tm):
    N = route.shape[0]
    A = N * TOP_K
    counts = cnt[0, :n_experts].astype(jnp.int32)
    padded = ((counts + tm - 1) // tm) * tm
    pend = jnp.cumsum(padded)
    pstart = pend - padded
    dest = jnp.take(pstart, route[:, :TOP_K].astype(jnp.int32)) + rank[:, :TOP_K].astype(jnp.int32)
    nt = A // tm + n_experts
    n_used = pend[-1] // tm
    tidx = jnp.arange(nt, dtype=jnp.int32)
    tstart = tidx * tm
    te = jnp.minimum(jnp.sum(pend[None, :] <= tstart[:, None], axis=1), n_experts - 1).astype(jnp.int32)
    tv = jnp.clip(counts[te] - (tstart - pstart[te]), 0, tm).astype(jnp.int32)
    used = tidx < n_used
    last = jnp.maximum(n_used - 1, 0)
    te = jnp.where(used, te, te[last]).astype(jnp.int32)
    ti = jnp.where(used, tidx, last).astype(jnp.int32)
    to = tidx
    return dest.astype(jnp.int32), te, tv, ti, to, nt


def kernel(x, c, w_mod, b_mod, g_pre_mix, g_post_mix, w_in, g_gmlp_v, w_spatial, b_spatial, g_out_gmlp,
           g_out_attn, w_out, g_pre_ffn, g_post_ffn, w_router_group, b_router_group, w_router_expert,
           b_router_expert, w_gate, w_up, w_down):
    B, S, D = x.shape
    N = B * S
    depth = w_mod.shape[0]
    d_gmlp = g_gmlp_v.shape[1]
    d_attn = g_out_attn.shape[1]
    heads = w_spatial.shape[1]
    n_groups, epg = b_router_expert.shape[1], b_router_expert.shape[2]
    n_experts = n_groups * epg
    assert d_gmlp == d_attn and w_in.shape[2] == 2 * d_gmlp + 3 * d_attn
    assert n_groups + n_experts <= LANES

    x2 = x.reshape(N, D)
    for l in range(depth):
        mod3 = _modulation(c, w_mod[l], b_mod[l]).reshape(B, 1, w_mod.shape[2])

        pa = _inproj(x2, mod3, g_pre_mix[l][None], w_in[l].astype(BF16), g_gmlp_v[l][None], S)
        ya_n = _gmlp(pa, w_spatial[l], b_spatial[l], g_out_gmlp[l][None])
        ob = _attention(pa.reshape(B, S, pa.shape[1]), heads, d_attn, (2 * d_gmlp) // d_attn).reshape(N, d_attn)

        wr32 = jnp.concatenate([w_router_group[l],
                                jnp.transpose(w_router_expert[l], (1, 0, 2)).reshape(D, n_experts)], axis=1)
        wr32 = jnp.pad(wr32, ((0, 0), (0, LANES - wr32.shape[1])))
        wr_hi = wr32.astype(BF16)
        wr = jnp.concatenate([wr_hi, (wr32 - wr_hi.astype(F32)).astype(BF16)], axis=1)
        br = jnp.pad(jnp.concatenate([b_router_group[l], b_router_expert[l].reshape(n_experts)]),
                     (0, LANES - n_groups - n_experts))[None]

        x1, xl, route = _outproj(ya_n, ob, x2, w_out[l].astype(BF16), g_out_attn[l][None], g_post_mix[l][None],
                                 mod3, g_pre_ffn[l][None], wr, br, S, n_groups, epg)

        rank, cnt = _rank(route)
        dest, te, tv, ti, to, nt = _routing_tables(route, rank, cnt, n_experts, TM_FFN)
        xs = _dispatch(xl, dest, nt * TM_FFN)
        ys = _ffn(xs, te, tv, ti, to, w_gate[l].astype(BF16), w_up[l].astype(BF16), w_down[l].astype(BF16))
        x2 = _combine(ys, dest, route, x1, mod3, g_post_ffn[l][None], S)
    return x2.reshape(B, S, D)
```

```python
import functools
import math

import jax
import jax.numpy as jnp
from jax import lax
from jax.experimental import pallas as pl
from jax.experimental.pallas import tpu as pltpu

F32 = jnp.float32
BF16 = jnp.bfloat16
EPS = 1e-6
NEG = -1e30

DILATION_PATTERNS = ((128, 1), (512, 4), (2048, 16))
ATTN_BLOCK = 128
TOP_K = 2
LANES = 128

TM_INPROJ = 512
SUB_INPROJ = 256
TM_GMLP = 512
ATTN_WINDOW = ATTN_BLOCK * max(d for _, d in DILATION_PATTERNS)
ATTN_GROUP = 256
ATTN_HEADS_PER_STEP = 2
ATTN_UNROLL = 4
TM_OUTPROJ = 512
SUB_OUTPROJ = 256
TM_RANK = 512
TM_DISPATCH = 512
TM_FFN = 256
FFN_CAST_CHUNKS = 8
TM_COMBINE = 256
TN_MOD = 1024
ROW_SLAB = 8
VMEM_LIMIT = 56 * 1024 * 1024


def _cp(*dims):
    return pltpu.CompilerParams(dimension_semantics=dims, vmem_limit_bytes=VMEM_LIMIT)


def _rms(x):
    return x * lax.rsqrt(jnp.mean(x * x, axis=-1, keepdims=True) + EPS)


def _gelu_tanh(x):
    c = math.sqrt(2.0 / math.pi)
    return x * (0.5 * (1.0 + jnp.tanh(c * (x + 0.044715 * (x * x * x)))))


def _sigmoid(x):
    return 1.0 / (1.0 + jnp.exp(-x))


def _pack_rows(x):
    half = x.shape[1] // 2
    lo = pltpu.bitcast(x[:, :half].astype(BF16).astype(F32), jnp.uint32)
    hi = pltpu.bitcast(x[:, half:].astype(BF16).astype(F32), jnp.uint32)
    return lax.shift_right_logical(lo, jnp.uint32(16)) | (hi & jnp.uint32(0xFFFF0000))


def _unpack_lo(w):
    return pltpu.bitcast(lax.shift_left(w, jnp.uint32(16)), F32)


def _unpack_hi(w):
    return pltpu.bitcast(w & jnp.uint32(0xFFFF0000), F32)


def _store_slabs(ref, words, row0=0):
    rows = words.shape[0]
    for j in range(ROW_SLAB):
        ref[pl.ds(row0 * ROW_SLAB + j, rows, stride=ROW_SLAB), :] = words[:, j * LANES:(j + 1) * LANES]


def _mod_kernel(ct_ref, w_ref, b_ref, o_ref):
    ct = ct_ref[...]
    at = ct * _sigmoid(ct)
    w = w_ref[...]
    for b in range(o_ref.shape[0]):
        o_ref[b:b + 1, :] = jnp.sum(w * at[:, b:b + 1], axis=0, keepdims=True) + b_ref[...]


def _modulation(c, w_mod, b_mod):
    B, D = c.shape
    n_out = w_mod.shape[1]
    return pl.pallas_call(
        _mod_kernel,
        grid=(n_out // TN_MOD,),
        in_specs=[pl.BlockSpec((D, B), lambda j: (0, 0)),
                  pl.BlockSpec((D, TN_MOD), lambda j: (0, j)),
                  pl.BlockSpec((1, TN_MOD), lambda j: (0, j))],
        out_specs=pl.BlockSpec((B, TN_MOD), lambda j: (0, j)),
        out_shape=jax.ShapeDtypeStruct((B, n_out), F32),
        compiler_params=_cp("arbitrary"),
        name="mod",
    )(c.T, w_mod, b_mod.reshape(1, n_out))


def _inproj_kernel(x_ref, g_ref, sc_ref, sh_ref, w_ref, gv_ref, o_ref, h_ref, *, sub):
    s = pl.program_id(0)
    last = pl.num_programs(0) - 1
    tm = x_ref.shape[0]
    tn = gv_ref.shape[1]

    def normalise():
        h = (_rms(x_ref[...]) * g_ref[...]) * (1.0 + sc_ref[...]) + sh_ref[...]
        h_ref[pl.ds(pl.multiple_of((s % 2) * tm, tm), tm), :] = h.astype(BF16)

    def project():
        base = pl.multiple_of(((s + 1) % 2) * tm, tm)
        for j in range(w_ref.shape[1] // tn):
            cs = slice(j * tn, (j + 1) * tn)
            for k in range(tm // sub):
                acc = jnp.dot(h_ref[pl.ds(base + k * sub, sub), :], w_ref[:, cs], preferred_element_type=F32)
                if j == 0:
                    acc = _gelu_tanh(acc)
                elif j == 1:
                    v = _gelu_tanh(acc)
                    vc = v - jnp.mean(v, axis=-1, keepdims=True)
                    acc = vc * lax.rsqrt(jnp.mean(vc * vc, axis=-1, keepdims=True) + EPS) * gv_ref[...]
                o_ref[k * sub:(k + 1) * sub, cs] = acc.astype(BF16)

    @pl.when(s == 0)
    def _():
        normalise()

    @pl.when(jnp.logical_and(s > 0, s < last))
    def _():
        normalise()
        project()

    @pl.when(s == last)
    def _():
        project()


def _inproj(x2, mod3, g_pre, w_in_bf, g_v, seq):
    N, D = x2.shape
    d_in = w_in_bf.shape[1]
    tn = g_v.shape[1]
    tm = TM_INPROJ
    tpb = seq // tm
    nt = N // tm
    cur = lambda s: jnp.minimum(s, nt - 1)
    return pl.pallas_call(
        functools.partial(_inproj_kernel, sub=SUB_INPROJ),
        grid=(nt + 1,),
        in_specs=[pl.BlockSpec((tm, D), lambda s: (cur(s), 0)),
                  pl.BlockSpec((1, D), lambda s: (0, 0)),
                  pl.BlockSpec((None, 1, D), lambda s: (cur(s) // tpb, 0, 1)),
                  pl.BlockSpec((None, 1, D), lambda s: (cur(s) // tpb, 0, 0)),
                  pl.BlockSpec((D, d_in), lambda s: (0, 0), pipeline_mode=pl.Buffered(1)),
                  pl.BlockSpec((1, tn), lambda s: (0, 0))],
        out_specs=pl.BlockSpec((tm, d_in), lambda s: (jnp.maximum(s - 1, 0), 0)),
        out_shape=jax.ShapeDtypeStruct((N, d_in), BF16),
        scratch_shapes=[pltpu.VMEM((2 * tm, D), BF16)],
        compiler_params=_cp("arbitrary"),
        name="inproj",
    )(x2, g_pre, mod3, mod3, w_in_bf, g_v)


def _gmlp_kernel(u_ref, v_ref, w_ref, bt_ref, g_ref, o_ref, ya_ref, *, heads, chunk, hd):
    tm = u_ref.shape[0]
    row = lax.broadcasted_iota(jnp.int32, (chunk, chunk), 0)
    col = lax.broadcasted_iota(jnp.int32, (chunk, chunk), 1)
    causal = col <= row
    for h in range(heads):
        wm = jnp.where(causal, w_ref[h], 0.0).astype(BF16)
        bcol = bt_ref[:, h:h + 1]
        cs = slice(h * hd, (h + 1) * hd)
        for c in range(tm // chunk):
            rs = slice(c * chunk, (c + 1) * chunk)
            sv = jnp.dot(wm, v_ref[rs, cs], preferred_element_type=F32) + bcol
            ya_ref[rs, cs] = u_ref[rs, cs].astype(F32) * sv
    o_ref[...] = (_rms(ya_ref[...]) * g_ref[...]).astype(BF16)


def _gmlp(pa, w_spatial, b_spatial, g_out):
    N = pa.shape[0]
    heads, chunk, _ = w_spatial.shape
    dg = g_out.shape[1]
    kern = functools.partial(_gmlp_kernel, heads=heads, chunk=chunk, hd=dg // heads)
    return pl.pallas_call(
        kern,
        grid=(N // TM_GMLP,),
        in_specs=[pl.BlockSpec((TM_GMLP, dg), lambda i: (i, 0)),
                  pl.BlockSpec((TM_GMLP, dg), lambda i: (i, 1)),
                  pl.BlockSpec((heads, chunk, chunk), lambda i: (0, 0, 0)),
                  pl.BlockSpec((chunk, heads), lambda i: (0, 0)),
                  pl.BlockSpec((1, dg), lambda i: (0, 0))],
        out_specs=pl.BlockSpec((TM_GMLP, dg), lambda i: (i, 0)),
        out_shape=jax.ShapeDtypeStruct((N, dg), BF16),
        scratch_shapes=[pltpu.VMEM((TM_GMLP, dg), F32)],
        compiler_params=_cp("arbitrary"),
        name="gmlp",
    )(pa, pa, w_spatial, b_spatial.T, g_out)


def _perm_matrix(d, n):
    i = jnp.arange(n)
    src = (i % (n // d)) * d + i // (n // d)
    return (src[:, None] == jnp.arange(n)[None, :]).astype(BF16)


def _attn_kernel(q_ref, kc_ref, kp_ref, vc_ref, vp_ref, p4_ref, p16_ref, o_ref,
                 qd_ref, kd_ref, va_ref, op_ref, lp_ref, bias_ref, *, hd, dils):
    n = pl.program_id(2)
    blk = ATTN_BLOCK
    grp = ATTN_GROUP
    W = q_ref.shape[0]
    nh = q_ref.shape[1] // hd
    scale = hd ** -0.5
    c2 = scale * math.log2(math.e)
    nt = (((1,), (1,)), ((), ()))

    ri = lax.broadcasted_iota(jnp.int32, (blk, 2 * blk), 0)
    ci = lax.broadcasted_iota(jnp.int32, (blk, 2 * blk), 1)
    band = jnp.where(ci >= ri, jnp.where(ci <= ri + blk, 0.0, NEG), NEG)
    bias_ref[0] = band
    bias_ref[1] = jnp.where(n > 0, band, jnp.where(ci < blk, NEG, band))

    def put_v(rows, x):
        ones = jnp.ones((x.shape[0], hd), BF16)
        for hh in range(nh):
            parts = [x[:, hh * hd:(hh + 1) * hd] if c == hh else ones for c in range(nh)]
            va_ref[hh, rows, :] = jnp.concatenate(parts, axis=1)

    def head_blocks(items):
        ss = [lax.dot_general(lq(), lk(), nt, preferred_element_type=F32) + lb()
              for lq, lk, _, _, lb, _ in items]
        ms = [jnp.max(s, axis=-1, keepdims=True) for s in ss]
        ps = [jnp.exp2((s - m) * c2).astype(BF16) for s, m in zip(ss, ms)]
        for p, m, (_, _, lva, hh, _, store) in zip(ps, ms, items):
            r = jnp.dot(p, lva(), preferred_element_type=F32)
            oh = (hh + 1) % nh
            l = r[:, oh * hd:(oh + 1) * hd]
            store(r[:, hh * hd:(hh + 1) * hd] / l, m * scale + jnp.log(l))

    def run_pattern(pi, d):
        ql = W // d
        kl = ql + blk
        nb = ql // blk
        unroll = ATTN_UNROLL
        assert nb % unroll == 0 or unroll % nb == 0

        def aligned(x):
            return x if isinstance(x, int) else pl.multiple_of(x, blk)

        def loop_body(it, carry):
            items = []
            for u in range(unroll):
                if nb >= unroll:
                    bodies_per_class = nb // unroll
                    r = 0 if d == 1 else it // bodies_per_class
                    jb = u if bodies_per_class == 1 else (it % bodies_per_class) * unroll + u
                else:
                    r = it * (unroll // nb) + u // nb
                    jb = u % nb
                qrow = aligned(r * ql + jb * blk)
                krow = aligned(r * kl + jb * blk)
                first_blk = jb == 0
                orow = aligned(jb * blk) if d == 1 else jb * (blk * d) + r
                for hh in range(nh):
                    cs = slice(hh * hd, (hh + 1) * hd)

                    def store(o, lse, hh=hh, orow=orow):
                        if d == 1:
                            rows = pl.ds(orow, blk)
                        else:
                            rows = pl.ds(orow, blk, stride=d)
                        op_ref[pi, hh, rows, :] = o
                        lp_ref[pi, hh, rows, :] = lse

                    items.append((
                        lambda qrow=qrow, cs=cs: (q_ref if d == 1 else qd_ref)[pl.ds(qrow, blk), cs],
                        lambda krow=krow, cs=cs: kd_ref[pl.ds(krow, 2 * blk), cs],
                        lambda krow=krow, hh=hh: va_ref[hh, pl.ds(krow, 2 * blk), :],
                        hh,
                        lambda first_blk=first_blk: bias_ref[
                            int(first_blk) if isinstance(first_blk, bool) else jnp.where(first_blk, 1, 0)],
                        store))
            head_blocks(items)
            return carry

        lax.fori_loop(0, (d * nb) // unroll, loop_body, 0)

    def deinterleave(d, p_ref):
        pc = grp // d
        ql = W // d
        kl = ql + blk
        ng = W // grp
        pm = p_ref[...]

        def split(x):
            return jnp.dot(pm, x, preferred_element_type=F32).astype(BF16)

        for g in range(ng):
            rows = slice(g * grp, (g + 1) * grp)
            yq, yk, yv = split(q_ref[rows, :]), split(kc_ref[rows, :]), split(vc_ref[rows, :])
            for r in range(d):
                piece = slice(r * pc, (r + 1) * pc)
                qd_ref[r * ql + g * pc:r * ql + (g + 1) * pc, :] = yq[piece]
                kd_ref[r * kl + blk + g * pc:r * kl + blk + (g + 1) * pc, :] = yk[piece]
                put_v(slice(r * kl + blk + g * pc, r * kl + blk + (g + 1) * pc), yv[piece])
        g0 = ng - (blk * d) // grp
        for g in range(g0, ng):
            rows = slice(g * grp, (g + 1) * grp)
            yk, yv = split(kp_ref[rows, :]), split(vp_ref[rows, :])
            for r in range(d):
                piece = slice(r * pc, (r + 1) * pc)
                dst = r * kl + (g - g0) * pc
                kd_ref[dst:dst + pc, :] = yk[piece]
                put_v(slice(dst, dst + pc), yv[piece])

    for pi, d in enumerate(dils):
        if d == 1:
            kd_ref[0:blk, :] = kp_ref[W - blk:W, :]
            kd_ref[blk:blk + W, :] = kc_ref[...]
            put_v(slice(0, blk), vp_ref[W - blk:W, :])
            put_v(slice(blk, blk + W), vc_ref[...])
        else:
            deinterleave(d, p4_ref if d == 4 else p16_ref)
        run_pattern(pi, d)

    fin = 2 * blk

    def fin_body(c, carry):
        rows = pl.ds(pl.multiple_of(c * fin, fin), fin)
        for hh in range(nh):
            ls = [lp_ref[pi, hh, rows, :] for pi in range(len(dils))]
            mx = functools.reduce(jnp.maximum, ls)
            ws = [jnp.exp(l - mx) for l in ls]
            num = sum(w * op_ref[pi, hh, rows, :] for pi, w in enumerate(ws))
            o_ref[rows, hh * hd:(hh + 1) * hd] = num / sum(ws)
        return carry

    lax.fori_loop(0, W // fin, fin_body, 0)


def _attention(pa3, heads, d_attn, col0):
    B, S, _ = pa3.shape
    hd = d_attn // heads
    dils = tuple(d for _, d in DILATION_PATTERNS)
    assert all(w == ATTN_BLOCK * d for w, d in DILATION_PATTERNS) and dils == (1, 4, 16)
    W = ATTN_WINDOW
    lanes = ATTN_HEADS_PER_STEP * hd
    cb = d_attn // lanes
    assert S % W == 0 and W % ATTN_GROUP == 0 and hd == LANES

    def cur(c):
        return pl.BlockSpec((None, W, lanes), lambda b, hp, n: (b, n, (col0 + c) * cb + hp))

    def prv(c):
        return pl.BlockSpec((None, W, lanes), lambda b, hp, n: (b, jnp.maximum(n - 1, 0), (col0 + c) * cb + hp))

    perm = pl.BlockSpec((ATTN_GROUP, ATTN_GROUP), lambda b, hp, n: (0, 0))
    kern = functools.partial(_attn_kernel, hd=hd, dils=dils)
    npat = len(dils)
    return pl.pallas_call(
        kern,
        grid=(B, cb, S // W),
        in_specs=[cur(0), cur(1), prv(1), cur(2), prv(2), perm, perm],
        out_specs=pl.BlockSpec((None, W, lanes), lambda b, hp, n: (b, n, hp)),
        out_shape=jax.ShapeDtypeStruct((B, S, d_attn), F32),
        scratch_shapes=[pltpu.VMEM((W, lanes), BF16),
                        pltpu.VMEM((2 * W, lanes), BF16),
                        pltpu.VMEM((ATTN_HEADS_PER_STEP, 2 * W, lanes), BF16),
                        pltpu.VMEM((npat, ATTN_HEADS_PER_STEP, W, hd), F32),
                        pltpu.VMEM((npat, ATTN_HEADS_PER_STEP, W, hd), F32),
                        pltpu.VMEM((2, ATTN_BLOCK, 2 * ATTN_BLOCK), F32)],
        compiler_params=_cp("arbitrary", "arbitrary", "arbitrary"),
        name="attn",
    )(pa3, pa3, pa3, pa3, pa3, _perm_matrix(4, ATTN_GROUP), _perm_matrix(16, ATTN_GROUP))


def _outproj_kernel(ya_ref, ob_ref, x_ref, w_ref, gattn_ref, gpost_ref, gate1_ref, gpre_ref,
                    sc2_ref, sh2_ref, wr_ref, br_ref, x1_ref, xl_ref, route_ref, cat_ref, h2_ref,
                    *, n_groups, epg, sub):
    dg = ya_ref.shape[1]
    half = x_ref.shape[1] // 2
    subs = [slice(k * sub, (k + 1) * sub) for k in range(ya_ref.shape[0] // sub)]
    for rs in subs:
        cat_ref[rs, :dg] = ya_ref[rs, :]
        cat_ref[rs, dg:] = (_rms(ob_ref[rs, :]) * gattn_ref[...]).astype(BF16)
    ys = [jnp.dot(cat_ref[rs, :], w_ref[...], preferred_element_type=F32) for rs in subs]
    for rs, y in zip(subs, ys):
        x1 = x_ref[rs, :] + gate1_ref[...] * (_rms(y) * gpost_ref[...])
        x1_ref[rs, :] = x1
        h2_ref[rs, :] = (_rms(x1) * gpre_ref[...]) * (1.0 + sc2_ref[...]) + sh2_ref[...]
    for k, rs in enumerate(subs):
        _route_rows(h2_ref[rs, :], wr_ref, br_ref, xl_ref, route_ref, rs, k * sub, half, n_groups, epg)


def _route_rows(h2, wr_ref, br_ref, xl_ref, route_ref, rs, row0, half, n_groups, epg):
    tm = h2.shape[0]
    hb = h2.astype(BF16)
    hb32 = hb.astype(F32)
    words = (lax.shift_right_logical(pltpu.bitcast(hb32[:, :half], jnp.uint32), jnp.uint32(16))
             | (pltpu.bitcast(hb32[:, half:], jnp.uint32) & jnp.uint32(0xFFFF0000)))
    _store_slabs(xl_ref, words, row0)

    lo = (h2 - hb32).astype(BF16)
    r = (jnp.dot(hb, wr_ref[...], preferred_element_type=F32)
         + jnp.dot(lo, wr_ref[...], preferred_element_type=F32))
    logits = r[:, :LANES] + r[:, LANES:] + br_ref[...]

    lane = lax.broadcasted_iota(jnp.int32, (tm, LANES), 1)
    lane_f = lane.astype(F32)
    big = float(LANES)
    lg = jnp.where(lane < n_groups, logits, NEG)
    mg = jnp.max(lg, axis=-1, keepdims=True)
    gi = jnp.min(jnp.where(lg == mg, lane_f, big), axis=-1, keepdims=True)
    gp = 1.0 / jnp.sum(jnp.exp(lg - mg), axis=-1, keepdims=True)
    e_lo = n_groups + gi * epg
    le = jnp.where(lane_f >= e_lo, jnp.where(lane_f < e_lo + epg, logits, NEG), NEG)
    m1 = jnp.max(le, axis=-1, keepdims=True)
    i1 = jnp.min(jnp.where(le == m1, lane_f, big), axis=-1, keepdims=True)
    le2 = jnp.where(lane_f == i1, NEG, le)
    m2 = jnp.max(le2, axis=-1, keepdims=True)
    i2 = jnp.min(jnp.where(le2 == m2, lane_f, big), axis=-1, keepdims=True)
    t = jnp.exp(m2 - m1)
    w1 = gp / (1.0 + t)
    w2 = gp * t / (1.0 + t)
    route_ref[rs, :] = jnp.where(lane == 0, i1 - n_groups,
                                 jnp.where(lane == 1, i2 - n_groups,
                                           jnp.where(lane == 2, w1, jnp.where(lane == 3, w2, 0.0))))


def _outproj(ya_n, ob, x2, w_out_bf, g_attn, g_post, mod3, g_pre, wr, br, seq, n_groups, epg):
    N, D = x2.shape
    dg = ya_n.shape[1]
    tm = TM_OUTPROJ
    tpb = seq // tm
    assert D == 2 * ROW_SLAB * LANES
    row = lambda i: (i, 0)
    const = lambda i: (0, 0)
    modc = lambda k: pl.BlockSpec((None, 1, D), lambda i: (i // tpb, 0, k))
    resident = dict(pipeline_mode=pl.Buffered(1))
    kern = functools.partial(_outproj_kernel, n_groups=n_groups, epg=epg, sub=SUB_OUTPROJ)
    return pl.pallas_call(
        kern,
        grid=(N // tm,),
        in_specs=[pl.BlockSpec((tm, dg), row), pl.BlockSpec((tm, dg), row), pl.BlockSpec((tm, D), row),
                  pl.BlockSpec((D, D), const, **resident), pl.BlockSpec((1, dg), const), pl.BlockSpec((1, D), const),
                  modc(2), pl.BlockSpec((1, D), const), modc(4), modc(3),
                  pl.BlockSpec((D, 2 * LANES), const, **resident), pl.BlockSpec((1, LANES), const)],
        out_specs=[pl.BlockSpec((tm, D), row), pl.BlockSpec((tm * ROW_SLAB, LANES), row),
                   pl.BlockSpec((tm, LANES), row)],
        out_shape=[jax.ShapeDtypeStruct((N, D), F32), jax.ShapeDtypeStruct((N * ROW_SLAB, LANES), jnp.uint32),
                   jax.ShapeDtypeStruct((N, LANES), F32)],
        scratch_shapes=[pltpu.VMEM((tm, D), BF16), pltpu.VMEM((tm, D), F32)],
        compiler_params=_cp("arbitrary"),
        name="outproj",
    )(ya_n, ob, x2, w_out_bf, g_attn, g_post, mod3, g_pre, mod3, mod3, wr, br)


def _rank_kernel(route_ref, rank_ref, cnt_ref, carry_ref):
    tr = route_ref.shape[0]

    @pl.when(pl.program_id(0) == 0)
    def _():
        carry_ref[...] = jnp.zeros_like(carry_ref)

    lane = lax.broadcasted_iota(jnp.int32, (tr, LANES), 1)
    lane_f = lane.astype(F32)
    oh0 = lane_f == route_ref[:, 0:1]
    oh1 = lane_f == route_ref[:, 1:2]
    oh = jnp.where(oh0, 1.0, jnp.where(oh1, 1.0, 0.0))
    ri = lax.broadcasted_iota(jnp.int32, (tr, tr), 0)
    ci = lax.broadcasted_iota(jnp.int32, (tr, tr), 1)
    earlier = jnp.where(ci < ri, 1.0, 0.0).astype(BF16)
    before = jnp.dot(earlier, oh.astype(BF16), preferred_element_type=F32) + carry_ref[...]
    r0 = jnp.sum(jnp.where(oh0, before, 0.0), axis=-1, keepdims=True)
    r1 = jnp.sum(jnp.where(oh1, before, 0.0), axis=-1, keepdims=True)
    rank_ref[...] = jnp.where(lane == 0, r0, jnp.where(lane == 1, r1, 0.0))
    carry_ref[...] = carry_ref[...] + jnp.sum(oh, axis=0, keepdims=True)
    cnt_ref[...] = carry_ref[...]


def _rank(route):
    N = route.shape[0]
    tr = TM_RANK
    return pl.pallas_call(
        _rank_kernel,
        grid=(N // tr,),
        in_specs=[pl.BlockSpec((tr, LANES), lambda i: (i, 0))],
        out_specs=[pl.BlockSpec((tr, LANES), lambda i: (i, 0)), pl.BlockSpec((1, LANES), lambda i: (0, 0))],
        out_shape=[jax.ShapeDtypeStruct((N, LANES), F32), jax.ShapeDtypeStruct((1, LANES), F32)],
        scratch_shapes=[pltpu.VMEM((1, LANES), F32)],
        compiler_params=_cp("arbitrary"),
        name="rank",
    )(route)


def _slab(i):
    return pl.ds(pl.multiple_of(i * ROW_SLAB, ROW_SLAB), ROW_SLAB)


def _dispatch_kernel(dest_ref, xl_ref, xs_hbm, sem):
    tm = xl_ref.shape[0] // ROW_SLAB

    def copy(t, d):
        return pltpu.make_async_copy(xl_ref.at[_slab(t)], xs_hbm.at[_slab(d)], sem)

    def start(t, c):
        for s in range(TOP_K):
            copy(t, dest_ref[0, TOP_K * t + s]).start(priority=s % 2)
        return c

    def wait(t, c):
        for s in range(TOP_K):
            copy(t, dest_ref[0, TOP_K * t + s]).wait()
        return c

    lax.fori_loop(0, tm, start, 0, unroll=8)
    lax.fori_loop(0, tm, wait, 0, unroll=8)


def _dispatch(xl, dest, n_rows):
    N = xl.shape[0] // ROW_SLAB
    tm = TM_DISPATCH
    return pl.pallas_call(
        _dispatch_kernel,
        grid=(N // tm,),
        in_specs=[pl.BlockSpec((None, 1, TOP_K * tm), lambda i: (i, 0, 0), memory_space=pltpu.SMEM),
                  pl.BlockSpec((tm * ROW_SLAB, LANES), lambda i: (i, 0))],
        out_specs=pl.BlockSpec(memory_space=pl.ANY),
        out_shape=jax.ShapeDtypeStruct((n_rows * ROW_SLAB, LANES), jnp.uint32),
        scratch_shapes=[pltpu.SemaphoreType.DMA(())],
        compiler_params=_cp("arbitrary"),
        name="dispatch",
    )(dest.reshape(N // tm, 1, TOP_K * tm), xl)


def _ffn_kernel(te_ref, tv_ref, ti_ref, tf_ref, tn_ref, xs_ref, wg_hbm, wu_hbm, wd_hbm, ys_ref,
                x_ref, stage_g, stage_u, stage_d, wg_ref, wu_ref, wd_ref, sems):
    del ti_ref
    t = pl.program_id(0)
    nv = tv_ref[t]
    tm = x_ref.shape[0]
    half = x_ref.shape[1] // 2
    stages = ((wg_hbm, stage_g, wg_ref), (wu_hbm, stage_u, wu_ref), (wd_hbm, stage_d, wd_ref))

    def fetch(e):
        return [pltpu.make_async_copy(src.at[e], stage, sems.at[k]) for k, (src, stage, _) in enumerate(stages)]

    @pl.when(t == 0)
    def _():
        for c in fetch(te_ref[0]):
            c.start()

    @pl.when(tf_ref[t] == 1)
    def _():
        for c in fetch(0):
            c.wait()
        for _, stage, dst in stages:
            rows = stage.shape[0] // FFN_CAST_CHUNKS

            def cast(i, carry, stage=stage, dst=dst, rows=rows):
                rs = pl.ds(pl.multiple_of(i * rows, rows), rows)
                dst[rs, :] = stage[rs, :].astype(BF16)
                return carry

            lax.fori_loop(0, FFN_CAST_CHUNKS, cast, 0)

        @pl.when(tn_ref[t] >= 0)
        def _():
            for c in fetch(tn_ref[t]):
                c.start()

    @pl.when(nv > 0)
    def _():
        keep = lax.broadcasted_iota(jnp.int32, (tm, 1), 0) < nv
        for j in range(ROW_SLAB):
            w = jnp.where(keep, xs_ref[pl.ds(j, tm, stride=ROW_SLAB), :], jnp.uint32(0))
            x_ref[:, j * LANES:(j + 1) * LANES] = _unpack_lo(w).astype(BF16)
            x_ref[:, half + j * LANES:half + (j + 1) * LANES] = _unpack_hi(w).astype(BF16)
        x = x_ref[...]
        g = jnp.dot(x, wg_ref[...], preferred_element_type=F32)
        u = jnp.dot(x, wu_ref[...], preferred_element_type=F32)
        hm = ((g * _sigmoid(g)) * u).astype(BF16)
        _store_slabs(ys_ref, _pack_rows(jnp.dot(hm, wd_ref[...], preferred_element_type=F32)))

    @pl.when(nv == 0)
    def _():
        ys_ref[...] = jnp.zeros_like(ys_ref)


def _ffn(xs, tile_e, tile_valid, tile_in, tile_first, tile_next, wg, wu, wd):
    D, de = wg.shape[1], wg.shape[2]
    tm = TM_FFN
    nt = tile_e.shape[0]
    hbm = pl.BlockSpec(memory_space=pl.ANY)
    grid_spec = pltpu.PrefetchScalarGridSpec(
        num_scalar_prefetch=5,
        grid=(nt,),
        in_specs=[pl.BlockSpec((tm * ROW_SLAB, LANES), lambda t, te, tv, ti, tf, tn: (ti[t], 0)), hbm, hbm, hbm],
        out_specs=pl.BlockSpec((tm * ROW_SLAB, LANES), lambda t, te, tv, ti, tf, tn: (t, 0)),
        scratch_shapes=[pltpu.VMEM((tm, D), BF16),
                        pltpu.VMEM((D, de), F32), pltpu.VMEM((D, de), F32), pltpu.VMEM((de, D), F32),
                        pltpu.VMEM((D, de), BF16), pltpu.VMEM((D, de), BF16), pltpu.VMEM((de, D), BF16),
                        pltpu.SemaphoreType.DMA((3,))],
    )
    return pl.pallas_call(
        _ffn_kernel,
        grid_spec=grid_spec,
        out_shape=jax.ShapeDtypeStruct(xs.shape, jnp.uint32),
        compiler_params=_cp("arbitrary"),
        name="ffn",
    )(tile_e, tile_valid, tile_in, tile_first, tile_next, xs, wg, wu, wd)


def _combine_kernel(dcur_ref, dnext_ref, route_ref, x1_ref, gate2_ref, g_ref, ys_hbm, o_ref, buf, y_ref, sems):
    i = pl.program_id(0)
    nsteps = pl.num_programs(0)
    tm = x1_ref.shape[0]
    half = x1_ref.shape[1] // 2

    def copy(slot, t, s, d):
        return pltpu.make_async_copy(ys_hbm.at[_slab(d)], buf.at[slot, s, _slab(t)], sems.at[slot])

    def start_tile(slot, dref):
        def body(t, c):
            for s in range(TOP_K):
                copy(slot, t, s, dref[0, TOP_K * t + s]).start(priority=s % 2)
            return c
        lax.fori_loop(0, tm, body, 0, unroll=8)

    def wait_tile(slot, dref):
        def body(t, c):
            for s in range(TOP_K):
                copy(slot, t, s, dref[0, TOP_K * t + s]).wait()
            return c
        lax.fori_loop(0, tm, body, 0, unroll=8)

    @pl.when(i == 0)
    def _():
        start_tile(0, dcur_ref)

    for slot in range(2):
        @pl.when(jnp.logical_and(i + 1 < nsteps, (i + 1) % 2 == slot))
        def _():
            start_tile(slot, dnext_ref)

    for slot in range(2):
        @pl.when(i % 2 == slot)
        def _():
            wait_tile(slot, dcur_ref)
            w0 = route_ref[:, 2:3]
            w1 = route_ref[:, 3:4]
            for j in range(ROW_SLAB):
                a = buf[slot, 0, pl.ds(j, tm, stride=ROW_SLAB), :]
                b = buf[slot, 1, pl.ds(j, tm, stride=ROW_SLAB), :]
                y_ref[:, j * LANES:(j + 1) * LANES] = _unpack_lo(a) * w0 + _unpack_lo(b) * w1
                y_ref[:, half + j * LANES:half + (j + 1) * LANES] = _unpack_hi(a) * w0 + _unpack_hi(b) * w1

    o_ref[...] = x1_ref[...] + gate2_ref[...] * (_rms(y_ref[...]) * g_ref[...])


def _combine(ys, dest, route, x1, mod3, g_post, seq):
    N, D = x1.shape
    tm = TM_COMBINE
    tpb = seq // tm
    nsteps = N // tm
    dest3 = dest.reshape(nsteps, 1, TOP_K * tm)
    return pl.pallas_call(
        _combine_kernel,
        grid=(nsteps,),
        in_specs=[pl.BlockSpec((None, 1, TOP_K * tm), lambda i: (i, 0, 0), memory_space=pltpu.SMEM),
                  pl.BlockSpec((None, 1, TOP_K * tm), lambda i: (jnp.minimum(i + 1, nsteps - 1), 0, 0),
                               memory_space=pltpu.SMEM),
                  pl.BlockSpec((tm, LANES), lambda i: (i, 0)),
                  pl.BlockSpec((tm, D), lambda i: (i, 0)),
                  pl.BlockSpec((None, 1, D), lambda i: (i // tpb, 0, 5)),
                  pl.BlockSpec((1, D), lambda i: (0, 0)),
                  pl.BlockSpec(memory_space=pl.ANY)],
        out_specs=pl.BlockSpec((tm, D), lambda i: (i, 0)),
        out_shape=jax.ShapeDtypeStruct((N, D), F32),
        scratch_shapes=[pltpu.VMEM((2, TOP_K, tm * ROW_SLAB, LANES), jnp.uint32), pltpu.VMEM((tm, D), F32),
                        pltpu.SemaphoreType.DMA((2,))],
        compiler_params=_cp("arbitrary"),
        name="combine",
    )(dest3, dest3, route, x1, mod3, g_post, ys)


def _routing_tables(route, rank, cnt, n_experts, tm):
    N = route.shape[0]
    A = N * TOP_K
    counts = cnt[0, :n_experts].astype(jnp.int32)
    padded = ((counts + tm - 1) // tm) * tm
    pend = jnp.cumsum(padded)
    pstart = pend - padded
    dest = jnp.take(pstart, route[:, :TOP_K].astype(jnp.int32)) + rank[:, :TOP_K].astype(jnp.int32)
    nt = A // tm + n_experts
    n_used = pend[-1] // tm
    tidx = jnp.arange(nt, dtype=jnp.int32)
    tstart = tidx * tm
    te = jnp.minimum(jnp.sum(pend[None, :] <= tstart[:, None], axis=1), n_experts - 1).astype(jnp.int32)
    tv = jnp.clip(counts[te] - (tstart - pstart[te]), 0, tm).astype(jnp.int32)
    used = tidx < n_used
    last = jnp.maximum(n_used - 1, 0)
    te = jnp.where(used, te, te[last]).astype(jnp.int32)
    ti = jnp.where(used, tidx, last).astype(jnp.int32)
    first = used & ((tidx == 0) | (te != jnp.roll(te, 1)))
    nxt = lax.cummin(jnp.where(first, tidx, nt)[::-1])[::-1]
    nxt = jnp.concatenate([nxt[1:], jnp.full((1,), nt, jnp.int32)])
    tn = jnp.where(first & (nxt < nt), te[jnp.minimum(nxt, nt - 1)], -1).astype(jnp.int32)
    return dest.astype(jnp.int32), te, tv, ti, first.astype(jnp.int32), tn, nt


def kernel(x, c, w_mod, b_mod, g_pre_mix, g_post_mix, w_in, g_gmlp_v, w_spatial, b_spatial, g_out_gmlp,
           g_out_attn, w_out, g_pre_ffn, g_post_ffn, w_router_group, b_router_group, w_router_expert,
           b_router_expert, w_gate, w_up, w_down):
    B, S, D = x.shape
    N = B * S
    depth = w_mod.shape[0]
    d_gmlp = g_gmlp_v.shape[1]
    d_attn = g_out_attn.shape[1]
    heads = w_spatial.shape[1]
    n_groups, epg = b_router_expert.shape[1], b_router_expert.shape[2]
    n_experts = n_groups * epg
    assert d_gmlp == d_attn and w_in.shape[2] == 2 * d_gmlp + 3 * d_attn
    assert n_groups + n_experts <= LANES

    x2 = x.reshape(N, D)
    for l in range(depth):
        mod3 = _modulation(c, w_mod[l], b_mod[l]).reshape(B, 1, w_mod.shape[2])

        pa = _inproj(x2, mod3, g_pre_mix[l][None], w_in[l].astype(BF16), g_gmlp_v[l][None], S)
        ya_n = _gmlp(pa, w_spatial[l], b_spatial[l], g_out_gmlp[l][None])
        ob = _attention(pa.reshape(B, S, pa.shape[1]), heads, d_attn, (2 * d_gmlp) // d_attn).reshape(N, d_attn)

        wr32 = jnp.concatenate([w_router_group[l],
                                jnp.transpose(w_router_expert[l], (1, 0, 2)).reshape(D, n_experts)], axis=1)
        wr32 = jnp.pad(wr32, ((0, 0), (0, LANES - wr32.shape[1])))
        wr_hi = wr32.astype(BF16)
        wr = jnp.concatenate([wr_hi, (wr32 - wr_hi.astype(F32)).astype(BF16)], axis=1)
        br = jnp.pad(jnp.concatenate([b_router_group[l], b_router_expert[l].reshape(n_experts)]),
                     (0, LANES - n_groups - n_experts))[None]

        x1, xl, route = _outproj(ya_n, ob, x2, w_out[l].astype(BF16), g_out_attn[l][None], g_post_mix[l][None],
                                 mod3, g_pre_ffn[l][None], wr, br, S, n_groups, epg)

        rank, cnt = _rank(route)
        dest, te, tv, ti, tf, tn, nt = _routing_tables(route, rank, cnt, n_experts, TM_FFN)
        xs = _dispatch(xl, dest, nt * TM_FFN)
        ys = _ffn(xs, te, tv, ti, tf, tn, w_gate[l], w_up[l], w_down[l])
        x2 = _combine(ys, dest, route, x1, mod3, g_post_ffn[l][None], S)
    return x2.reshape(B, S, D)
```

```python
import functools
import math

import jax
import jax.numpy as jnp
from jax import lax
from jax.experimental import pallas as pl
from jax.experimental.pallas import tpu as pltpu

F32 = jnp.float32
BF16 = jnp.bfloat16
EPS = 1e-6
NEG = -1e30

DILATION_PATTERNS = ((128, 1), (512, 4), (2048, 16))
ATTN_BLOCK = 128
TOP_K = 2
LANES = 128

TM_INPROJ = 512
SUB_INPROJ = 256
TM_GMLP = 512
ATTN_WINDOW = ATTN_BLOCK * max(d for _, d in DILATION_PATTERNS)
ATTN_GROUP = 256
ATTN_HEADS_PER_STEP = 2
ATTN_UNROLL = 8
TM_OUTPROJ = 512
SUB_OUTPROJ = 256
TM_RANK = 512
TM_DISPATCH = 512
TM_FFN = 256
FFN_CAST_CHUNKS = 8
TM_COMBINE = 256
TN_MOD = 1024
ROW_SLAB = 8
VMEM_LIMIT = 56 * 1024 * 1024


def _cp(*dims):
    return pltpu.CompilerParams(dimension_semantics=dims, vmem_limit_bytes=VMEM_LIMIT)


def _rms(x):
    return x * lax.rsqrt(jnp.mean(x * x, axis=-1, keepdims=True) + EPS)


def _gelu_tanh(x):
    c = math.sqrt(2.0 / math.pi)
    return x * (0.5 * (1.0 + jnp.tanh(c * (x + 0.044715 * (x * x * x)))))


def _sigmoid(x):
    return 1.0 / (1.0 + jnp.exp(-x))


def _pack_rows(x):
    half = x.shape[1] // 2
    lo = pltpu.bitcast(x[:, :half].astype(BF16).astype(F32), jnp.uint32)
    hi = pltpu.bitcast(x[:, half:].astype(BF16).astype(F32), jnp.uint32)
    return lax.shift_right_logical(lo, jnp.uint32(16)) | (hi & jnp.uint32(0xFFFF0000))


def _unpack_lo(w):
    return pltpu.bitcast(lax.shift_left(w, jnp.uint32(16)), F32)


def _unpack_hi(w):
    return pltpu.bitcast(w & jnp.uint32(0xFFFF0000), F32)


def _store_slabs(ref, words, row0=0):
    rows = words.shape[0]
    for j in range(ROW_SLAB):
        ref[pl.ds(row0 * ROW_SLAB + j, rows, stride=ROW_SLAB), :] = words[:, j * LANES:(j + 1) * LANES]


def _mod_kernel(ct_ref, w_ref, b_ref, o_ref):
    ct = ct_ref[...]
    at = ct * _sigmoid(ct)
    w = w_ref[...]
    for b in range(o_ref.shape[0]):
        o_ref[b:b + 1, :] = jnp.sum(w * at[:, b:b + 1], axis=0, keepdims=True) + b_ref[...]


def _modulation(c, w_mod, b_mod):
    B, D = c.shape
    n_out = w_mod.shape[1]
    return pl.pallas_call(
        _mod_kernel,
        grid=(n_out // TN_MOD,),
        in_specs=[pl.BlockSpec((D, B), lambda j: (0, 0)),
                  pl.BlockSpec((D, TN_MOD), lambda j: (0, j)),
                  pl.BlockSpec((1, TN_MOD), lambda j: (0, j))],
        out_specs=pl.BlockSpec((B, TN_MOD), lambda j: (0, j)),
        out_shape=jax.ShapeDtypeStruct((B, n_out), F32),
        compiler_params=_cp("arbitrary"),
        name="mod",
    )(c.T, w_mod, b_mod.reshape(1, n_out))


def _inproj_kernel(x_ref, g_ref, sc_ref, sh_ref, w_ref, gv_ref, o_ref, h_ref, *, sub):
    s = pl.program_id(0)
    last = pl.num_programs(0) - 1
    tm = x_ref.shape[0]
    tn = gv_ref.shape[1]

    def normalise():
        h = (_rms(x_ref[...]) * g_ref[...]) * (1.0 + sc_ref[...]) + sh_ref[...]
        h_ref[pl.ds(pl.multiple_of((s % 2) * tm, tm), tm), :] = h.astype(BF16)

    def project():
        base = pl.multiple_of(((s + 1) % 2) * tm, tm)
        for j in range(w_ref.shape[1] // tn):
            cs = slice(j * tn, (j + 1) * tn)
            for k in range(tm // sub):
                acc = jnp.dot(h_ref[pl.ds(base + k * sub, sub), :], w_ref[:, cs], preferred_element_type=F32)
                if j == 0:
                    acc = _gelu_tanh(acc)
                elif j == 1:
                    v = _gelu_tanh(acc)
                    vc = v - jnp.mean(v, axis=-1, keepdims=True)
                    acc = vc * lax.rsqrt(jnp.mean(vc * vc, axis=-1, keepdims=True) + EPS) * gv_ref[...]
                o_ref[k * sub:(k + 1) * sub, cs] = acc.astype(BF16)

    @pl.when(s == 0)
    def _():
        normalise()

    @pl.when(jnp.logical_and(s > 0, s < last))
    def _():
        normalise()
        project()

    @pl.when(s == last)
    def _():
        project()


def _inproj(x2, mod3, g_pre, w_in_bf, g_v, seq):
    N, D = x2.shape
    d_in = w_in_bf.shape[1]
    tn = g_v.shape[1]
    tm = TM_INPROJ
    tpb = seq // tm
    nt = N // tm
    cur = lambda s: jnp.minimum(s, nt - 1)
    return pl.pallas_call(
        functools.partial(_inproj_kernel, sub=SUB_INPROJ),
        grid=(nt + 1,),
        in_specs=[pl.BlockSpec((tm, D), lambda s: (cur(s), 0)),
                  pl.BlockSpec((1, D), lambda s: (0, 0)),
                  pl.BlockSpec((None, 1, D), lambda s: (cur(s) // tpb, 0, 1)),
                  pl.BlockSpec((None, 1, D), lambda s: (cur(s) // tpb, 0, 0)),
                  pl.BlockSpec((D, d_in), lambda s: (0, 0), pipeline_mode=pl.Buffered(1)),
                  pl.BlockSpec((1, tn), lambda s: (0, 0))],
        out_specs=pl.BlockSpec((tm, d_in), lambda s: (jnp.maximum(s - 1, 0), 0)),
        out_shape=jax.ShapeDtypeStruct((N, d_in), BF16),
        scratch_shapes=[pltpu.VMEM((2 * tm, D), BF16)],
        compiler_params=_cp("arbitrary"),
        name="inproj",
    )(x2, g_pre, mod3, mod3, w_in_bf, g_v)


def _gmlp_kernel(u_ref, v_ref, w_ref, bt_ref, g_ref, o_ref, ya_ref, *, heads, chunk, hd):
    tm = u_ref.shape[0]
    row = lax.broadcasted_iota(jnp.int32, (chunk, chunk), 0)
    col = lax.broadcasted_iota(jnp.int32, (chunk, chunk), 1)
    causal = col <= row
    for h in range(heads):
        wm = jnp.where(causal, w_ref[h], 0.0).astype(BF16)
        bcol = bt_ref[:, h:h + 1]
        cs = slice(h * hd, (h + 1) * hd)
        for c in range(tm // chunk):
            rs = slice(c * chunk, (c + 1) * chunk)
            sv = jnp.dot(wm, v_ref[rs, cs], preferred_element_type=F32) + bcol
            ya_ref[rs, cs] = u_ref[rs, cs].astype(F32) * sv
    o_ref[...] = (_rms(ya_ref[...]) * g_ref[...]).astype(BF16)


def _gmlp(pa, w_spatial, b_spatial, g_out):
    N = pa.shape[0]
    heads, chunk, _ = w_spatial.shape
    dg = g_out.shape[1]
    kern = functools.partial(_gmlp_kernel, heads=heads, chunk=chunk, hd=dg // heads)
    return pl.pallas_call(
        kern,
        grid=(N // TM_GMLP,),
        in_specs=[pl.BlockSpec((TM_GMLP, dg), lambda i: (i, 0)),
                  pl.BlockSpec((TM_GMLP, dg), lambda i: (i, 1)),
                  pl.BlockSpec((heads, chunk, chunk), lambda i: (0, 0, 0)),
                  pl.BlockSpec((chunk, heads), lambda i: (0, 0)),
                  pl.BlockSpec((1, dg), lambda i: (0, 0))],
        out_specs=pl.BlockSpec((TM_GMLP, dg), lambda i: (i, 0)),
        out_shape=jax.ShapeDtypeStruct((N, dg), BF16),
        scratch_shapes=[pltpu.VMEM((TM_GMLP, dg), F32)],
        compiler_params=_cp("arbitrary"),
        name="gmlp",
    )(pa, pa, w_spatial, b_spatial.T, g_out)


def _perm_matrix(d, n):
    i = jnp.arange(n)
    src = (i % (n // d)) * d + i // (n // d)
    return (src[:, None] == jnp.arange(n)[None, :]).astype(BF16)


def _attn_kernel(q_ref, kc_ref, kp_ref, vc_ref, vp_ref, p4_ref, p16_ref, o_ref,
                 qd_ref, kd_ref, va_ref, op_ref, lp_ref, bias_ref, *, hd, dils):
    n = pl.program_id(2)
    blk = ATTN_BLOCK
    grp = ATTN_GROUP
    W = q_ref.shape[0]
    nh = q_ref.shape[1] // hd
    scale = hd ** -0.5
    c2 = scale * math.log2(math.e)
    nt = (((1,), (1,)), ((), ()))

    ri = lax.broadcasted_iota(jnp.int32, (blk, 2 * blk), 0)
    ci = lax.broadcasted_iota(jnp.int32, (blk, 2 * blk), 1)
    band = jnp.where(ci >= ri, jnp.where(ci <= ri + blk, 0.0, NEG), NEG)
    bias_ref[0] = band
    bias_ref[1] = jnp.where(n > 0, band, jnp.where(ci < blk, NEG, band))

    def put_v(rows, x):
        ones = jnp.ones((x.shape[0], hd), BF16)
        for hh in range(nh):
            parts = [x[:, hh * hd:(hh + 1) * hd] if c == hh else ones for c in range(nh)]
            va_ref[hh, rows, :] = jnp.concatenate(parts, axis=1)

    def head_blocks(items):
        cnt = len(items)
        ss, ms, ps = [None] * cnt, [None] * cnt, [None] * cnt

        def scores(i):
            lq, lk, _, _, lb, _ = items[i]
            ss[i] = lax.dot_general(lq(), lk(), nt, preferred_element_type=F32) + lb()

        def probs(i):
            ms[i] = jnp.max(ss[i], axis=-1, keepdims=True)
            ps[i] = jnp.exp2((ss[i] - ms[i]) * c2).astype(BF16)

        def output(i):
            _, _, lva, hh, _, store = items[i]
            r = jnp.dot(ps[i], lva(), preferred_element_type=F32)
            oh = (hh + 1) % nh
            l = r[:, oh * hd:(oh + 1) * hd]
            store(r[:, hh * hd:(hh + 1) * hd] / l, ms[i] * scale + jnp.log(l))

        for stage in (scores, probs, output):
            for i in range(cnt):
                stage(i)

    def run_pattern(pi, d):
        ql = W // d
        kl = ql + blk
        nb = ql // blk
        unroll = ATTN_UNROLL
        assert nb % unroll == 0 or unroll % nb == 0

        def aligned(x):
            return x if isinstance(x, int) else pl.multiple_of(x, blk)

        def loop_body(it, carry):
            items = []
            for u in range(unroll):
                if nb >= unroll:
                    bodies_per_class = nb // unroll
                    r = 0 if d == 1 else it // bodies_per_class
                    jb = u if bodies_per_class == 1 else (it % bodies_per_class) * unroll + u
                else:
                    r = it * (unroll // nb) + u // nb
                    jb = u % nb
                qrow = aligned(r * ql + jb * blk)
                krow = aligned(r * kl + jb * blk)
                first_blk = jb == 0
                orow = aligned(jb * blk) if d == 1 else jb * (blk * d) + r
                for hh in range(nh):
                    cs = slice(hh * hd, (hh + 1) * hd)

                    def store(o, lse, hh=hh, orow=orow):
                        if d == 1:
                            rows = pl.ds(orow, blk)
                        else:
                            rows = pl.ds(orow, blk, stride=d)
                        op_ref[pi, hh, rows, :] = o
                        lp_ref[pi, hh, rows, :] = lse

                    items.append((
                        lambda qrow=qrow, cs=cs: (q_ref if d == 1 else qd_ref)[pl.ds(qrow, blk), cs],
                        lambda krow=krow, cs=cs: kd_ref[pl.ds(krow, 2 * blk), cs],
                        lambda krow=krow, hh=hh: va_ref[hh, pl.ds(krow, 2 * blk), :],
                        hh,
                        lambda first_blk=first_blk: bias_ref[
                            int(first_blk) if isinstance(first_blk, bool) else jnp.where(first_blk, 1, 0)],
                        store))
            head_blocks(items)
            return carry

        lax.fori_loop(0, (d * nb) // unroll, loop_body, 0)

    def deinterleave(d, p_ref):
        pc = grp // d
        ql = W // d
        kl = ql + blk
        ng = W // grp
        pm = p_ref[...]

        def split(x):
            return jnp.dot(pm, x, preferred_element_type=F32).astype(BF16)

        for g in range(ng):
            rows = slice(g * grp, (g + 1) * grp)
            yq, yk, yv = split(q_ref[rows, :]), split(kc_ref[rows, :]), split(vc_ref[rows, :])
            for r in range(d):
                piece = slice(r * pc, (r + 1) * pc)
                qd_ref[r * ql + g * pc:r * ql + (g + 1) * pc, :] = yq[piece]
                kd_ref[r * kl + blk + g * pc:r * kl + blk + (g + 1) * pc, :] = yk[piece]
                put_v(slice(r * kl + blk + g * pc, r * kl + blk + (g + 1) * pc), yv[piece])
        g0 = ng - (blk * d) // grp
        for g in range(g0, ng):
            rows = slice(g * grp, (g + 1) * grp)
            yk, yv = split(kp_ref[rows, :]), split(vp_ref[rows, :])
            for r in range(d):
                piece = slice(r * pc, (r + 1) * pc)
                dst = r * kl + (g - g0) * pc
                kd_ref[dst:dst + pc, :] = yk[piece]
                put_v(slice(dst, dst + pc), yv[piece])

    for pi, d in enumerate(dils):
        if d == 1:
            kd_ref[0:blk, :] = kp_ref[W - blk:W, :]
            kd_ref[blk:blk + W, :] = kc_ref[...]
            put_v(slice(0, blk), vp_ref[W - blk:W, :])
            put_v(slice(blk, blk + W), vc_ref[...])
        else:
            deinterleave(d, p4_ref if d == 4 else p16_ref)
        run_pattern(pi, d)

    fin = 2 * blk

    def fin_body(c, carry):
        rows = pl.ds(pl.multiple_of(c * fin, fin), fin)
        for hh in range(nh):
            ls = [lp_ref[pi, hh, rows, :] for pi in range(len(dils))]
            mx = functools.reduce(jnp.maximum, ls)
            ws = [jnp.exp(l - mx) for l in ls]
            num = sum(w * op_ref[pi, hh, rows, :] for pi, w in enumerate(ws))
            o_ref[rows, hh * hd:(hh + 1) * hd] = num / sum(ws)
        return carry

    lax.fori_loop(0, W // fin, fin_body, 0)


def _attention(pa3, heads, d_attn, col0):
    B, S, _ = pa3.shape
    hd = d_attn // heads
    dils = tuple(d for _, d in DILATION_PATTERNS)
    assert all(w == ATTN_BLOCK * d for w, d in DILATION_PATTERNS) and dils == (1, 4, 16)
    W = ATTN_WINDOW
    lanes = ATTN_HEADS_PER_STEP * hd
    cb = d_attn // lanes
    assert S % W == 0 and W % ATTN_GROUP == 0 and hd == LANES

    def cur(c):
        return pl.BlockSpec((None, W, lanes), lambda b, hp, n: (b, n, (col0 + c) * cb + hp))

    def prv(c):
        return pl.BlockSpec((None, W, lanes), lambda b, hp, n: (b, jnp.maximum(n - 1, 0), (col0 + c) * cb + hp))

    perm = pl.BlockSpec((ATTN_GROUP, ATTN_GROUP), lambda b, hp, n: (0, 0))
    kern = functools.partial(_attn_kernel, hd=hd, dils=dils)
    npat = len(dils)
    return pl.pallas_call(
        kern,
        grid=(B, cb, S // W),
        in_specs=[cur(0), cur(1), prv(1), cur(2), prv(2), perm, perm],
        out_specs=pl.BlockSpec((None, W, lanes), lambda b, hp, n: (b, n, hp)),
        out_shape=jax.ShapeDtypeStruct((B, S, d_attn), F32),
        scratch_shapes=[pltpu.VMEM((W, lanes), BF16),
                        pltpu.VMEM((2 * W, lanes), BF16),
                        pltpu.VMEM((ATTN_HEADS_PER_STEP, 2 * W, lanes), BF16),
                        pltpu.VMEM((npat, ATTN_HEADS_PER_STEP, W, hd), F32),
                        pltpu.VMEM((npat, ATTN_HEADS_PER_STEP, W, hd), F32),
                        pltpu.VMEM((2, ATTN_BLOCK, 2 * ATTN_BLOCK), F32)],
        compiler_params=_cp("arbitrary", "arbitrary", "arbitrary"),
        name="attn",
    )(pa3, pa3, pa3, pa3, pa3, _perm_matrix(4, ATTN_GROUP), _perm_matrix(16, ATTN_GROUP))


def _outproj_kernel(ya_ref, ob_ref, x_ref, w_ref, gattn_ref, gpost_ref, gate1_ref, gpre_ref,
                    sc2_ref, sh2_ref, wr_ref, br_ref, x1_ref, xl_ref, route_ref, cat_ref, h2_ref,
                    *, n_groups, epg, sub):
    dg = ya_ref.shape[1]
    half = x_ref.shape[1] // 2
    subs = [slice(k * sub, (k + 1) * sub) for k in range(ya_ref.shape[0] // sub)]
    for rs in subs:
        cat_ref[rs, :dg] = ya_ref[rs, :]
        cat_ref[rs, dg:] = (_rms(ob_ref[rs, :]) * gattn_ref[...]).astype(BF16)
    ys = [jnp.dot(cat_ref[rs, :], w_ref[...], preferred_element_type=F32) for rs in subs]
    for rs, y in zip(subs, ys):
        x1 = x_ref[rs, :] + gate1_ref[...] * (_rms(y) * gpost_ref[...])
        x1_ref[rs, :] = x1
        h2_ref[rs, :] = (_rms(x1) * gpre_ref[...]) * (1.0 + sc2_ref[...]) + sh2_ref[...]

    @pl.when(pl.program_id(0) >= 0)
    def _():
        for k, rs in enumerate(subs):
            _route_rows(h2_ref[rs, :], wr_ref, br_ref, xl_ref, route_ref, rs, k * sub, half, n_groups, epg)


def _route_rows(h2, wr_ref, br_ref, xl_ref, route_ref, rs, row0, half, n_groups, epg):
    tm = h2.shape[0]
    hb = h2.astype(BF16)
    hb32 = hb.astype(F32)
    words = (lax.shift_right_logical(pltpu.bitcast(hb32[:, :half], jnp.uint32), jnp.uint32(16))
             | (pltpu.bitcast(hb32[:, half:], jnp.uint32) & jnp.uint32(0xFFFF0000)))
    _store_slabs(xl_ref, words, row0)

    lo = (h2 - hb32).astype(BF16)
    r = (jnp.dot(hb, wr_ref[...], preferred_element_type=F32)
         + jnp.dot(lo, wr_ref[...], preferred_element_type=F32))
    logits = r[:, :LANES] + r[:, LANES:] + br_ref[...]

    lane = lax.broadcasted_iota(jnp.int32, (tm, LANES), 1)
    lane_f = lane.astype(F32)
    big = float(LANES)
    lg = jnp.where(lane < n_groups, logits, NEG)
    mg = jnp.max(lg, axis=-1, keepdims=True)
    gi = jnp.min(jnp.where(lg == mg, lane_f, big), axis=-1, keepdims=True)
    gp = 1.0 / jnp.sum(jnp.exp(lg - mg), axis=-1, keepdims=True)
    e_lo = n_groups + gi * epg
    le = jnp.where(lane_f >= e_lo, jnp.where(lane_f < e_lo + epg, logits, NEG), NEG)
    m1 = jnp.max(le, axis=-1, keepdims=True)
    i1 = jnp.min(jnp.where(le == m1, lane_f, big), axis=-1, keepdims=True)
    le2 = jnp.where(lane_f == i1, NEG, le)
    m2 = jnp.max(le2, axis=-1, keepdims=True)
    i2 = jnp.min(jnp.where(le2 == m2, lane_f, big), axis=-1, keepdims=True)
    t = jnp.exp(m2 - m1)
    w1 = gp / (1.0 + t)
    w2 = gp * t / (1.0 + t)
    route_ref[rs, :] = jnp.where(lane == 0, i1 - n_groups,
                                 jnp.where(lane == 1, i2 - n_groups,
                                           jnp.where(lane == 2, w1, jnp.where(lane == 3, w2, 0.0))))


def _outproj(ya_n, ob, x2, w_out_bf, g_attn, g_post, mod3, g_pre, wr, br, seq, n_groups, epg):
    N, D = x2.shape
    dg = ya_n.shape[1]
    tm = TM_OUTPROJ
    tpb = seq // tm
    assert D == 2 * ROW_SLAB * LANES
    row = lambda i: (i, 0)
    const = lambda i: (0, 0)
    modc = lambda k: pl.BlockSpec((None, 1, D), lambda i: (i // tpb, 0, k))
    resident = dict(pipeline_mode=pl.Buffered(1))
    kern = functools.partial(_outproj_kernel, n_groups=n_groups, epg=epg, sub=SUB_OUTPROJ)
    return pl.pallas_call(
        kern,
        grid=(N // tm,),
        in_specs=[pl.BlockSpec((tm, dg), row), pl.BlockSpec((tm, dg), row), pl.BlockSpec((tm, D), row),
                  pl.BlockSpec((D, D), const, **resident), pl.BlockSpec((1, dg), const), pl.BlockSpec((1, D), const),
                  modc(2), pl.BlockSpec((1, D), const), modc(4), modc(3),
                  pl.BlockSpec((D, 2 * LANES), const, **resident), pl.BlockSpec((1, LANES), const)],
        out_specs=[pl.BlockSpec((tm, D), row), pl.BlockSpec((tm * ROW_SLAB, LANES), row),
                   pl.BlockSpec((tm, LANES), row)],
        out_shape=[jax.ShapeDtypeStruct((N, D), F32), jax.ShapeDtypeStruct((N * ROW_SLAB, LANES), jnp.uint32),
                   jax.ShapeDtypeStruct((N, LANES), F32)],
        scratch_shapes=[pltpu.VMEM((tm, D), BF16), pltpu.VMEM((tm, D), F32)],
        compiler_params=_cp("arbitrary"),
        name="outproj",
    )(ya_n, ob, x2, w_out_bf, g_attn, g_post, mod3, g_pre, mod3, mod3, wr, br)


def _rank_kernel(route_ref, rank_ref, cnt_ref, carry_ref):
    tr = route_ref.shape[0]

    @pl.when(pl.program_id(0) == 0)
    def _():
        carry_ref[...] = jnp.zeros_like(carry_ref)

    lane = lax.broadcasted_iota(jnp.int32, (tr, LANES), 1)
    lane_f = lane.astype(F32)
    oh0 = lane_f == route_ref[:, 0:1]
    oh1 = lane_f == route_ref[:, 1:2]
    oh = jnp.where(oh0, 1.0, jnp.where(oh1, 1.0, 0.0))
    ri = lax.broadcasted_iota(jnp.int32, (tr, tr), 0)
    ci = lax.broadcasted_iota(jnp.int32, (tr, tr), 1)
    earlier = jnp.where(ci < ri, 1.0, 0.0).astype(BF16)
    before = jnp.dot(earlier, oh.astype(BF16), preferred_element_type=F32) + carry_ref[...]
    r0 = jnp.sum(jnp.where(oh0, before, 0.0), axis=-1, keepdims=True)
    r1 = jnp.sum(jnp.where(oh1, before, 0.0), axis=-1, keepdims=True)
    rank_ref[...] = jnp.where(lane == 0, r0, jnp.where(lane == 1, r1, 0.0))
    carry_ref[...] = carry_ref[...] + jnp.sum(oh, axis=0, keepdims=True)
    cnt_ref[...] = carry_ref[...]


def _rank(route):
    N = route.shape[0]
    tr = TM_RANK
    return pl.pallas_call(
        _rank_kernel,
        grid=(N // tr,),
        in_specs=[pl.BlockSpec((tr, LANES), lambda i: (i, 0))],
        out_specs=[pl.BlockSpec((tr, LANES), lambda i: (i, 0)), pl.BlockSpec((1, LANES), lambda i: (0, 0))],
        out_shape=[jax.ShapeDtypeStruct((N, LANES), F32), jax.ShapeDtypeStruct((1, LANES), F32)],
        scratch_shapes=[pltpu.VMEM((1, LANES), F32)],
        compiler_params=_cp("arbitrary"),
        name="rank",
    )(route)


def _slab(i):
    return pl.ds(pl.multiple_of(i * ROW_SLAB, ROW_SLAB), ROW_SLAB)


def _dispatch_kernel(dest_ref, xl_ref, xs_hbm, sem):
    tm = xl_ref.shape[0] // ROW_SLAB

    def copy(t, d):
        return pltpu.make_async_copy(xl_ref.at[_slab(t)], xs_hbm.at[_slab(d)], sem)

    def start(t, c):
        for s in range(TOP_K):
            copy(t, dest_ref[0, TOP_K * t + s]).start(priority=s % 2)
        return c

    def wait(t, c):
        for s in range(TOP_K):
            copy(t, dest_ref[0, TOP_K * t + s]).wait()
        return c

    lax.fori_loop(0, tm, start, 0, unroll=8)
    lax.fori_loop(0, tm, wait, 0, unroll=8)


def _dispatch(xl, dest, n_rows):
    N = xl.shape[0] // ROW_SLAB
    tm = TM_DISPATCH
    return pl.pallas_call(
        _dispatch_kernel,
        grid=(N // tm,),
        in_specs=[pl.BlockSpec((None, 1, TOP_K * tm), lambda i: (i, 0, 0), memory_space=pltpu.SMEM),
                  pl.BlockSpec((tm * ROW_SLAB, LANES), lambda i: (i, 0))],
        out_specs=pl.BlockSpec(memory_space=pl.ANY),
        out_shape=jax.ShapeDtypeStruct((n_rows * ROW_SLAB, LANES), jnp.uint32),
        scratch_shapes=[pltpu.SemaphoreType.DMA(())],
        compiler_params=_cp("arbitrary"),
        name="dispatch",
    )(dest.reshape(N // tm, 1, TOP_K * tm), xl)


def _ffn_kernel(te_ref, tv_ref, ti_ref, tf_ref, tn_ref, xs_ref, wg_hbm, wu_hbm, wd_hbm, ys_ref,
                x_ref, stage_g, stage_u, stage_d, wg_ref, wu_ref, wd_ref, sems):
    del ti_ref
    t = pl.program_id(0)
    nv = tv_ref[t]
    tm = x_ref.shape[0]
    half = x_ref.shape[1] // 2
    stages = ((wg_hbm, stage_g, wg_ref), (wu_hbm, stage_u, wu_ref), (wd_hbm, stage_d, wd_ref))

    def fetch(e):
        return [pltpu.make_async_copy(src.at[e], stage, sems.at[k]) for k, (src, stage, _) in enumerate(stages)]

    @pl.when(t == 0)
    def _():
        for c in fetch(te_ref[0]):
            c.start()

    @pl.when(tf_ref[t] == 1)
    def _():
        for c in fetch(0):
            c.wait()
        for _, stage, dst in stages:
            rows = stage.shape[0] // FFN_CAST_CHUNKS

            def cast(i, carry, stage=stage, dst=dst, rows=rows):
                rs = pl.ds(pl.multiple_of(i * rows, rows), rows)
                dst[rs, :] = stage[rs, :].astype(BF16)
                return carry

            lax.fori_loop(0, FFN_CAST_CHUNKS, cast, 0)

        @pl.when(tn_ref[t] >= 0)
        def _():
            for c in fetch(tn_ref[t]):
                c.start()

    @pl.when(nv > 0)
    def _():
        keep = lax.broadcasted_iota(jnp.int32, (tm, 1), 0) < nv
        for j in range(ROW_SLAB):
            w = jnp.where(keep, xs_ref[pl.ds(j, tm, stride=ROW_SLAB), :], jnp.uint32(0))
            x_ref[:, j * LANES:(j + 1) * LANES] = _unpack_lo(w).astype(BF16)
            x_ref[:, half + j * LANES:half + (j + 1) * LANES] = _unpack_hi(w).astype(BF16)
        x = x_ref[...]
        g = jnp.dot(x, wg_ref[...], preferred_element_type=F32)
        u = jnp.dot(x, wu_ref[...], preferred_element_type=F32)
        hm = ((g * _sigmoid(g)) * u).astype(BF16)
        _store_slabs(ys_ref, _pack_rows(jnp.dot(hm, wd_ref[...], preferred_element_type=F32)))

    @pl.when(nv == 0)
    def _():
        ys_ref[...] = jnp.zeros_like(ys_ref)


def _ffn(xs, tile_e, tile_valid, tile_in, tile_first, tile_next, wg, wu, wd):
    D, de = wg.shape[1], wg.shape[2]
    tm = TM_FFN
    nt = tile_e.shape[0]
    hbm = pl.BlockSpec(memory_space=pl.ANY)
    grid_spec = pltpu.PrefetchScalarGridSpec(
        num_scalar_prefetch=5,
        grid=(nt,),
        in_specs=[pl.BlockSpec((tm * ROW_SLAB, LANES), lambda t, te, tv, ti, tf, tn: (ti[t], 0)), hbm, hbm, hbm],
        out_specs=pl.BlockSpec((tm * ROW_SLAB, LANES), lambda t, te, tv, ti, tf, tn: (t, 0)),
        scratch_shapes=[pltpu.VMEM((tm, D), BF16),
                        pltpu.VMEM((D, de), F32), pltpu.VMEM((D, de), F32), pltpu.VMEM((de, D), F32),
                        pltpu.VMEM((D, de), BF16), pltpu.VMEM((D, de), BF16), pltpu.VMEM((de, D), BF16),
                        pltpu.SemaphoreType.DMA((3,))],
    )
    return pl.pallas_call(
        _ffn_kernel,
        grid_spec=grid_spec,
        out_shape=jax.ShapeDtypeStruct(xs.shape, jnp.uint32),
        compiler_params=_cp("arbitrary"),
        name="ffn",
    )(tile_e, tile_valid, tile_in, tile_first, tile_next, xs, wg, wu, wd)


def _combine_kernel(dcur_ref, dnext_ref, route_ref, x1_ref, gate2_ref, g_ref, ys_hbm, o_ref, buf, y_ref, sems):
    i = pl.program_id(0)
    nsteps = pl.num_programs(0)
    tm = x1_ref.shape[0]
    half = x1_ref.shape[1] // 2

    def copy(slot, t, s, d):
        return pltpu.make_async_copy(ys_hbm.at[_slab(d)], buf.at[slot, s, _slab(t)], sems.at[slot])

    def start_tile(slot, dref):
        def body(t, c):
            for s in range(TOP_K):
                copy(slot, t, s, dref[0, TOP_K * t + s]).start(priority=s % 2)
            return c
        lax.fori_loop(0, tm, body, 0, unroll=8)

    def wait_tile(slot, dref):
        def body(t, c):
            for s in range(TOP_K):
                copy(slot, t, s, dref[0, TOP_K * t + s]).wait()
            return c
        lax.fori_loop(0, tm, body, 0, unroll=8)

    @pl.when(i == 0)
    def _():
        start_tile(0, dcur_ref)

    for slot in range(2):
        @pl.when(jnp.logical_and(i + 1 < nsteps, (i + 1) % 2 == slot))
        def _():
            start_tile(slot, dnext_ref)

    for slot in range(2):
        @pl.when(i % 2 == slot)
        def _():
            wait_tile(slot, dcur_ref)
            w0 = route_ref[:, 2:3]
            w1 = route_ref[:, 3:4]
            for j in range(ROW_SLAB):
                a = buf[slot, 0, pl.ds(j, tm, stride=ROW_SLAB), :]
                b = buf[slot, 1, pl.ds(j, tm, stride=ROW_SLAB), :]
                y_ref[:, j * LANES:(j + 1) * LANES] = _unpack_lo(a) * w0 + _unpack_lo(b) * w1
                y_ref[:, half + j * LANES:half + (j + 1) * LANES] = _unpack_hi(a) * w0 + _unpack_hi(b) * w1

    o_ref[...] = x1_ref[...] + gate2_ref[...] * (_rms(y_ref[...]) * g_ref[...])


def _combine(ys, dest, route, x1, mod3, g_post, seq):
    N, D = x1.shape
    tm = TM_COMBINE
    tpb = seq // tm
    nsteps = N // tm
    dest3 = dest.reshape(nsteps, 1, TOP_K * tm)
    return pl.pallas_call(
        _combine_kernel,
        grid=(nsteps,),
        in_specs=[pl.BlockSpec((None, 1, TOP_K * tm), lambda i: (i, 0, 0), memory_space=pltpu.SMEM),
                  pl.BlockSpec((None, 1, TOP_K * tm), lambda i: (jnp.minimum(i + 1, nsteps - 1), 0, 0),
                               memory_space=pltpu.SMEM),
                  pl.BlockSpec((tm, LANES), lambda i: (i, 0)),
                  pl.BlockSpec((tm, D), lambda i: (i, 0)),
                  pl.BlockSpec((None, 1, D), lambda i: (i // tpb, 0, 5)),
                  pl.BlockSpec((1, D), lambda i: (0, 0)),
                  pl.BlockSpec(memory_space=pl.ANY)],
        out_specs=pl.BlockSpec((tm, D), lambda i: (i, 0)),
        out_shape=jax.ShapeDtypeStruct((N, D), F32),
        scratch_shapes=[pltpu.VMEM((2, TOP_K, tm * ROW_SLAB, LANES), jnp.uint32), pltpu.VMEM((tm, D), F32),
                        pltpu.SemaphoreType.DMA((2,))],
        compiler_params=_cp("arbitrary"),
        name="combine",
    )(dest3, dest3, route, x1, mod3, g_post, ys)


def _routing_tables(route, rank, cnt, n_experts, tm):
    N = route.shape[0]
    A = N * TOP_K
    counts = cnt[0, :n_experts].astype(jnp.int32)
    padded = ((counts + tm - 1) // tm) * tm
    pend = jnp.cumsum(padded)
    pstart = pend - padded
    e = route[:, :TOP_K].astype(jnp.int32)
    start = jnp.sum(jnp.where(e[..., None] == jnp.arange(n_experts, dtype=jnp.int32), pstart, 0), axis=-1)
    dest = start + rank[:, :TOP_K].astype(jnp.int32)
    nt = A // tm + n_experts
    n_used = pend[-1] // tm
    tidx = jnp.arange(nt, dtype=jnp.int32)
    tstart = tidx * tm
    te = jnp.minimum(jnp.sum(pend[None, :] <= tstart[:, None], axis=1), n_experts - 1).astype(jnp.int32)
    tv = jnp.clip(counts[te] - (tstart - pstart[te]), 0, tm).astype(jnp.int32)
    used = tidx < n_used
    last = jnp.maximum(n_used - 1, 0)
    te = jnp.where(used, te, te[last]).astype(jnp.int32)
    ti = jnp.where(used, tidx, last).astype(jnp.int32)
    first = used & ((tidx == 0) | (te != jnp.roll(te, 1)))
    nxt = lax.cummin(jnp.where(first, tidx, nt)[::-1])[::-1]
    nxt = jnp.concatenate([nxt[1:], jnp.full((1,), nt, jnp.int32)])
    tn = jnp.where(first & (nxt < nt), te[jnp.minimum(nxt, nt - 1)], -1).astype(jnp.int32)
    return dest.astype(jnp.int32), te, tv, ti, first.astype(jnp.int32), tn, nt


def kernel(x, c, w_mod, b_mod, g_pre_mix, g_post_mix, w_in, g_gmlp_v, w_spatial, b_spatial, g_out_gmlp,
           g_out_attn, w_out, g_pre_ffn, g_post_ffn, w_router_group, b_router_group, w_router_expert,
           b_router_expert, w_gate, w_up, w_down):
    B, S, D = x.shape
    N = B * S
    depth = w_mod.shape[0]
    d_gmlp = g_gmlp_v.shape[1]
    d_attn = g_out_attn.shape[1]
    heads = w_spatial.shape[1]
    n_groups, epg = b_router_expert.shape[1], b_router_expert.shape[2]
    n_experts = n_groups * epg
    assert d_gmlp == d_attn and w_in.shape[2] == 2 * d_gmlp + 3 * d_attn
    assert n_groups + n_experts <= LANES

    x2 = x.reshape(N, D)
    for l in range(depth):
        mod3 = _modulation(c, w_mod[l], b_mod[l]).reshape(B, 1, w_mod.shape[2])

        pa = _inproj(x2, mod3, g_pre_mix[l][None], w_in[l].astype(BF16), g_gmlp_v[l][None], S)
        ya_n = _gmlp(pa, w_spatial[l], b_spatial[l], g_out_gmlp[l][None])
        ob = _attention(pa.reshape(B, S, pa.shape[1]), heads, d_attn, (2 * d_gmlp) // d_attn).reshape(N, d_attn)

        wr32 = jnp.concatenate([w_router_group[l],
                                jnp.transpose(w_router_expert[l], (1, 0, 2)).reshape(D, n_experts)], axis=1)
        wr32 = jnp.pad(wr32, ((0, 0), (0, LANES - wr32.shape[1])))
        wr_hi = wr32.astype(BF16)
        wr = jnp.concatenate([wr_hi, (wr32 - wr_hi.astype(F32)).astype(BF16)], axis=1)
        br = jnp.pad(jnp.concatenate([b_router_group[l], b_router_expert[l].reshape(n_experts)]),
                     (0, LANES - n_groups - n_experts))[None]

        x1, xl, route = _outproj(ya_n, ob, x2, w_out[l].astype(BF16), g_out_attn[l][None], g_post_mix[l][None],
                                 mod3, g_pre_ffn[l][None], wr, br, S, n_groups, epg)

        rank, cnt = _rank(route)
        dest, te, tv, ti, tf, tn, nt = _routing_tables(route, rank, cnt, n_experts, TM_FFN)
        xs = _dispatch(xl, dest, nt * TM_FFN)
        ys = _ffn(xs, te, tv, ti, tf, tn, w_gate[l], w_up[l], w_down[l])
        x2 = _combine(ys, dest, route, x1, mod3, g_post_ffn[l][None], S)
    return x2.reshape(B, S, D)
```

```python
import functools
import math

import jax
import jax.numpy as jnp
from jax import lax
from jax.experimental import pallas as pl
from jax.experimental.pallas import tpu as pltpu

F32 = jnp.float32
BF16 = jnp.bfloat16
EPS = 1e-6
NEG = -1e30

DILATION_PATTERNS = ((128, 1), (512, 4), (2048, 16))
ATTN_BLOCK = 128
TOP_K = 2
LANES = 128

TM_INPROJ = 512
SUB_INPROJ = 256
TM_GMLP = 512
ATTN_WINDOW = ATTN_BLOCK * max(d for _, d in DILATION_PATTERNS)
ATTN_GROUP = 256
ATTN_HEADS_PER_STEP = 2
ATTN_UNROLL = 8
TM_OUTPROJ = 512
SUB_OUTPROJ = 256
TM_RANK = 512
TM_DISPATCH = TM_RANK
TM_FFN = 256
FFN_CAST_CHUNKS = 8
TM_COMBINE = TM_RANK
TN_MOD = 1024
ROW_SLAB = 8
VMEM_LIMIT = 56 * 1024 * 1024


def _cp(*dims):
    return pltpu.CompilerParams(dimension_semantics=dims, vmem_limit_bytes=VMEM_LIMIT)


def _rms(x):
    return x * lax.rsqrt(jnp.mean(x * x, axis=-1, keepdims=True) + EPS)


def _gelu_tanh(x):
    c = math.sqrt(2.0 / math.pi)
    return x * (0.5 * (1.0 + jnp.tanh(c * (x + 0.044715 * (x * x * x)))))


def _sigmoid(x):
    return 1.0 / (1.0 + jnp.exp(-x))


def _pack_rows(x):
    half = x.shape[1] // 2
    lo = pltpu.bitcast(x[:, :half].astype(BF16).astype(F32), jnp.uint32)
    hi = pltpu.bitcast(x[:, half:].astype(BF16).astype(F32), jnp.uint32)
    return lax.shift_right_logical(lo, jnp.uint32(16)) | (hi & jnp.uint32(0xFFFF0000))


def _unpack_lo(w):
    return pltpu.bitcast(lax.shift_left(w, jnp.uint32(16)), F32)


def _unpack_hi(w):
    return pltpu.bitcast(w & jnp.uint32(0xFFFF0000), F32)


def _store_slabs(ref, words, row0=0):
    rows = words.shape[0]
    for j in range(ROW_SLAB):
        ref[pl.ds(row0 * ROW_SLAB + j, rows, stride=ROW_SLAB), :] = words[:, j * LANES:(j + 1) * LANES]


def _mod_kernel(ct_ref, w_ref, b_ref, o_ref):
    ct = ct_ref[...]
    at = ct * _sigmoid(ct)
    w = w_ref[...]
    for b in range(o_ref.shape[0]):
        o_ref[b:b + 1, :] = jnp.sum(w * at[:, b:b + 1], axis=0, keepdims=True) + b_ref[...]


def _modulation(c, w_mod, b_mod):
    B, D = c.shape
    n_out = w_mod.shape[1]
    return pl.pallas_call(
        _mod_kernel,
        grid=(n_out // TN_MOD,),
        in_specs=[pl.BlockSpec((D, B), lambda j: (0, 0)),
                  pl.BlockSpec((D, TN_MOD), lambda j: (0, j)),
                  pl.BlockSpec((1, TN_MOD), lambda j: (0, j))],
        out_specs=pl.BlockSpec((B, TN_MOD), lambda j: (0, j)),
        out_shape=jax.ShapeDtypeStruct((B, n_out), F32),
        compiler_params=_cp("arbitrary"),
        name="mod",
    )(c.T, w_mod, b_mod.reshape(1, n_out))


def _inproj_kernel(x_ref, g_ref, sc_ref, sh_ref, w_ref, gv_ref, o_ref, h_ref, *, sub):
    s = pl.program_id(0)
    last = pl.num_programs(0) - 1
    tm = x_ref.shape[0]
    tn = gv_ref.shape[1]

    def normalise():
        h = (_rms(x_ref[...]) * g_ref[...]) * (1.0 + sc_ref[...]) + sh_ref[...]
        h_ref[pl.ds(pl.multiple_of((s % 2) * tm, tm), tm), :] = h.astype(BF16)

    def project():
        base = pl.multiple_of(((s + 1) % 2) * tm, tm)
        for j in range(w_ref.shape[1] // tn):
            cs = slice(j * tn, (j + 1) * tn)
            for k in range(tm // sub):
                acc = jnp.dot(h_ref[pl.ds(base + k * sub, sub), :], w_ref[:, cs], preferred_element_type=F32)
                if j == 0:
                    acc = _gelu_tanh(acc)
                elif j == 1:
                    v = _gelu_tanh(acc)
                    vc = v - jnp.mean(v, axis=-1, keepdims=True)
                    acc = vc * lax.rsqrt(jnp.mean(vc * vc, axis=-1, keepdims=True) + EPS) * gv_ref[...]
                o_ref[k * sub:(k + 1) * sub, cs] = acc.astype(BF16)

    @pl.when(s == 0)
    def _():
        normalise()

    @pl.when(jnp.logical_and(s > 0, s < last))
    def _():
        normalise()
        project()

    @pl.when(s == last)
    def _():
        project()


def _inproj(x2, mod3, g_pre, w_in_bf, g_v, seq):
    N, D = x2.shape
    d_in = w_in_bf.shape[1]
    tn = g_v.shape[1]
    tm = TM_INPROJ
    tpb = seq // tm
    nt = N // tm
    cur = lambda s: jnp.minimum(s, nt - 1)
    return pl.pallas_call(
        functools.partial(_inproj_kernel, sub=SUB_INPROJ),
        grid=(nt + 1,),
        in_specs=[pl.BlockSpec((tm, D), lambda s: (cur(s), 0)),
                  pl.BlockSpec((1, D), lambda s: (0, 0)),
                  pl.BlockSpec((None, 1, D), lambda s: (cur(s) // tpb, 0, 1)),
                  pl.BlockSpec((None, 1, D), lambda s: (cur(s) // tpb, 0, 0)),
                  pl.BlockSpec((D, d_in), lambda s: (0, 0), pipeline_mode=pl.Buffered(1)),
                  pl.BlockSpec((1, tn), lambda s: (0, 0))],
        out_specs=pl.BlockSpec((tm, d_in), lambda s: (jnp.maximum(s - 1, 0), 0)),
        out_shape=jax.ShapeDtypeStruct((N, d_in), BF16),
        scratch_shapes=[pltpu.VMEM((2 * tm, D), BF16)],
        compiler_params=_cp("arbitrary"),
        name="inproj",
    )(x2, g_pre, mod3, mod3, w_in_bf, g_v)


def _gmlp_kernel(u_ref, v_ref, w_ref, bt_ref, g_ref, o_ref, ya_ref, *, heads, chunk, hd):
    tm = u_ref.shape[0]
    row = lax.broadcasted_iota(jnp.int32, (chunk, chunk), 0)
    col = lax.broadcasted_iota(jnp.int32, (chunk, chunk), 1)
    causal = col <= row
    for h in range(heads):
        wm = jnp.where(causal, w_ref[h], 0.0).astype(BF16)
        bcol = bt_ref[:, h:h + 1]
        cs = slice(h * hd, (h + 1) * hd)
        for c in range(tm // chunk):
            rs = slice(c * chunk, (c + 1) * chunk)
            sv = jnp.dot(wm, v_ref[rs, cs], preferred_element_type=F32) + bcol
            ya_ref[rs, cs] = u_ref[rs, cs].astype(F32) * sv
    o_ref[...] = (_rms(ya_ref[...]) * g_ref[...]).astype(BF16)


def _gmlp(pa, w_spatial, b_spatial, g_out):
    N = pa.shape[0]
    heads, chunk, _ = w_spatial.shape
    dg = g_out.shape[1]
    kern = functools.partial(_gmlp_kernel, heads=heads, chunk=chunk, hd=dg // heads)
    return pl.pallas_call(
        kern,
        grid=(N // TM_GMLP,),
        in_specs=[pl.BlockSpec((TM_GMLP, dg), lambda i: (i, 0)),
                  pl.BlockSpec((TM_GMLP, dg), lambda i: (i, 1)),
                  pl.BlockSpec((heads, chunk, chunk), lambda i: (0, 0, 0)),
                  pl.BlockSpec((chunk, heads), lambda i: (0, 0)),
                  pl.BlockSpec((1, dg), lambda i: (0, 0))],
        out_specs=pl.BlockSpec((TM_GMLP, dg), lambda i: (i, 0)),
        out_shape=jax.ShapeDtypeStruct((N, dg), BF16),
        scratch_shapes=[pltpu.VMEM((TM_GMLP, dg), F32)],
        compiler_params=_cp("arbitrary"),
        name="gmlp",
    )(pa, pa, w_spatial, b_spatial.T, g_out)


def _perm_matrix(d, n):
    i = jnp.arange(n)
    src = (i % (n // d)) * d + i // (n // d)
    return (src[:, None] == jnp.arange(n)[None, :]).astype(BF16)


def _attn_kernel(q_ref, kc_ref, kp_ref, vc_ref, vp_ref, p4_ref, p16_ref, o_ref,
                 qd_ref, kd_ref, va_ref, op_ref, lp_ref, bias_ref, *, hd, dils):
    n = pl.program_id(2)
    blk = ATTN_BLOCK
    grp = ATTN_GROUP
    W = q_ref.shape[0]
    nh = q_ref.shape[1] // hd
    scale = hd ** -0.5
    c2 = scale * math.log2(math.e)
    nt = (((1,), (1,)), ((), ()))

    ri = lax.broadcasted_iota(jnp.int32, (blk, 2 * blk), 0)
    ci = lax.broadcasted_iota(jnp.int32, (blk, 2 * blk), 1)
    band = jnp.where(ci >= ri, jnp.where(ci <= ri + blk, 0.0, NEG), NEG)
    bias_ref[0] = band
    bias_ref[1] = jnp.where(n > 0, band, jnp.where(ci < blk, NEG, band))

    def put_v(rows, x):
        ones = jnp.ones((x.shape[0], hd), BF16)
        for hh in range(nh):
            parts = [x[:, hh * hd:(hh + 1) * hd] if c == hh else ones for c in range(nh)]
            va_ref[hh, rows, :] = jnp.concatenate(parts, axis=1)

    def head_blocks(items):
        cnt = len(items)
        ss, ms, ps = [None] * cnt, [None] * cnt, [None] * cnt

        def scores(i):
            lq, lk, _, _, lb, _ = items[i]
            ss[i] = lax.dot_general(lq(), lk(), nt, preferred_element_type=F32) + lb()

        def probs(i):
            ms[i] = jnp.max(ss[i], axis=-1, keepdims=True)
            ps[i] = jnp.exp2((ss[i] - ms[i]) * c2).astype(BF16)

        def output(i):
            _, _, lva, hh, _, store = items[i]
            r = jnp.dot(ps[i], lva(), preferred_element_type=F32)
            oh = (hh + 1) % nh
            l = r[:, oh * hd:(oh + 1) * hd]
            store(r[:, hh * hd:(hh + 1) * hd] / l, ms[i] * scale + jnp.log(l))

        for stage in (scores, probs, output):
            for i in range(cnt):
                stage(i)

    def run_pattern(pi, d):
        ql = W // d
        kl = ql + blk
        nb = ql // blk
        unroll = ATTN_UNROLL
        assert nb % unroll == 0 or unroll % nb == 0

        def aligned(x):
            return x if isinstance(x, int) else pl.multiple_of(x, blk)

        def loop_body(it, carry):
            items = []
            for u in range(unroll):
                if nb >= unroll:
                    bodies_per_class = nb // unroll
                    r = 0 if d == 1 else it // bodies_per_class
                    jb = u if bodies_per_class == 1 else (it % bodies_per_class) * unroll + u
                else:
                    r = it * (unroll // nb) + u // nb
                    jb = u % nb
                qrow = aligned(r * ql + jb * blk)
                krow = aligned(r * kl + jb * blk)
                first_blk = jb == 0
                orow = aligned(jb * blk) if d == 1 else jb * (blk * d) + r
                for hh in range(nh):
                    cs = slice(hh * hd, (hh + 1) * hd)

                    def store(o, lse, hh=hh, orow=orow):
                        if d == 1:
                            rows = pl.ds(orow, blk)
                        else:
                            rows = pl.ds(orow, blk, stride=d)
                        op_ref[pi, hh, rows, :] = o
                        lp_ref[pi, hh, rows, :] = lse

                    items.append((
                        lambda qrow=qrow, cs=cs: (q_ref if d == 1 else qd_ref)[pl.ds(qrow, blk), cs],
                        lambda krow=krow, cs=cs: kd_ref[pl.ds(krow, 2 * blk), cs],
                        lambda krow=krow, hh=hh: va_ref[hh, pl.ds(krow, 2 * blk), :],
                        hh,
                        lambda first_blk=first_blk: bias_ref[
                            int(first_blk) if isinstance(first_blk, bool) else jnp.where(first_blk, 1, 0)],
                        store))
            head_blocks(items)
            return carry

        lax.fori_loop(0, (d * nb) // unroll, loop_body, 0)

    def deinterleave(d, p_ref):
        pc = grp // d
        ql = W // d
        kl = ql + blk
        ng = W // grp
        pm = p_ref[...]

        def split(x):
            return jnp.dot(pm, x, preferred_element_type=F32).astype(BF16)

        for g in range(ng):
            rows = slice(g * grp, (g + 1) * grp)
            yq, yk, yv = split(q_ref[rows, :]), split(kc_ref[rows, :]), split(vc_ref[rows, :])
            for r in range(d):
                piece = slice(r * pc, (r + 1) * pc)
                qd_ref[r * ql + g * pc:r * ql + (g + 1) * pc, :] = yq[piece]
                kd_ref[r * kl + blk + g * pc:r * kl + blk + (g + 1) * pc, :] = yk[piece]
                put_v(slice(r * kl + blk + g * pc, r * kl + blk + (g + 1) * pc), yv[piece])
        g0 = ng - (blk * d) // grp
        for g in range(g0, ng):
            rows = slice(g * grp, (g + 1) * grp)
            yk, yv = split(kp_ref[rows, :]), split(vp_ref[rows, :])
            for r in range(d):
                piece = slice(r * pc, (r + 1) * pc)
                dst = r * kl + (g - g0) * pc
                kd_ref[dst:dst + pc, :] = yk[piece]
                put_v(slice(dst, dst + pc), yv[piece])

    for pi, d in enumerate(dils):
        if d == 1:
            kd_ref[0:blk, :] = kp_ref[W - blk:W, :]
            kd_ref[blk:blk + W, :] = kc_ref[...]
            put_v(slice(0, blk), vp_ref[W - blk:W, :])
            put_v(slice(blk, blk + W), vc_ref[...])
        else:
            deinterleave(d, p4_ref if d == 4 else p16_ref)
        run_pattern(pi, d)

    fin = 2 * blk

    def fin_body(c, carry):
        rows = pl.ds(pl.multiple_of(c * fin, fin), fin)
        for hh in range(nh):
            ls = [lp_ref[pi, hh, rows, :] for pi in range(len(dils))]
            mx = functools.reduce(jnp.maximum, ls)
            ws = [jnp.exp(l - mx) for l in ls]
            num = sum(w * op_ref[pi, hh, rows, :] for pi, w in enumerate(ws))
            o_ref[rows, hh * hd:(hh + 1) * hd] = num / sum(ws)
        return carry

    lax.fori_loop(0, W // fin, fin_body, 0)


def _attention(pa3, heads, d_attn, col0):
    B, S, _ = pa3.shape
    hd = d_attn // heads
    dils = tuple(d for _, d in DILATION_PATTERNS)
    assert all(w == ATTN_BLOCK * d for w, d in DILATION_PATTERNS) and dils == (1, 4, 16)
    W = ATTN_WINDOW
    lanes = ATTN_HEADS_PER_STEP * hd
    cb = d_attn // lanes
    assert S % W == 0 and W % ATTN_GROUP == 0 and hd == LANES

    def cur(c):
        return pl.BlockSpec((None, W, lanes), lambda b, hp, n: (b, n, (col0 + c) * cb + hp))

    def prv(c):
        return pl.BlockSpec((None, W, lanes), lambda b, hp, n: (b, jnp.maximum(n - 1, 0), (col0 + c) * cb + hp))

    perm = pl.BlockSpec((ATTN_GROUP, ATTN_GROUP), lambda b, hp, n: (0, 0))
    kern = functools.partial(_attn_kernel, hd=hd, dils=dils)
    npat = len(dils)
    return pl.pallas_call(
        kern,
        grid=(B, cb, S // W),
        in_specs=[cur(0), cur(1), prv(1), cur(2), prv(2), perm, perm],
        out_specs=pl.BlockSpec((None, W, lanes), lambda b, hp, n: (b, n, hp)),
        out_shape=jax.ShapeDtypeStruct((B, S, d_attn), F32),
        scratch_shapes=[pltpu.VMEM((W, lanes), BF16),
                        pltpu.VMEM((2 * W, lanes), BF16),
                        pltpu.VMEM((ATTN_HEADS_PER_STEP, 2 * W, lanes), BF16),
                        pltpu.VMEM((npat, ATTN_HEADS_PER_STEP, W, hd), F32),
                        pltpu.VMEM((npat, ATTN_HEADS_PER_STEP, W, hd), F32),
                        pltpu.VMEM((2, ATTN_BLOCK, 2 * ATTN_BLOCK), F32)],
        compiler_params=_cp("arbitrary", "arbitrary", "arbitrary"),
        name="attn",
    )(pa3, pa3, pa3, pa3, pa3, _perm_matrix(4, ATTN_GROUP), _perm_matrix(16, ATTN_GROUP))


def _outproj_kernel(ya_ref, ob_ref, x_ref, w_ref, gattn_ref, gpost_ref, gate1_ref, gpre_ref,
                    sc2_ref, sh2_ref, wr_ref, br_ref, x1_ref, xl_ref, route_ref, cnt_ref, cat_ref, h2_ref,
                    *, n_groups, epg, sub):
    dg = ya_ref.shape[1]
    half = x_ref.shape[1] // 2
    subs = [slice(k * sub, (k + 1) * sub) for k in range(ya_ref.shape[0] // sub)]

    @pl.when(pl.program_id(0) == 0)
    def _():
        cnt_ref[...] = jnp.zeros_like(cnt_ref)

    for rs in subs:
        cat_ref[rs, :dg] = ya_ref[rs, :]
        cat_ref[rs, dg:] = (_rms(ob_ref[rs, :]) * gattn_ref[...]).astype(BF16)
    ys = [jnp.dot(cat_ref[rs, :], w_ref[...], preferred_element_type=F32) for rs in subs]
    for rs, y in zip(subs, ys):
        x1 = x_ref[rs, :] + gate1_ref[...] * (_rms(y) * gpost_ref[...])
        x1_ref[rs, :] = x1
        h2_ref[rs, :] = (_rms(x1) * gpre_ref[...]) * (1.0 + sc2_ref[...]) + sh2_ref[...]

    @pl.when(pl.program_id(0) >= 0)
    def _():
        for k, rs in enumerate(subs):
            _route_rows(h2_ref[rs, :], wr_ref, br_ref, xl_ref, route_ref, cnt_ref, rs, k * sub, half, n_groups, epg)


def _route_rows(h2, wr_ref, br_ref, xl_ref, route_ref, cnt_ref, rs, row0, half, n_groups, epg):
    tm = h2.shape[0]
    hb = h2.astype(BF16)
    hb32 = hb.astype(F32)
    words = (lax.shift_right_logical(pltpu.bitcast(hb32[:, :half], jnp.uint32), jnp.uint32(16))
             | (pltpu.bitcast(hb32[:, half:], jnp.uint32) & jnp.uint32(0xFFFF0000)))
    _store_slabs(xl_ref, words, row0)

    lo = (h2 - hb32).astype(BF16)
    r = (jnp.dot(hb, wr_ref[...], preferred_element_type=F32)
         + jnp.dot(lo, wr_ref[...], preferred_element_type=F32))
    logits = r[:, :LANES] + r[:, LANES:] + br_ref[...]

    lane = lax.broadcasted_iota(jnp.int32, (tm, LANES), 1)
    lane_f = lane.astype(F32)
    big = float(LANES)
    lg = jnp.where(lane < n_groups, logits, NEG)
    mg = jnp.max(lg, axis=-1, keepdims=True)
    gi = jnp.min(jnp.where(lg == mg, lane_f, big), axis=-1, keepdims=True)
    gp = 1.0 / jnp.sum(jnp.exp(lg - mg), axis=-1, keepdims=True)
    e_lo = n_groups + gi * epg
    le = jnp.where(lane_f >= e_lo, jnp.where(lane_f < e_lo + epg, logits, NEG), NEG)
    m1 = jnp.max(le, axis=-1, keepdims=True)
    i1 = jnp.min(jnp.where(le == m1, lane_f, big), axis=-1, keepdims=True)
    le2 = jnp.where(lane_f == i1, NEG, le)
    m2 = jnp.max(le2, axis=-1, keepdims=True)
    i2 = jnp.min(jnp.where(le2 == m2, lane_f, big), axis=-1, keepdims=True)
    t = jnp.exp(m2 - m1)
    w1 = gp / (1.0 + t)
    w2 = gp * t / (1.0 + t)
    e1, e2 = i1 - n_groups, i2 - n_groups
    route_ref[rs, :] = jnp.where(lane == 0, e1,
                                 jnp.where(lane == 1, e2, jnp.where(lane == 2, w1, jnp.where(lane == 3, w2, 0.0))))
    cnt_ref[...] = cnt_ref[...] + jnp.sum(jnp.where(lane_f == e1, 1.0, jnp.where(lane_f == e2, 1.0, 0.0)),
                                          axis=0, keepdims=True)


def _outproj(ya_n, ob, x2, w_out_bf, g_attn, g_post, mod3, g_pre, wr, br, seq, n_groups, epg):
    N, D = x2.shape
    dg = ya_n.shape[1]
    tm = TM_OUTPROJ
    tpb = seq // tm
    assert D == 2 * ROW_SLAB * LANES
    row = lambda i: (i, 0)
    const = lambda i: (0, 0)
    modc = lambda k: pl.BlockSpec((None, 1, D), lambda i: (i // tpb, 0, k))
    resident = dict(pipeline_mode=pl.Buffered(1))
    kern = functools.partial(_outproj_kernel, n_groups=n_groups, epg=epg, sub=SUB_OUTPROJ)
    return pl.pallas_call(
        kern,
        grid=(N // tm,),
        in_specs=[pl.BlockSpec((tm, dg), row), pl.BlockSpec((tm, dg), row), pl.BlockSpec((tm, D), row),
                  pl.BlockSpec((D, D), const, **resident), pl.BlockSpec((1, dg), const), pl.BlockSpec((1, D), const),
                  modc(2), pl.BlockSpec((1, D), const), modc(4), modc(3),
                  pl.BlockSpec((D, 2 * LANES), const, **resident), pl.BlockSpec((1, LANES), const)],
        out_specs=[pl.BlockSpec((tm, D), row), pl.BlockSpec((tm * ROW_SLAB, LANES), row),
                   pl.BlockSpec((tm, LANES), row), pl.BlockSpec((1, LANES), const)],
        out_shape=[jax.ShapeDtypeStruct((N, D), F32), jax.ShapeDtypeStruct((N * ROW_SLAB, LANES), jnp.uint32),
                   jax.ShapeDtypeStruct((N, LANES), F32), jax.ShapeDtypeStruct((1, LANES), F32)],
        scratch_shapes=[pltpu.VMEM((tm, D), BF16), pltpu.VMEM((tm, D), F32)],
        compiler_params=_cp("arbitrary"),
        name="outproj",
    )(ya_n, ob, x2, w_out_bf, g_attn, g_post, mod3, g_pre, mod3, mod3, wr, br)


def _rank_kernel(route_ref, cnt_ref, dest_ref, carry_ref, *, tile):
    tr = route_ref.shape[0]

    @pl.when(pl.program_id(0) == 0)
    def _():
        counts = jnp.broadcast_to(cnt_ref[...], (ROW_SLAB, LANES))
        padded = jnp.floor((counts + (tile - 1)) * (1.0 / tile)) * tile
        lane8 = lax.broadcasted_iota(jnp.int32, (ROW_SLAB, LANES), 1)
        incl = padded
        for k in (1, 2, 4, 8, 16, 32, 64):
            incl = incl + jnp.where(lane8 >= k, pltpu.roll(incl, k, axis=1), 0.0)
        carry_ref[...] = (incl - padded)[0:1, :]

    lane = lax.broadcasted_iota(jnp.int32, (tr, LANES), 1)
    lane_f = lane.astype(F32)
    oh0 = lane_f == route_ref[:, 0:1]
    oh1 = lane_f == route_ref[:, 1:2]
    oh = jnp.where(oh0, 1.0, jnp.where(oh1, 1.0, 0.0))
    ri = lax.broadcasted_iota(jnp.int32, (tr, tr), 0)
    ci = lax.broadcasted_iota(jnp.int32, (tr, tr), 1)
    earlier = jnp.where(ci < ri, 1.0, 0.0).astype(BF16)
    row = jnp.dot(earlier, oh.astype(BF16), preferred_element_type=F32) + carry_ref[...]
    d0 = jnp.sum(jnp.where(oh0, row, 0.0), axis=-1, keepdims=True)
    d1 = jnp.sum(jnp.where(oh1, row, 0.0), axis=-1, keepdims=True)
    cols = jnp.where(lane == 0, d0, jnp.where(lane == 1, d1, 0.0))
    rows = cols.T
    dest_ref[...] = jnp.concatenate([rows[s:s + 1, :] for s in range(TOP_K)], axis=1).astype(jnp.int32)
    carry_ref[...] = carry_ref[...] + jnp.sum(oh, axis=0, keepdims=True)


def _rank(route, cnt, tile):
    N = route.shape[0]
    tr = TM_RANK
    return pl.pallas_call(
        functools.partial(_rank_kernel, tile=tile),
        grid=(N // tr,),
        in_specs=[pl.BlockSpec((tr, LANES), lambda i: (i, 0)), pl.BlockSpec((1, LANES), lambda i: (0, 0))],
        out_specs=pl.BlockSpec((None, 1, TOP_K * tr), lambda i: (i, 0, 0)),
        out_shape=jax.ShapeDtypeStruct((N // tr, 1, TOP_K * tr), jnp.int32),
        scratch_shapes=[pltpu.VMEM((1, LANES), F32)],
        compiler_params=_cp("arbitrary"),
        name="rank",
    )(route, cnt)


def _slab(i):
    return pl.ds(pl.multiple_of(i * ROW_SLAB, ROW_SLAB), ROW_SLAB)


def _dispatch_kernel(dest_ref, xl_ref, xs_hbm, sem):
    tm = xl_ref.shape[0] // ROW_SLAB

    def copy(t, d):
        return pltpu.make_async_copy(xl_ref.at[_slab(t)], xs_hbm.at[_slab(d)], sem)

    def start(t, c):
        for s in range(TOP_K):
            copy(t, dest_ref[0, s * tm + t]).start(priority=s % 2)
        return c

    def wait(t, c):
        for s in range(TOP_K):
            copy(t, dest_ref[0, s * tm + t]).wait()
        return c

    lax.fori_loop(0, tm, start, 0, unroll=8)
    lax.fori_loop(0, tm, wait, 0, unroll=8)


def _dispatch(xl, dest, n_rows):
    N = xl.shape[0] // ROW_SLAB
    tm = TM_DISPATCH
    return pl.pallas_call(
        _dispatch_kernel,
        grid=(N // tm,),
        in_specs=[pl.BlockSpec((None, 1, TOP_K * tm), lambda i: (i, 0, 0), memory_space=pltpu.SMEM),
                  pl.BlockSpec((tm * ROW_SLAB, LANES), lambda i: (i, 0))],
        out_specs=pl.BlockSpec(memory_space=pl.ANY),
        out_shape=jax.ShapeDtypeStruct((n_rows * ROW_SLAB, LANES), jnp.uint32),
        scratch_shapes=[pltpu.SemaphoreType.DMA(())],
        compiler_params=_cp("arbitrary"),
        name="dispatch",
    )(dest, xl)


def _ffn_kernel(te_ref, tv_ref, ti_ref, tf_ref, tn_ref, xs_ref, wg_hbm, wu_hbm, wd_hbm, ys_ref,
                x_ref, stage_g, stage_u, stage_d, wg_ref, wu_ref, wd_ref, sems):
    del ti_ref
    t = pl.program_id(0)
    nv = tv_ref[t]
    tm = x_ref.shape[0]
    half = x_ref.shape[1] // 2
    stages = ((wg_hbm, stage_g, wg_ref), (wu_hbm, stage_u, wu_ref), (wd_hbm, stage_d, wd_ref))

    def fetch(e):
        return [pltpu.make_async_copy(src.at[e], stage, sems.at[k]) for k, (src, stage, _) in enumerate(stages)]

    @pl.when(t == 0)
    def _():
        for c in fetch(te_ref[0]):
            c.start()

    @pl.when(tf_ref[t] == 1)
    def _():
        for c in fetch(0):
            c.wait()
        for _, stage, dst in stages:
            rows = stage.shape[0] // FFN_CAST_CHUNKS

            def cast(i, carry, stage=stage, dst=dst, rows=rows):
                rs = pl.ds(pl.multiple_of(i * rows, rows), rows)
                dst[rs, :] = stage[rs, :].astype(BF16)
                return carry

            lax.fori_loop(0, FFN_CAST_CHUNKS, cast, 0)

        @pl.when(tn_ref[t] >= 0)
        def _():
            for c in fetch(tn_ref[t]):
                c.start()

    @pl.when(nv > 0)
    def _():
        keep = lax.broadcasted_iota(jnp.int32, (tm, 1), 0) < nv
        for j in range(ROW_SLAB):
            w = jnp.where(keep, xs_ref[pl.ds(j, tm, stride=ROW_SLAB), :], jnp.uint32(0))
            x_ref[:, j * LANES:(j + 1) * LANES] = _unpack_lo(w).astype(BF16)
            x_ref[:, half + j * LANES:half + (j + 1) * LANES] = _unpack_hi(w).astype(BF16)
        x = x_ref[...]
        g = jnp.dot(x, wg_ref[...], preferred_element_type=F32)
        u = jnp.dot(x, wu_ref[...], preferred_element_type=F32)
        hm = ((g * _sigmoid(g)) * u).astype(BF16)
        _store_slabs(ys_ref, _pack_rows(jnp.dot(hm, wd_ref[...], preferred_element_type=F32)))

    @pl.when(nv == 0)
    def _():
        ys_ref[...] = jnp.zeros_like(ys_ref)


def _ffn(xs, tile_e, tile_valid, tile_in, tile_first, tile_next, wg, wu, wd):
    D, de = wg.shape[1], wg.shape[2]
    tm = TM_FFN
    nt = tile_e.shape[0]
    hbm = pl.BlockSpec(memory_space=pl.ANY)
    grid_spec = pltpu.PrefetchScalarGridSpec(
        num_scalar_prefetch=5,
        grid=(nt,),
        in_specs=[pl.BlockSpec((tm * ROW_SLAB, LANES), lambda t, te, tv, ti, tf, tn: (ti[t], 0)), hbm, hbm, hbm],
        out_specs=pl.BlockSpec((tm * ROW_SLAB, LANES), lambda t, te, tv, ti, tf, tn: (t, 0)),
        scratch_shapes=[pltpu.VMEM((tm, D), BF16),
                        pltpu.VMEM((D, de), F32), pltpu.VMEM((D, de), F32), pltpu.VMEM((de, D), F32),
                        pltpu.VMEM((D, de), BF16), pltpu.VMEM((D, de), BF16), pltpu.VMEM((de, D), BF16),
                        pltpu.SemaphoreType.DMA((3,))],
    )
    return pl.pallas_call(
        _ffn_kernel,
        grid_spec=grid_spec,
        out_shape=jax.ShapeDtypeStruct(xs.shape, jnp.uint32),
        compiler_params=_cp("arbitrary"),
        name="ffn",
    )(tile_e, tile_valid, tile_in, tile_first, tile_next, xs, wg, wu, wd)


def _combine_kernel(dcur_ref, dnext_ref, route_ref, x1_ref, gate2_ref, g_ref, ys_hbm, o_ref, buf, y_ref, sems):
    i = pl.program_id(0)
    nsteps = pl.num_programs(0)
    tm = x1_ref.shape[0]
    half = x1_ref.shape[1] // 2

    def copy(slot, t, s, d):
        return pltpu.make_async_copy(ys_hbm.at[_slab(d)], buf.at[slot, s, _slab(t)], sems.at[slot])

    def start_tile(slot, dref):
        def body(t, c):
            for s in range(TOP_K):
                copy(slot, t, s, dref[0, s * tm + t]).start(priority=s % 2)
            return c
        lax.fori_loop(0, tm, body, 0, unroll=8)

    def wait_tile(slot, dref):
        def body(t, c):
            for s in range(TOP_K):
                copy(slot, t, s, dref[0, s * tm + t]).wait()
            return c
        lax.fori_loop(0, tm, body, 0, unroll=8)

    @pl.when(i == 0)
    def _():
        start_tile(0, dcur_ref)

    for slot in range(2):
        @pl.when(jnp.logical_and(i + 1 < nsteps, (i + 1) % 2 == slot))
        def _():
            start_tile(slot, dnext_ref)

    for slot in range(2):
        @pl.when(i % 2 == slot)
        def _():
            wait_tile(slot, dcur_ref)
            w0 = route_ref[:, 2:3]
            w1 = route_ref[:, 3:4]
            for j in range(ROW_SLAB):
                a = buf[slot, 0, pl.ds(j, tm, stride=ROW_SLAB), :]
                b = buf[slot, 1, pl.ds(j, tm, stride=ROW_SLAB), :]
                y_ref[:, j * LANES:(j + 1) * LANES] = _unpack_lo(a) * w0 + _unpack_lo(b) * w1
                y_ref[:, half + j * LANES:half + (j + 1) * LANES] = _unpack_hi(a) * w0 + _unpack_hi(b) * w1

    o_ref[...] = x1_ref[...] + gate2_ref[...] * (_rms(y_ref[...]) * g_ref[...])


def _combine(ys, dest, route, x1, mod3, g_post, seq):
    N, D = x1.shape
    tm = TM_COMBINE
    tpb = seq // tm
    nsteps = N // tm
    return pl.pallas_call(
        _combine_kernel,
        grid=(nsteps,),
        in_specs=[pl.BlockSpec((None, 1, TOP_K * tm), lambda i: (i, 0, 0), memory_space=pltpu.SMEM),
                  pl.BlockSpec((None, 1, TOP_K * tm), lambda i: (jnp.minimum(i + 1, nsteps - 1), 0, 0),
                               memory_space=pltpu.SMEM),
                  pl.BlockSpec((tm, LANES), lambda i: (i, 0)),
                  pl.BlockSpec((tm, D), lambda i: (i, 0)),
                  pl.BlockSpec((None, 1, D), lambda i: (i // tpb, 0, 5)),
                  pl.BlockSpec((1, D), lambda i: (0, 0)),
                  pl.BlockSpec(memory_space=pl.ANY)],
        out_specs=pl.BlockSpec((tm, D), lambda i: (i, 0)),
        out_shape=jax.ShapeDtypeStruct((N, D), F32),
        scratch_shapes=[pltpu.VMEM((2, TOP_K, tm * ROW_SLAB, LANES), jnp.uint32), pltpu.VMEM((tm, D), F32),
                        pltpu.SemaphoreType.DMA((2,))],
        compiler_params=_cp("arbitrary"),
        name="combine",
    )(dest, dest, route, x1, mod3, g_post, ys)


def _tile_tables(cnt, n_assign, n_experts, tm):
    counts = cnt[0, :n_experts].astype(jnp.int32)
    padded = ((counts + tm - 1) // tm) * tm
    pend = jnp.cumsum(padded)
    pstart = pend - padded
    nt = n_assign // tm + n_experts
    n_used = pend[-1] // tm
    tidx = jnp.arange(nt, dtype=jnp.int32)
    tstart = tidx * tm
    te = jnp.minimum(jnp.sum(pend[None, :] <= tstart[:, None], axis=1), n_experts - 1).astype(jnp.int32)
    tv = jnp.clip(counts[te] - (tstart - pstart[te]), 0, tm).astype(jnp.int32)
    used = tidx < n_used
    last = jnp.maximum(n_used - 1, 0)
    te = jnp.where(used, te, te[last]).astype(jnp.int32)
    ti = jnp.where(used, tidx, last).astype(jnp.int32)
    first = used & ((tidx == 0) | (te != jnp.roll(te, 1)))
    nxt = lax.cummin(jnp.where(first, tidx, nt)[::-1])[::-1]
    nxt = jnp.concatenate([nxt[1:], jnp.full((1,), nt, jnp.int32)])
    tn = jnp.where(first & (nxt < nt), te[jnp.minimum(nxt, nt - 1)], -1).astype(jnp.int32)
    return te, tv, ti, first.astype(jnp.int32), tn, nt


def kernel(x, c, w_mod, b_mod, g_pre_mix, g_post_mix, w_in, g_gmlp_v, w_spatial, b_spatial, g_out_gmlp,
           g_out_attn, w_out, g_pre_ffn, g_post_ffn, w_router_group, b_router_group, w_router_expert,
           b_router_expert, w_gate, w_up, w_down):
    B, S, D = x.shape
    N = B * S
    depth = w_mod.shape[0]
    d_gmlp = g_gmlp_v.shape[1]
    d_attn = g_out_attn.shape[1]
    heads = w_spatial.shape[1]
    n_groups, epg = b_router_expert.shape[1], b_router_expert.shape[2]
    n_experts = n_groups * epg
    assert d_gmlp == d_attn and w_in.shape[2] == 2 * d_gmlp + 3 * d_attn
    assert n_groups + n_experts <= LANES

    x2 = x.reshape(N, D)
    for l in range(depth):
        mod3 = _modulation(c, w_mod[l], b_mod[l]).reshape(B, 1, w_mod.shape[2])

        pa = _inproj(x2, mod3, g_pre_mix[l][None], w_in[l].astype(BF16), g_gmlp_v[l][None], S)
        ya_n = _gmlp(pa, w_spatial[l], b_spatial[l], g_out_gmlp[l][None])
        ob = _attention(pa.reshape(B, S, pa.shape[1]), heads, d_attn, (2 * d_gmlp) // d_attn).reshape(N, d_attn)

        wr32 = jnp.concatenate([w_router_group[l],
                                jnp.transpose(w_router_expert[l], (1, 0, 2)).reshape(D, n_experts)], axis=1)
        wr32 = jnp.pad(wr32, ((0, 0), (0, LANES - wr32.shape[1])))
        wr_hi = wr32.astype(BF16)
        wr = jnp.concatenate([wr_hi, (wr32 - wr_hi.astype(F32)).astype(BF16)], axis=1)
        br = jnp.pad(jnp.concatenate([b_router_group[l], b_router_expert[l].reshape(n_experts)]),
                     (0, LANES - n_groups - n_experts))[None]

        x1, xl, route, cnt = _outproj(ya_n, ob, x2, w_out[l].astype(BF16), g_out_attn[l][None],
                                      g_post_mix[l][None], mod3, g_pre_ffn[l][None], wr, br, S, n_groups, epg)

        dest = _rank(route, cnt, TM_FFN)
        te, tv, ti, tf, tn, nt = _tile_tables(cnt, N * TOP_K, n_experts, TM_FFN)
        xs = _dispatch(xl, dest, nt * TM_FFN)
        ys = _ffn(xs, te, tv, ti, tf, tn, w_gate[l], w_up[l], w_down[l])
        x2 = _combine(ys, dest, route, x1, mod3, g_post_ffn[l][None], S)
    return x2.reshape(B, S, D)
```

```python
import functools
import math

import jax
import jax.numpy as jnp
from jax import lax
from jax.experimental import pallas as pl
from jax.experimental.pallas import tpu as pltpu

F32 = jnp.float32
BF16 = jnp.bfloat16
EPS = 1e-6
NEG = -1e30

DILATION_PATTERNS = ((128, 1), (512, 4), (2048, 16))
ATTN_BLOCK = 128
TOP_K = 2
LANES = 128

TM_INPROJ = 512
SUB_INPROJ = 256
TM_GMLP = 512
ATTN_WINDOW = ATTN_BLOCK * max(d for _, d in DILATION_PATTERNS)
ATTN_GROUP = 256
ATTN_HEADS_PER_STEP = 2
ATTN_UNROLL = 8
TM_OUTPROJ = 512
SUB_OUTPROJ = 256
TM_RANK = 512
TM_DISPATCH = TM_RANK
TM_FFN = 256
FFN_CAST_CHUNKS = 8
TM_COMBINE = 256
TN_MOD = 1024
ROW_SLAB = 8
VMEM_LIMIT = 56 * 1024 * 1024


def _cp(*dims):
    return pltpu.CompilerParams(dimension_semantics=dims, vmem_limit_bytes=VMEM_LIMIT)


def _rms(x):
    return x * lax.rsqrt(jnp.mean(x * x, axis=-1, keepdims=True) + EPS)


def _gelu_tanh(x):
    c = math.sqrt(2.0 / math.pi)
    return x * (0.5 * (1.0 + jnp.tanh(c * (x + 0.044715 * (x * x * x)))))


def _sigmoid(x):
    return 1.0 / (1.0 + jnp.exp(-x))


def _pack_rows(x):
    half = x.shape[1] // 2
    lo = pltpu.bitcast(x[:, :half].astype(BF16).astype(F32), jnp.uint32)
    hi = pltpu.bitcast(x[:, half:].astype(BF16).astype(F32), jnp.uint32)
    return lax.shift_right_logical(lo, jnp.uint32(16)) | (hi & jnp.uint32(0xFFFF0000))


def _unpack_lo(w):
    return pltpu.bitcast(lax.shift_left(w, jnp.uint32(16)), F32)


def _unpack_hi(w):
    return pltpu.bitcast(w & jnp.uint32(0xFFFF0000), F32)


def _store_slabs(ref, words, row0=0):
    rows = words.shape[0]
    for j in range(ROW_SLAB):
        ref[pl.ds(row0 * ROW_SLAB + j, rows, stride=ROW_SLAB), :] = words[:, j * LANES:(j + 1) * LANES]


def _mod_kernel(ct_ref, w_ref, b_ref, o_ref):
    ct = ct_ref[...]
    at = ct * _sigmoid(ct)
    w = w_ref[...]
    for b in range(o_ref.shape[0]):
        o_ref[b:b + 1, :] = jnp.sum(w * at[:, b:b + 1], axis=0, keepdims=True) + b_ref[...]


def _modulation(c, w_mod, b_mod):
    B, D = c.shape
    n_out = w_mod.shape[1]
    return pl.pallas_call(
        _mod_kernel,
        grid=(n_out // TN_MOD,),
        in_specs=[pl.BlockSpec((D, B), lambda j: (0, 0)),
                  pl.BlockSpec((D, TN_MOD), lambda j: (0, j)),
                  pl.BlockSpec((1, TN_MOD), lambda j: (0, j))],
        out_specs=pl.BlockSpec((B, TN_MOD), lambda j: (0, j)),
        out_shape=jax.ShapeDtypeStruct((B, n_out), F32),
        compiler_params=_cp("arbitrary"),
        name="mod",
    )(c.T, w_mod, b_mod.reshape(1, n_out))


def _inproj_kernel(x_ref, g_ref, sc_ref, sh_ref, w_ref, gv_ref, o_ref, h_ref, *, sub):
    s = pl.program_id(0)
    last = pl.num_programs(0) - 1
    tm = x_ref.shape[0]
    tn = gv_ref.shape[1]

    def normalise():
        h = (_rms(x_ref[...]) * g_ref[...]) * (1.0 + sc_ref[...]) + sh_ref[...]
        h_ref[pl.ds(pl.multiple_of((s % 2) * tm, tm), tm), :] = h.astype(BF16)

    def project():
        base = pl.multiple_of(((s + 1) % 2) * tm, tm)
        for j in range(w_ref.shape[1] // tn):
            cs = slice(j * tn, (j + 1) * tn)
            for k in range(tm // sub):
                acc = jnp.dot(h_ref[pl.ds(base + k * sub, sub), :], w_ref[:, cs], preferred_element_type=F32)
                if j == 0:
                    acc = _gelu_tanh(acc)
                elif j == 1:
                    v = _gelu_tanh(acc)
                    vc = v - jnp.mean(v, axis=-1, keepdims=True)
                    acc = vc * lax.rsqrt(jnp.mean(vc * vc, axis=-1, keepdims=True) + EPS) * gv_ref[...]
                o_ref[k * sub:(k + 1) * sub, cs] = acc.astype(BF16)

    @pl.when(s == 0)
    def _():
        normalise()

    @pl.when(jnp.logical_and(s > 0, s < last))
    def _():
        normalise()
        project()

    @pl.when(s == last)
    def _():
        project()


def _inproj(x2, mod3, g_pre, w_in_bf, g_v, seq):
    N, D = x2.shape
    d_in = w_in_bf.shape[1]
    tn = g_v.shape[1]
    tm = TM_INPROJ
    tpb = seq // tm
    nt = N // tm
    cur = lambda s: jnp.minimum(s, nt - 1)
    return pl.pallas_call(
        functools.partial(_inproj_kernel, sub=SUB_INPROJ),
        grid=(nt + 1,),
        in_specs=[pl.BlockSpec((tm, D), lambda s: (cur(s), 0)),
                  pl.BlockSpec((1, D), lambda s: (0, 0)),
                  pl.BlockSpec((None, 1, D), lambda s: (cur(s) // tpb, 0, 1)),
                  pl.BlockSpec((None, 1, D), lambda s: (cur(s) // tpb, 0, 0)),
                  pl.BlockSpec((D, d_in), lambda s: (0, 0), pipeline_mode=pl.Buffered(1)),
                  pl.BlockSpec((1, tn), lambda s: (0, 0))],
        out_specs=pl.BlockSpec((tm, d_in), lambda s: (jnp.maximum(s - 1, 0), 0)),
        out_shape=jax.ShapeDtypeStruct((N, d_in), BF16),
        scratch_shapes=[pltpu.VMEM((2 * tm, D), BF16)],
        compiler_params=_cp("arbitrary"),
        name="inproj",
    )(x2, g_pre, mod3, mod3, w_in_bf, g_v)


def _gmlp_kernel(u_ref, v_ref, w_ref, bt_ref, g_ref, o_ref, ya_ref, *, heads, chunk, hd):
    tm = u_ref.shape[0]
    row = lax.broadcasted_iota(jnp.int32, (chunk, chunk), 0)
    col = lax.broadcasted_iota(jnp.int32, (chunk, chunk), 1)
    causal = col <= row
    for h in range(heads):
        wm = jnp.where(causal, w_ref[h], 0.0).astype(BF16)
        bcol = bt_ref[:, h:h + 1]
        cs = slice(h * hd, (h + 1) * hd)
        for c in range(tm // chunk):
            rs = slice(c * chunk, (c + 1) * chunk)
            sv = jnp.dot(wm, v_ref[rs, cs], preferred_element_type=F32) + bcol
            ya_ref[rs, cs] = u_ref[rs, cs].astype(F32) * sv
    o_ref[...] = (_rms(ya_ref[...]) * g_ref[...]).astype(BF16)


def _gmlp(pa, w_spatial, b_spatial, g_out):
    N = pa.shape[0]
    heads, chunk, _ = w_spatial.shape
    dg = g_out.shape[1]
    kern = functools.partial(_gmlp_kernel, heads=heads, chunk=chunk, hd=dg // heads)
    return pl.pallas_call(
        kern,
        grid=(N // TM_GMLP,),
        in_specs=[pl.BlockSpec((TM_GMLP, dg), lambda i: (i, 0)),
                  pl.BlockSpec((TM_GMLP, dg), lambda i: (i, 1)),
                  pl.BlockSpec((heads, chunk, chunk), lambda i: (0, 0, 0)),
                  pl.BlockSpec((chunk, heads), lambda i: (0, 0)),
                  pl.BlockSpec((1, dg), lambda i: (0, 0))],
        out_specs=pl.BlockSpec((TM_GMLP, dg), lambda i: (i, 0)),
        out_shape=jax.ShapeDtypeStruct((N, dg), BF16),
        scratch_shapes=[pltpu.VMEM((TM_GMLP, dg), F32)],
        compiler_params=_cp("arbitrary"),
        name="gmlp",
    )(pa, pa, w_spatial, b_spatial.T, g_out)


def _perm_matrix(d, n):
    i = jnp.arange(n)
    src = (i % (n // d)) * d + i // (n // d)
    return (src[:, None] == jnp.arange(n)[None, :]).astype(BF16)


def _attn_kernel(q_ref, kc_ref, kp_ref, vc_ref, vp_ref, p4_ref, p16_ref, o_ref,
                 qd_ref, kd_ref, va_ref, op_ref, lp_ref, bias_ref, *, hd, dils):
    n = pl.program_id(2)
    blk = ATTN_BLOCK
    grp = ATTN_GROUP
    W = q_ref.shape[0]
    nh = q_ref.shape[1] // hd
    scale = hd ** -0.5
    c2 = scale * math.log2(math.e)
    nt = (((1,), (1,)), ((), ()))

    ri = lax.broadcasted_iota(jnp.int32, (blk, 2 * blk), 0)
    ci = lax.broadcasted_iota(jnp.int32, (blk, 2 * blk), 1)
    band = jnp.where(ci >= ri, jnp.where(ci <= ri + blk, 0.0, NEG), NEG)
    bias_ref[0] = band
    bias_ref[1] = jnp.where(n > 0, band, jnp.where(ci < blk, NEG, band))

    def put_v(rows, x):
        ones = jnp.ones((x.shape[0], hd), BF16)
        for hh in range(nh):
            parts = [x[:, hh * hd:(hh + 1) * hd] if c == hh else ones for c in range(nh)]
            va_ref[hh, rows, :] = jnp.concatenate(parts, axis=1)

    def head_blocks(items):
        cnt = len(items)
        ss, ms, ps = [None] * cnt, [None] * cnt, [None] * cnt

        def scores(i):
            lq, lk, _, _, lb, _ = items[i]
            ss[i] = lax.dot_general(lq(), lk(), nt, preferred_element_type=F32) + lb()

        def probs(i):
            ms[i] = jnp.max(ss[i], axis=-1, keepdims=True)
            ps[i] = jnp.exp2((ss[i] - ms[i]) * c2).astype(BF16)

        def output(i):
            _, _, lva, hh, _, store = items[i]
            r = jnp.dot(ps[i], lva(), preferred_element_type=F32)
            oh = (hh + 1) % nh
            l = r[:, oh * hd:(oh + 1) * hd]
            store(r[:, hh * hd:(hh + 1) * hd] / l, ms[i] * scale + jnp.log(l))

        for stage in (scores, probs, output):
            for i in range(cnt):
                stage(i)

    def run_pattern(pi, d):
        ql = W // d
        kl = ql + blk
        nb = ql // blk
        unroll = ATTN_UNROLL
        assert nb % unroll == 0 or unroll % nb == 0

        def aligned(x):
            return x if isinstance(x, int) else pl.multiple_of(x, blk)

        def loop_body(it, carry):
            items = []
            for u in range(unroll):
                if nb >= unroll:
                    bodies_per_class = nb // unroll
                    r = 0 if d == 1 else it // bodies_per_class
                    jb = u if bodies_per_class == 1 else (it % bodies_per_class) * unroll + u
                else:
                    r = it * (unroll // nb) + u // nb
                    jb = u % nb
                qrow = aligned(r * ql + jb * blk)
                krow = aligned(r * kl + jb * blk)
                first_blk = jb == 0
                orow = aligned(jb * blk) if d == 1 else jb * (blk * d) + r
                for hh in range(nh):
                    cs = slice(hh * hd, (hh + 1) * hd)

                    def store(o, lse, hh=hh, orow=orow):
                        if d == 1:
                            rows = pl.ds(orow, blk)
                        else:
                            rows = pl.ds(orow, blk, stride=d)
                        op_ref[pi, hh, rows, :] = o
                        lp_ref[pi, hh, rows, :] = lse

                    items.append((
                        lambda qrow=qrow, cs=cs: (q_ref if d == 1 else qd_ref)[pl.ds(qrow, blk), cs],
                        lambda krow=krow, cs=cs: kd_ref[pl.ds(krow, 2 * blk), cs],
                        lambda krow=krow, hh=hh: va_ref[hh, pl.ds(krow, 2 * blk), :],
                        hh,
                        lambda first_blk=first_blk: bias_ref[
                            int(first_blk) if isinstance(first_blk, bool) else jnp.where(first_blk, 1, 0)],
                        store))
            head_blocks(items)
            return carry

        lax.fori_loop(0, (d * nb) // unroll, loop_body, 0)

    def deinterleave(d, p_ref):
        pc = grp // d
        ql = W // d
        kl = ql + blk
        ng = W // grp
        pm = p_ref[...]

        def split(x):
            return jnp.dot(pm, x, preferred_element_type=F32).astype(BF16)

        for g in range(ng):
            rows = slice(g * grp, (g + 1) * grp)
            yq, yk, yv = split(q_ref[rows, :]), split(kc_ref[rows, :]), split(vc_ref[rows, :])
            for r in range(d):
                piece = slice(r * pc, (r + 1) * pc)
                qd_ref[r * ql + g * pc:r * ql + (g + 1) * pc, :] = yq[piece]
                kd_ref[r * kl + blk + g * pc:r * kl + blk + (g + 1) * pc, :] = yk[piece]
                put_v(slice(r * kl + blk + g * pc, r * kl + blk + (g + 1) * pc), yv[piece])
        g0 = ng - (blk * d) // grp
        for g in range(g0, ng):
            rows = slice(g * grp, (g + 1) * grp)
            yk, yv = split(kp_ref[rows, :]), split(vp_ref[rows, :])
            for r in range(d):
                piece = slice(r * pc, (r + 1) * pc)
                dst = r * kl + (g - g0) * pc
                kd_ref[dst:dst + pc, :] = yk[piece]
                put_v(slice(dst, dst + pc), yv[piece])

    for pi, d in enumerate(dils):
        if d == 1:
            kd_ref[0:blk, :] = kp_ref[W - blk:W, :]
            kd_ref[blk:blk + W, :] = kc_ref[...]
            put_v(slice(0, blk), vp_ref[W - blk:W, :])
            put_v(slice(blk, blk + W), vc_ref[...])
        else:
            deinterleave(d, p4_ref if d == 4 else p16_ref)
        run_pattern(pi, d)

    fin = 2 * blk

    def fin_body(c, carry):
        rows = pl.ds(pl.multiple_of(c * fin, fin), fin)
        for hh in range(nh):
            ls = [lp_ref[pi, hh, rows, :] for pi in range(len(dils))]
            mx = functools.reduce(jnp.maximum, ls)
            ws = [jnp.exp(l - mx) for l in ls]
            num = sum(w * op_ref[pi, hh, rows, :] for pi, w in enumerate(ws))
            o_ref[rows, hh * hd:(hh + 1) * hd] = num / sum(ws)
        return carry

    lax.fori_loop(0, W // fin, fin_body, 0)


def _attention(pa3, heads, d_attn, col0):
    B, S, _ = pa3.shape
    hd = d_attn // heads
    dils = tuple(d for _, d in DILATION_PATTERNS)
    assert all(w == ATTN_BLOCK * d for w, d in DILATION_PATTERNS) and dils == (1, 4, 16)
    W = ATTN_WINDOW
    lanes = ATTN_HEADS_PER_STEP * hd
    cb = d_attn // lanes
    assert S % W == 0 and W % ATTN_GROUP == 0 and hd == LANES

    def cur(c):
        return pl.BlockSpec((None, W, lanes), lambda b, hp, n: (b, n, (col0 + c) * cb + hp))

    def prv(c):
        return pl.BlockSpec((None, W, lanes), lambda b, hp, n: (b, jnp.maximum(n - 1, 0), (col0 + c) * cb + hp))

    perm = pl.BlockSpec((ATTN_GROUP, ATTN_GROUP), lambda b, hp, n: (0, 0))
    kern = functools.partial(_attn_kernel, hd=hd, dils=dils)
    npat = len(dils)
    return pl.pallas_call(
        kern,
        grid=(B, cb, S // W),
        in_specs=[cur(0), cur(1), prv(1), cur(2), prv(2), perm, perm],
        out_specs=pl.BlockSpec((None, W, lanes), lambda b, hp, n: (b, n, hp)),
        out_shape=jax.ShapeDtypeStruct((B, S, d_attn), F32),
        scratch_shapes=[pltpu.VMEM((W, lanes), BF16),
                        pltpu.VMEM((2 * W, lanes), BF16),
                        pltpu.VMEM((ATTN_HEADS_PER_STEP, 2 * W, lanes), BF16),
                        pltpu.VMEM((npat, ATTN_HEADS_PER_STEP, W, hd), F32),
                        pltpu.VMEM((npat, ATTN_HEADS_PER_STEP, W, hd), F32),
                        pltpu.VMEM((2, ATTN_BLOCK, 2 * ATTN_BLOCK), F32)],
        compiler_params=_cp("arbitrary", "arbitrary", "arbitrary"),
        name="attn",
    )(pa3, pa3, pa3, pa3, pa3, _perm_matrix(4, ATTN_GROUP), _perm_matrix(16, ATTN_GROUP))


def _outproj_kernel(ya_ref, ob_ref, x_ref, w_ref, gattn_ref, gpost_ref, gate1_ref, gpre_ref,
                    sc2_ref, sh2_ref, wr_ref, br_ref, x1_ref, xl_ref, route_ref, cnt_ref, cat_ref, h2_ref,
                    *, n_groups, epg, sub):
    dg = ya_ref.shape[1]
    half = x_ref.shape[1] // 2
    subs = [slice(k * sub, (k + 1) * sub) for k in range(ya_ref.shape[0] // sub)]

    @pl.when(pl.program_id(0) == 0)
    def _():
        cnt_ref[...] = jnp.zeros_like(cnt_ref)

    for rs in subs:
        cat_ref[rs, :dg] = ya_ref[rs, :]
        cat_ref[rs, dg:] = (_rms(ob_ref[rs, :]) * gattn_ref[...]).astype(BF16)
    ys = [jnp.dot(cat_ref[rs, :], w_ref[...], preferred_element_type=F32) for rs in subs]
    for rs, y in zip(subs, ys):
        x1 = x_ref[rs, :] + gate1_ref[...] * (_rms(y) * gpost_ref[...])
        x1_ref[rs, :] = x1
        h2_ref[rs, :] = (_rms(x1) * gpre_ref[...]) * (1.0 + sc2_ref[...]) + sh2_ref[...]

    @pl.when(pl.program_id(0) >= 0)
    def _():
        for k, rs in enumerate(subs):
            _route_rows(h2_ref[rs, :], wr_ref, br_ref, xl_ref, route_ref, cnt_ref, rs, k * sub, half, n_groups, epg)


def _route_rows(h2, wr_ref, br_ref, xl_ref, route_ref, cnt_ref, rs, row0, half, n_groups, epg):
    tm = h2.shape[0]
    hb = h2.astype(BF16)
    hb32 = hb.astype(F32)
    words = (lax.shift_right_logical(pltpu.bitcast(hb32[:, :half], jnp.uint32), jnp.uint32(16))
             | (pltpu.bitcast(hb32[:, half:], jnp.uint32) & jnp.uint32(0xFFFF0000)))
    _store_slabs(xl_ref, words, row0)

    lo = (h2 - hb32).astype(BF16)
    r = (jnp.dot(hb, wr_ref[...], preferred_element_type=F32)
         + jnp.dot(lo, wr_ref[...], preferred_element_type=F32))
    logits = r[:, :LANES] + r[:, LANES:] + br_ref[...]

    lane = lax.broadcasted_iota(jnp.int32, (tm, LANES), 1)
    lane_f = lane.astype(F32)
    big = float(LANES)
    lg = jnp.where(lane < n_groups, logits, NEG)
    mg = jnp.max(lg, axis=-1, keepdims=True)
    gi = jnp.min(jnp.where(lg == mg, lane_f, big), axis=-1, keepdims=True)
    gp = 1.0 / jnp.sum(jnp.exp(lg - mg), axis=-1, keepdims=True)
    e_lo = n_groups + gi * epg
    le = jnp.where(lane_f >= e_lo, jnp.where(lane_f < e_lo + epg, logits, NEG), NEG)
    m1 = jnp.max(le, axis=-1, keepdims=True)
    i1 = jnp.min(jnp.where(le == m1, lane_f, big), axis=-1, keepdims=True)
    le2 = jnp.where(lane_f == i1, NEG, le)
    m2 = jnp.max(le2, axis=-1, keepdims=True)
    i2 = jnp.min(jnp.where(le2 == m2, lane_f, big), axis=-1, keepdims=True)
    t = jnp.exp(m2 - m1)
    w1 = gp / (1.0 + t)
    w2 = gp * t / (1.0 + t)
    e1, e2 = i1 - n_groups, i2 - n_groups
    route_ref[rs, :] = jnp.where(lane == 0, e1,
                                 jnp.where(lane == 1, e2, jnp.where(lane == 2, w1, jnp.where(lane == 3, w2, 0.0))))
    cnt_ref[...] = cnt_ref[...] + jnp.sum(jnp.where(lane_f == e1, 1.0, jnp.where(lane_f == e2, 1.0, 0.0)),
                                          axis=0, keepdims=True)


def _outproj(ya_n, ob, x2, w_out_bf, g_attn, g_post, mod3, g_pre, wr, br, seq, n_groups, epg):
    N, D = x2.shape
    dg = ya_n.shape[1]
    tm = TM_OUTPROJ
    tpb = seq // tm
    assert D == 2 * ROW_SLAB * LANES
    row = lambda i: (i, 0)
    const = lambda i: (0, 0)
    modc = lambda k: pl.BlockSpec((None, 1, D), lambda i: (i // tpb, 0, k))
    resident = dict(pipeline_mode=pl.Buffered(1))
    kern = functools.partial(_outproj_kernel, n_groups=n_groups, epg=epg, sub=SUB_OUTPROJ)
    return pl.pallas_call(
        kern,
        grid=(N // tm,),
        in_specs=[pl.BlockSpec((tm, dg), row), pl.BlockSpec((tm, dg), row), pl.BlockSpec((tm, D), row),
                  pl.BlockSpec((D, D), const, **resident), pl.BlockSpec((1, dg), const), pl.BlockSpec((1, D), const),
                  modc(2), pl.BlockSpec((1, D), const), modc(4), modc(3),
                  pl.BlockSpec((D, 2 * LANES), const, **resident), pl.BlockSpec((1, LANES), const)],
        out_specs=[pl.BlockSpec((tm, D), row), pl.BlockSpec((tm * ROW_SLAB, LANES), row),
                   pl.BlockSpec((tm, LANES), row), pl.BlockSpec((1, LANES), const)],
        out_shape=[jax.ShapeDtypeStruct((N, D), F32), jax.ShapeDtypeStruct((N * ROW_SLAB, LANES), jnp.uint32),
                   jax.ShapeDtypeStruct((N, LANES), F32), jax.ShapeDtypeStruct((1, LANES), F32)],
        scratch_shapes=[pltpu.VMEM((tm, D), BF16), pltpu.VMEM((tm, D), F32)],
        compiler_params=_cp("arbitrary"),
        name="outproj",
    )(ya_n, ob, x2, w_out_bf, g_attn, g_post, mod3, g_pre, mod3, mod3, wr, br)


def _rank_kernel(route_ref, cnt_ref, dest_ref, carry_ref, *, tile):
    tr = route_ref.shape[0]

    @pl.when(pl.program_id(0) == 0)
    def _():
        counts = jnp.broadcast_to(cnt_ref[...], (ROW_SLAB, LANES))
        padded = jnp.floor((counts + (tile - 1)) * (1.0 / tile)) * tile
        lane8 = lax.broadcasted_iota(jnp.int32, (ROW_SLAB, LANES), 1)
        incl = padded
        for k in (1, 2, 4, 8, 16, 32, 64):
            incl = incl + jnp.where(lane8 >= k, pltpu.roll(incl, k, axis=1), 0.0)
        carry_ref[...] = (incl - padded)[0:1, :]

    lane = lax.broadcasted_iota(jnp.int32, (tr, LANES), 1)
    lane_f = lane.astype(F32)
    oh0 = lane_f == route_ref[:, 0:1]
    oh1 = lane_f == route_ref[:, 1:2]
    oh = jnp.where(oh0, 1.0, jnp.where(oh1, 1.0, 0.0))
    ri = lax.broadcasted_iota(jnp.int32, (tr, tr), 0)
    ci = lax.broadcasted_iota(jnp.int32, (tr, tr), 1)
    earlier = jnp.where(ci < ri, 1.0, 0.0).astype(BF16)
    row = jnp.dot(earlier, oh.astype(BF16), preferred_element_type=F32) + carry_ref[...]
    d0 = jnp.sum(jnp.where(oh0, row, 0.0), axis=-1, keepdims=True)
    d1 = jnp.sum(jnp.where(oh1, row, 0.0), axis=-1, keepdims=True)
    cols = jnp.where(lane == 0, d0, jnp.where(lane == 1, d1, 0.0))
    rows = cols.T
    dest_ref[...] = jnp.concatenate([rows[s:s + 1, :] for s in range(TOP_K)], axis=1).astype(jnp.int32)
    carry_ref[...] = carry_ref[...] + jnp.sum(oh, axis=0, keepdims=True)


def _rank(route, cnt, tile):
    N = route.shape[0]
    tr = TM_RANK
    return pl.pallas_call(
        functools.partial(_rank_kernel, tile=tile),
        grid=(N // tr,),
        in_specs=[pl.BlockSpec((tr, LANES), lambda i: (i, 0)), pl.BlockSpec((1, LANES), lambda i: (0, 0))],
        out_specs=pl.BlockSpec((None, 1, TOP_K * tr), lambda i: (i, 0, 0)),
        out_shape=jax.ShapeDtypeStruct((N // tr, 1, TOP_K * tr), jnp.int32),
        scratch_shapes=[pltpu.VMEM((1, LANES), F32)],
        compiler_params=_cp("arbitrary"),
        name="rank",
    )(route, cnt)


def _slab(i):
    return pl.ds(pl.multiple_of(i * ROW_SLAB, ROW_SLAB), ROW_SLAB)


def _dispatch_kernel(dest_ref, xl_ref, xs_hbm, sem):
    tm = xl_ref.shape[0] // ROW_SLAB

    def copy(t, d):
        return pltpu.make_async_copy(xl_ref.at[_slab(t)], xs_hbm.at[_slab(d)], sem)

    def start(t, c):
        for s in range(TOP_K):
            copy(t, dest_ref[0, s * tm + t]).start(priority=s % 2)
        return c

    def wait(t, c):
        for s in range(TOP_K):
            copy(t, dest_ref[0, s * tm + t]).wait()
        return c

    lax.fori_loop(0, tm, start, 0, unroll=8)
    lax.fori_loop(0, tm, wait, 0, unroll=8)


def _dispatch(xl, dest, n_rows):
    N = xl.shape[0] // ROW_SLAB
    tm = TM_DISPATCH
    return pl.pallas_call(
        _dispatch_kernel,
        grid=(N // tm,),
        in_specs=[pl.BlockSpec((None, 1, TOP_K * tm), lambda i: (i, 0, 0), memory_space=pltpu.SMEM),
                  pl.BlockSpec((tm * ROW_SLAB, LANES), lambda i: (i, 0))],
        out_specs=pl.BlockSpec(memory_space=pl.ANY),
        out_shape=jax.ShapeDtypeStruct((n_rows * ROW_SLAB, LANES), jnp.uint32),
        scratch_shapes=[pltpu.SemaphoreType.DMA(())],
        compiler_params=_cp("arbitrary"),
        name="dispatch",
    )(dest, xl)


def _ffn_kernel(te_ref, tv_ref, ti_ref, tf_ref, tn_ref, xs_ref, wg_hbm, wu_hbm, wd_hbm, ys_ref,
                x_ref, stage_g, stage_u, stage_d, wg_ref, wu_ref, wd_ref, sems):
    del ti_ref
    t = pl.program_id(0)
    nv = tv_ref[t]
    tm = x_ref.shape[0]
    half = x_ref.shape[1] // 2
    stages = ((wg_hbm, stage_g, wg_ref), (wu_hbm, stage_u, wu_ref), (wd_hbm, stage_d, wd_ref))

    def fetch(e):
        return [pltpu.make_async_copy(src.at[e], stage, sems.at[k]) for k, (src, stage, _) in enumerate(stages)]

    @pl.when(t == 0)
    def _():
        for c in fetch(te_ref[0]):
            c.start()

    @pl.when(tf_ref[t] == 1)
    def _():
        for c in fetch(0):
            c.wait()
        for _, stage, dst in stages:
            rows = stage.shape[0] // FFN_CAST_CHUNKS

            def cast(i, carry, stage=stage, dst=dst, rows=rows):
                rs = pl.ds(pl.multiple_of(i * rows, rows), rows)
                dst[rs, :] = stage[rs, :].astype(BF16)
                return carry

            lax.fori_loop(0, FFN_CAST_CHUNKS, cast, 0)

        @pl.when(tn_ref[t] >= 0)
        def _():
            for c in fetch(tn_ref[t]):
                c.start()

    @pl.when(nv > 0)
    def _():
        keep = lax.broadcasted_iota(jnp.int32, (tm, 1), 0) < nv
        for j in range(ROW_SLAB):
            w = jnp.where(keep, xs_ref[pl.ds(j, tm, stride=ROW_SLAB), :], jnp.uint32(0))
            x_ref[:, j * LANES:(j + 1) * LANES] = _unpack_lo(w).astype(BF16)
            x_ref[:, half + j * LANES:half + (j + 1) * LANES] = _unpack_hi(w).astype(BF16)
        x = x_ref[...]
        g = jnp.dot(x, wg_ref[...], preferred_element_type=F32)
        u = jnp.dot(x, wu_ref[...], preferred_element_type=F32)
        hm = ((g * _sigmoid(g)) * u).astype(BF16)
        _store_slabs(ys_ref, _pack_rows(jnp.dot(hm, wd_ref[...], preferred_element_type=F32)))

    @pl.when(nv == 0)
    def _():
        ys_ref[...] = jnp.zeros_like(ys_ref)


def _ffn(xs, tile_e, tile_valid, tile_in, tile_first, tile_next, wg, wu, wd):
    D, de = wg.shape[1], wg.shape[2]
    tm = TM_FFN
    nt = tile_e.shape[0]
    hbm = pl.BlockSpec(memory_space=pl.ANY)
    grid_spec = pltpu.PrefetchScalarGridSpec(
        num_scalar_prefetch=5,
        grid=(nt,),
        in_specs=[pl.BlockSpec((tm * ROW_SLAB, LANES), lambda t, te, tv, ti, tf, tn: (ti[t], 0)), hbm, hbm, hbm],
        out_specs=pl.BlockSpec((tm * ROW_SLAB, LANES), lambda t, te, tv, ti, tf, tn: (t, 0)),
        scratch_shapes=[pltpu.VMEM((tm, D), BF16),
                        pltpu.VMEM((D, de), F32), pltpu.VMEM((D, de), F32), pltpu.VMEM((de, D), F32),
                        pltpu.VMEM((D, de), BF16), pltpu.VMEM((D, de), BF16), pltpu.VMEM((de, D), BF16),
                        pltpu.SemaphoreType.DMA((3,))],
    )
    return pl.pallas_call(
        _ffn_kernel,
        grid_spec=grid_spec,
        out_shape=jax.ShapeDtypeStruct(xs.shape, jnp.uint32),
        compiler_params=_cp("arbitrary"),
        name="ffn",
    )(tile_e, tile_valid, tile_in, tile_first, tile_next, xs, wg, wu, wd)


def _combine_kernel(dcur_ref, dnext_ref, route_ref, x1_ref, gate2_ref, g_ref, ys_hbm, o_ref, buf, y_ref, sems):
    i = pl.program_id(0)
    nsteps = pl.num_programs(0)
    tm = x1_ref.shape[0]
    half = x1_ref.shape[1] // 2

    def copy(slot, t, s, d):
        return pltpu.make_async_copy(ys_hbm.at[_slab(d)], buf.at[slot, s, _slab(t)], sems.at[slot])

    span = dcur_ref.shape[1] // TOP_K

    def start_tile(slot, dref, tile):
        base = (tile % (span // tm)) * tm

        def body(t, c):
            for s in range(TOP_K):
                copy(slot, t, s, dref[0, s * span + base + t]).start(priority=s % 2)
            return c
        lax.fori_loop(0, tm, body, 0, unroll=8)

    def wait_tile(slot):
        def body(t, c):
            for s in range(TOP_K):
                copy(slot, t, s, 0).wait()
            return c
        lax.fori_loop(0, tm, body, 0, unroll=8)

    @pl.when(i == 0)
    def _():
        start_tile(0, dcur_ref, i)

    for slot in range(2):
        @pl.when(jnp.logical_and(i + 1 < nsteps, (i + 1) % 2 == slot))
        def _():
            start_tile(slot, dnext_ref, i + 1)

    for slot in range(2):
        @pl.when(i % 2 == slot)
        def _():
            wait_tile(slot)
            w0 = route_ref[:, 2:3]
            w1 = route_ref[:, 3:4]
            for j in range(ROW_SLAB):
                a = buf[slot, 0, pl.ds(j, tm, stride=ROW_SLAB), :]
                b = buf[slot, 1, pl.ds(j, tm, stride=ROW_SLAB), :]
                y_ref[:, j * LANES:(j + 1) * LANES] = _unpack_lo(a) * w0 + _unpack_lo(b) * w1
                y_ref[:, half + j * LANES:half + (j + 1) * LANES] = _unpack_hi(a) * w0 + _unpack_hi(b) * w1

    o_ref[...] = x1_ref[...] + gate2_ref[...] * (_rms(y_ref[...]) * g_ref[...])


def _combine(ys, dest, route, x1, mod3, g_post, seq):
    N, D = x1.shape
    tm = TM_COMBINE
    tpb = seq // tm
    nsteps = N // tm
    span = dest.shape[2] // TOP_K
    per = span // tm
    assert span % tm == 0
    return pl.pallas_call(
        _combine_kernel,
        grid=(nsteps,),
        in_specs=[pl.BlockSpec((None, 1, TOP_K * span), lambda i: (i // per, 0, 0), memory_space=pltpu.SMEM),
                  pl.BlockSpec((None, 1, TOP_K * span), lambda i: (jnp.minimum(i + 1, nsteps - 1) // per, 0, 0),
                               memory_space=pltpu.SMEM),
                  pl.BlockSpec((tm, LANES), lambda i: (i, 0)),
                  pl.BlockSpec((tm, D), lambda i: (i, 0)),
                  pl.BlockSpec((None, 1, D), lambda i: (i // tpb, 0, 5)),
                  pl.BlockSpec((1, D), lambda i: (0, 0)),
                  pl.BlockSpec(memory_space=pl.ANY)],
        out_specs=pl.BlockSpec((tm, D), lambda i: (i, 0)),
        out_shape=jax.ShapeDtypeStruct((N, D), F32),
        scratch_shapes=[pltpu.VMEM((2, TOP_K, tm * ROW_SLAB, LANES), jnp.uint32), pltpu.VMEM((tm, D), F32),
                        pltpu.SemaphoreType.DMA((2,))],
        compiler_params=_cp("arbitrary"),
        name="combine",
    )(dest, dest, route, x1, mod3, g_post, ys)


def _tile_tables(cnt, n_assign, n_experts, tm):
    counts = cnt[0, :n_experts].astype(jnp.int32)
    padded = ((counts + tm - 1) // tm) * tm
    pend = jnp.cumsum(padded)
    pstart = pend - padded
    nt = n_assign // tm + n_experts
    n_used = pend[-1] // tm
    tidx = jnp.arange(nt, dtype=jnp.int32)
    tstart = tidx * tm
    te = jnp.minimum(jnp.sum(pend[None, :] <= tstart[:, None], axis=1), n_experts - 1).astype(jnp.int32)
    tv = jnp.clip(counts[te] - (tstart - pstart[te]), 0, tm).astype(jnp.int32)
    used = tidx < n_used
    last = jnp.maximum(n_used - 1, 0)
    te = jnp.where(used, te, te[last]).astype(jnp.int32)
    ti = jnp.where(used, tidx, last).astype(jnp.int32)
    first = used & ((tidx == 0) | (te != jnp.roll(te, 1)))
    nxt = lax.cummin(jnp.where(first, tidx, nt)[::-1])[::-1]
    nxt = jnp.concatenate([nxt[1:], jnp.full((1,), nt, jnp.int32)])
    tn = jnp.where(first & (nxt < nt), te[jnp.minimum(nxt, nt - 1)], -1).astype(jnp.int32)
    return te, tv, ti, first.astype(jnp.int32), tn, nt


def kernel(x, c, w_mod, b_mod, g_pre_mix, g_post_mix, w_in, g_gmlp_v, w_spatial, b_spatial, g_out_gmlp,
           g_out_attn, w_out, g_pre_ffn, g_post_ffn, w_router_group, b_router_group, w_router_expert,
           b_router_expert, w_gate, w_up, w_down):
    B, S, D = x.shape
    N = B * S
    depth = w_mod.shape[0]
    d_gmlp = g_gmlp_v.shape[1]
    d_attn = g_out_attn.shape[1]
    heads = w_spatial.shape[1]
    n_groups, epg = b_router_expert.shape[1], b_router_expert.shape[2]
    n_experts = n_groups * epg
    assert d_gmlp == d_attn and w_in.shape[2] == 2 * d_gmlp + 3 * d_attn
    assert n_groups + n_experts <= LANES

    x2 = x.reshape(N, D)
    for l in range(depth):
        mod3 = _modulation(c, w_mod[l], b_mod[l]).reshape(B, 1, w_mod.shape[2])

        pa = _inproj(x2, mod3, g_pre_mix[l][None], w_in[l].astype(BF16), g_gmlp_v[l][None], S)
        ya_n = _gmlp(pa, w_spatial[l], b_spatial[l], g_out_gmlp[l][None])
        ob = _attention(pa.reshape(B, S, pa.shape[1]), heads, d_attn, (2 * d_gmlp) // d_attn).reshape(N, d_attn)

        wr32 = jnp.concatenate([w_router_group[l],
                                jnp.transpose(w_router_expert[l], (1, 0, 2)).reshape(D, n_experts)], axis=1)
        wr32 = jnp.pad(wr32, ((0, 0), (0, LANES - wr32.shape[1])))
        wr_hi = wr32.astype(BF16)
        wr = jnp.concatenate([wr_hi, (wr32 - wr_hi.astype(F32)).astype(BF16)], axis=1)
        br = jnp.pad(jnp.concatenate([b_router_group[l], b_router_expert[l].reshape(n_experts)]),
                     (0, LANES - n_groups - n_experts))[None]

        x1, xl, route, cnt = _outproj(ya_n, ob, x2, w_out[l].astype(BF16), g_out_attn[l][None],
                                      g_post_mix[l][None], mod3, g_pre_ffn[l][None], wr, br, S, n_groups, epg)

        dest = _rank(route, cnt, TM_FFN)
        te, tv, ti, tf, tn, nt = _tile_tables(cnt, N * TOP_K, n_experts, TM_FFN)
        xs = _dispatch(xl, dest, nt * TM_FFN)
        ys = _ffn(xs, te, tv, ti, tf, tn, w_gate[l], w_up[l], w_down[l])
        x2 = _combine(ys, dest, route, x1, mod3, g_post_ffn[l][None], S)
    return x2.reshape(B, S, D)
```

```python
import functools
import math

import jax
import jax.numpy as jnp
from jax import lax
from jax.experimental import pallas as pl
from jax.experimental.pallas import tpu as pltpu

F32 = jnp.float32
BF16 = jnp.bfloat16
EPS = 1e-6
NEG = -1e30

DILATION_PATTERNS = ((128, 1), (512, 4), (2048, 16))
ATTN_BLOCK = 128
TOP_K = 2
LANES = 128

TM_INPROJ = 512
SUB_INPROJ = 256
TM_GMLP = 512
ATTN_WINDOW = ATTN_BLOCK * max(d for _, d in DILATION_PATTERNS)
ATTN_GROUP = 256
ATTN_HEADS_PER_STEP = 2
ATTN_UNROLL = 16
TM_OUTPROJ = 512
SUB_OUTPROJ = 256
TM_RANK = 512
TM_DISPATCH = TM_RANK
TM_FFN = 256
FFN_CAST_CHUNKS = 8
TM_COMBINE = 256
TN_MOD = 1024
ROW_SLAB = 8
VMEM_LIMIT = 56 * 1024 * 1024


def _cp(*dims):
    return pltpu.CompilerParams(dimension_semantics=dims, vmem_limit_bytes=VMEM_LIMIT)


def _rms(x):
    return x * lax.rsqrt(jnp.mean(x * x, axis=-1, keepdims=True) + EPS)


def _gelu_tanh(x):
    c = math.sqrt(2.0 / math.pi)
    return x * (0.5 * (1.0 + jnp.tanh(c * (x + 0.044715 * (x * x * x)))))


def _sigmoid(x):
    return 1.0 / (1.0 + jnp.exp(-x))


def _pack_rows(x):
    half = x.shape[1] // 2
    lo = pltpu.bitcast(x[:, :half].astype(BF16).astype(F32), jnp.uint32)
    hi = pltpu.bitcast(x[:, half:].astype(BF16).astype(F32), jnp.uint32)
    return lax.shift_right_logical(lo, jnp.uint32(16)) | (hi & jnp.uint32(0xFFFF0000))


def _unpack_lo(w):
    return pltpu.bitcast(lax.shift_left(w, jnp.uint32(16)), F32)


def _unpack_hi(w):
    return pltpu.bitcast(w & jnp.uint32(0xFFFF0000), F32)


def _store_slabs(ref, words, row0=0):
    rows = words.shape[0]
    for j in range(ROW_SLAB):
        ref[pl.ds(row0 * ROW_SLAB + j, rows, stride=ROW_SLAB), :] = words[:, j * LANES:(j + 1) * LANES]


def _mod_kernel(ct_ref, w_ref, b_ref, o_ref):
    ct = ct_ref[...]
    at = ct * _sigmoid(ct)
    w = w_ref[...]
    for b in range(o_ref.shape[0]):
        o_ref[b:b + 1, :] = jnp.sum(w * at[:, b:b + 1], axis=0, keepdims=True) + b_ref[...]


def _modulation(c, w_mod, b_mod):
    B, D = c.shape
    n_out = w_mod.shape[1]
    return pl.pallas_call(
        _mod_kernel,
        grid=(n_out // TN_MOD,),
        in_specs=[pl.BlockSpec((D, B), lambda j: (0, 0)),
                  pl.BlockSpec((D, TN_MOD), lambda j: (0, j)),
                  pl.BlockSpec((1, TN_MOD), lambda j: (0, j))],
        out_specs=pl.BlockSpec((B, TN_MOD), lambda j: (0, j)),
        out_shape=jax.ShapeDtypeStruct((B, n_out), F32),
        compiler_params=_cp("arbitrary"),
        name="mod",
    )(c.T, w_mod, b_mod.reshape(1, n_out))


def _inproj_kernel(x_ref, g_ref, sc_ref, sh_ref, w_ref, gv_ref, o_ref, h_ref, *, sub):
    s = pl.program_id(0)
    last = pl.num_programs(0) - 1
    tm = x_ref.shape[0]
    tn = gv_ref.shape[1]

    def normalise():
        h = (_rms(x_ref[...]) * g_ref[...]) * (1.0 + sc_ref[...]) + sh_ref[...]
        h_ref[pl.ds(pl.multiple_of((s % 2) * tm, tm), tm), :] = h.astype(BF16)

    def project():
        base = pl.multiple_of(((s + 1) % 2) * tm, tm)
        for j in range(w_ref.shape[1] // tn):
            cs = slice(j * tn, (j + 1) * tn)
            for k in range(tm // sub):
                acc = jnp.dot(h_ref[pl.ds(base + k * sub, sub), :], w_ref[:, cs], preferred_element_type=F32)
                if j == 0:
                    acc = _gelu_tanh(acc)
                elif j == 1:
                    v = _gelu_tanh(acc)
                    vc = v - jnp.mean(v, axis=-1, keepdims=True)
                    acc = vc * lax.rsqrt(jnp.mean(vc * vc, axis=-1, keepdims=True) + EPS) * gv_ref[...]
                o_ref[k * sub:(k + 1) * sub, cs] = acc.astype(BF16)

    @pl.when(s == 0)
    def _():
        normalise()

    @pl.when(jnp.logical_and(s > 0, s < last))
    def _():
        normalise()
        project()

    @pl.when(s == last)
    def _():
        project()


def _inproj(x2, mod3, g_pre, w_in_bf, g_v, seq):
    N, D = x2.shape
    d_in = w_in_bf.shape[1]
    tn = g_v.shape[1]
    tm = TM_INPROJ
    tpb = seq // tm
    nt = N // tm
    cur = lambda s: jnp.minimum(s, nt - 1)
    return pl.pallas_call(
        functools.partial(_inproj_kernel, sub=SUB_INPROJ),
        grid=(nt + 1,),
        in_specs=[pl.BlockSpec((tm, D), lambda s: (cur(s), 0)),
                  pl.BlockSpec((1, D), lambda s: (0, 0)),
                  pl.BlockSpec((None, 1, D), lambda s: (cur(s) // tpb, 0, 1)),
                  pl.BlockSpec((None, 1, D), lambda s: (cur(s) // tpb, 0, 0)),
                  pl.BlockSpec((D, d_in), lambda s: (0, 0), pipeline_mode=pl.Buffered(1)),
                  pl.BlockSpec((1, tn), lambda s: (0, 0))],
        out_specs=pl.BlockSpec((tm, d_in), lambda s: (jnp.maximum(s - 1, 0), 0)),
        out_shape=jax.ShapeDtypeStruct((N, d_in), BF16),
        scratch_shapes=[pltpu.VMEM((2 * tm, D), BF16)],
        compiler_params=_cp("arbitrary"),
        name="inproj",
    )(x2, g_pre, mod3, mod3, w_in_bf, g_v)


def _gmlp_kernel(u_ref, v_ref, w_ref, bt_ref, g_ref, o_ref, ya_ref, *, heads, chunk, hd):
    tm = u_ref.shape[0]
    row = lax.broadcasted_iota(jnp.int32, (chunk, chunk), 0)
    col = lax.broadcasted_iota(jnp.int32, (chunk, chunk), 1)
    causal = col <= row
    for h in range(heads):
        wm = jnp.where(causal, w_ref[h], 0.0).astype(BF16)
        bcol = bt_ref[:, h:h + 1]
        cs = slice(h * hd, (h + 1) * hd)
        for c in range(tm // chunk):
            rs = slice(c * chunk, (c + 1) * chunk)
            sv = jnp.dot(wm, v_ref[rs, cs], preferred_element_type=F32) + bcol
            ya_ref[rs, cs] = u_ref[rs, cs].astype(F32) * sv
    o_ref[...] = (_rms(ya_ref[...]) * g_ref[...]).astype(BF16)


def _gmlp(pa, w_spatial, b_spatial, g_out):
    N = pa.shape[0]
    heads, chunk, _ = w_spatial.shape
    dg = g_out.shape[1]
    kern = functools.partial(_gmlp_kernel, heads=heads, chunk=chunk, hd=dg // heads)
    return pl.pallas_call(
        kern,
        grid=(N // TM_GMLP,),
        in_specs=[pl.BlockSpec((TM_GMLP, dg), lambda i: (i, 0)),
                  pl.BlockSpec((TM_GMLP, dg), lambda i: (i, 1)),
                  pl.BlockSpec((heads, chunk, chunk), lambda i: (0, 0, 0)),
                  pl.BlockSpec((chunk, heads), lambda i: (0, 0)),
                  pl.BlockSpec((1, dg), lambda i: (0, 0))],
        out_specs=pl.BlockSpec((TM_GMLP, dg), lambda i: (i, 0)),
        out_shape=jax.ShapeDtypeStruct((N, dg), BF16),
        scratch_shapes=[pltpu.VMEM((TM_GMLP, dg), F32)],
        compiler_params=_cp("arbitrary"),
        name="gmlp",
    )(pa, pa, w_spatial, b_spatial.T, g_out)


def _perm_matrix(d, n):
    i = jnp.arange(n)
    src = (i % (n // d)) * d + i // (n // d)
    return (src[:, None] == jnp.arange(n)[None, :]).astype(BF16)


def _attn_kernel(q_ref, kc_ref, kp_ref, vc_ref, vp_ref, p4_ref, p16_ref, o_ref,
                 qd_ref, kd_ref, va_ref, op_ref, lp_ref, bias_ref, *, hd, dils):
    n = pl.program_id(2)
    blk = ATTN_BLOCK
    grp = ATTN_GROUP
    W = q_ref.shape[0]
    nh = q_ref.shape[1] // hd
    scale = hd ** -0.5
    c2 = scale * math.log2(math.e)
    nt = (((1,), (1,)), ((), ()))

    ri = lax.broadcasted_iota(jnp.int32, (blk, 2 * blk), 0)
    ci = lax.broadcasted_iota(jnp.int32, (blk, 2 * blk), 1)
    band = jnp.where(ci >= ri, jnp.where(ci <= ri + blk, 0.0, NEG), NEG)
    bias_ref[0] = band
    bias_ref[1] = jnp.where(n > 0, band, jnp.where(ci < blk, NEG, band))

    def put_v(rows, x):
        ones = jnp.ones((x.shape[0], hd), BF16)
        for hh in range(nh):
            parts = [x[:, hh * hd:(hh + 1) * hd] if c == hh else ones for c in range(nh)]
            va_ref[hh, rows, :] = jnp.concatenate(parts, axis=1)

    def head_blocks(items):
        cnt = len(items)
        ss, ms, ps = [None] * cnt, [None] * cnt, [None] * cnt

        def scores(i):
            lq, lk, _, _, lb, _ = items[i]
            ss[i] = lax.dot_general(lq(), lk(), nt, preferred_element_type=F32) + lb()

        def probs(i):
            ms[i] = jnp.max(ss[i], axis=-1, keepdims=True)
            ps[i] = jnp.exp2((ss[i] - ms[i]) * c2).astype(BF16)

        def output(i):
            _, _, lva, hh, _, store = items[i]
            r = jnp.dot(ps[i], lva(), preferred_element_type=F32)
            oh = (hh + 1) % nh
            l = r[:, oh * hd:(oh + 1) * hd]
            store(r[:, hh * hd:(hh + 1) * hd] / l, ms[i] * scale + jnp.log(l))

        for stage in (scores, probs, output):
            for i in range(cnt):
                stage(i)

    def run_pattern(pi, d):
        ql = W // d
        kl = ql + blk
        nb = ql // blk
        unroll = ATTN_UNROLL
        assert nb % unroll == 0 or unroll % nb == 0

        def aligned(x):
            return x if isinstance(x, int) else pl.multiple_of(x, blk)

        def loop_body(it, carry):
            items = []
            for u in range(unroll):
                if nb >= unroll:
                    bodies_per_class = nb // unroll
                    r = 0 if d == 1 else it // bodies_per_class
                    jb = u if bodies_per_class == 1 else (it % bodies_per_class) * unroll + u
                else:
                    r = it * (unroll // nb) + u // nb
                    jb = u % nb
                qrow = aligned(r * ql + jb * blk)
                krow = aligned(r * kl + jb * blk)
                first_blk = jb == 0
                orow = aligned(jb * blk) if d == 1 else jb * (blk * d) + r
                for hh in range(nh):
                    cs = slice(hh * hd, (hh + 1) * hd)

                    def store(o, lse, hh=hh, orow=orow):
                        if d == 1:
                            rows = pl.ds(orow, blk)
                        else:
                            rows = pl.ds(orow, blk, stride=d)
                        op_ref[pi, hh, rows, :] = o
                        lp_ref[pi, hh, rows, :] = lse

                    items.append((
                        lambda qrow=qrow, cs=cs: (q_ref if d == 1 else qd_ref)[pl.ds(qrow, blk), cs],
                        lambda krow=krow, cs=cs: kd_ref[pl.ds(krow, 2 * blk), cs],
                        lambda krow=krow, hh=hh: va_ref[hh, pl.ds(krow, 2 * blk), :],
                        hh,
                        lambda first_blk=first_blk: bias_ref[
                            int(first_blk) if isinstance(first_blk, bool) else jnp.where(first_blk, 1, 0)],
                        store))
            head_blocks(items)
            return carry

        lax.fori_loop(0, (d * nb) // unroll, loop_body, 0)

    def deinterleave(d, p_ref):
        pc = grp // d
        ql = W // d
        kl = ql + blk
        ng = W // grp
        pm = p_ref[...]

        def split(x):
            return jnp.dot(pm, x, preferred_element_type=F32).astype(BF16)

        for g in range(ng):
            rows = slice(g * grp, (g + 1) * grp)
            yq, yk, yv = split(q_ref[rows, :]), split(kc_ref[rows, :]), split(vc_ref[rows, :])
            for r in range(d):
                piece = slice(r * pc, (r + 1) * pc)
                qd_ref[r * ql + g * pc:r * ql + (g + 1) * pc, :] = yq[piece]
                kd_ref[r * kl + blk + g * pc:r * kl + blk + (g + 1) * pc, :] = yk[piece]
                put_v(slice(r * kl + blk + g * pc, r * kl + blk + (g + 1) * pc), yv[piece])
        g0 = ng - (blk * d) // grp
        for g in range(g0, ng):
            rows = slice(g * grp, (g + 1) * grp)
            yk, yv = split(kp_ref[rows, :]), split(vp_ref[rows, :])
            for r in range(d):
                piece = slice(r * pc, (r + 1) * pc)
                dst = r * kl + (g - g0) * pc
                kd_ref[dst:dst + pc, :] = yk[piece]
                put_v(slice(dst, dst + pc), yv[piece])

    for pi, d in enumerate(dils):
        if d == 1:
            kd_ref[0:blk, :] = kp_ref[W - blk:W, :]
            kd_ref[blk:blk + W, :] = kc_ref[...]
            put_v(slice(0, blk), vp_ref[W - blk:W, :])
            put_v(slice(blk, blk + W), vc_ref[...])
        else:
            deinterleave(d, p4_ref if d == 4 else p16_ref)
        run_pattern(pi, d)

    fin = 2 * blk

    def fin_body(c, carry):
        rows = pl.ds(pl.multiple_of(c * fin, fin), fin)
        for hh in range(nh):
            ls = [lp_ref[pi, hh, rows, :] for pi in range(len(dils))]
            mx = functools.reduce(jnp.maximum, ls)
            ws = [jnp.exp(l - mx) for l in ls]
            num = sum(w * op_ref[pi, hh, rows, :] for pi, w in enumerate(ws))
            o_ref[rows, hh * hd:(hh + 1) * hd] = num / sum(ws)
        return carry

    lax.fori_loop(0, W // fin, fin_body, 0)


def _attention(pa3, heads, d_attn, col0):
    B, S, _ = pa3.shape
    hd = d_attn // heads
    dils = tuple(d for _, d in DILATION_PATTERNS)
    assert all(w == ATTN_BLOCK * d for w, d in DILATION_PATTERNS) and dils == (1, 4, 16)
    W = ATTN_WINDOW
    lanes = ATTN_HEADS_PER_STEP * hd
    cb = d_attn // lanes
    assert S % W == 0 and W % ATTN_GROUP == 0 and hd == LANES

    def cur(c):
        return pl.BlockSpec((None, W, lanes), lambda b, hp, n: (b, n, (col0 + c) * cb + hp))

    def prv(c):
        return pl.BlockSpec((None, W, lanes), lambda b, hp, n: (b, jnp.maximum(n - 1, 0), (col0 + c) * cb + hp))

    perm = pl.BlockSpec((ATTN_GROUP, ATTN_GROUP), lambda b, hp, n: (0, 0))
    kern = functools.partial(_attn_kernel, hd=hd, dils=dils)
    npat = len(dils)
    return pl.pallas_call(
        kern,
        grid=(B, cb, S // W),
        in_specs=[cur(0), cur(1), prv(1), cur(2), prv(2), perm, perm],
        out_specs=pl.BlockSpec((None, W, lanes), lambda b, hp, n: (b, n, hp)),
        out_shape=jax.ShapeDtypeStruct((B, S, d_attn), F32),
        scratch_shapes=[pltpu.VMEM((W, lanes), BF16),
                        pltpu.VMEM((2 * W, lanes), BF16),
                        pltpu.VMEM((ATTN_HEADS_PER_STEP, 2 * W, lanes), BF16),
                        pltpu.VMEM((npat, ATTN_HEADS_PER_STEP, W, hd), F32),
                        pltpu.VMEM((npat, ATTN_HEADS_PER_STEP, W, hd), F32),
                        pltpu.VMEM((2, ATTN_BLOCK, 2 * ATTN_BLOCK), F32)],
        compiler_params=_cp("arbitrary", "arbitrary", "arbitrary"),
        name="attn",
    )(pa3, pa3, pa3, pa3, pa3, _perm_matrix(4, ATTN_GROUP), _perm_matrix(16, ATTN_GROUP))


def _outproj_kernel(ya_ref, ob_ref, x_ref, w_ref, gattn_ref, gpost_ref, gate1_ref, gpre_ref,
                    sc2_ref, sh2_ref, wr_ref, br_ref, x1_ref, xl_ref, route_ref, cnt_ref, cat_ref, h2_ref,
                    *, n_groups, epg, sub):
    dg = ya_ref.shape[1]
    half = x_ref.shape[1] // 2
    subs = [slice(k * sub, (k + 1) * sub) for k in range(ya_ref.shape[0] // sub)]

    @pl.when(pl.program_id(0) == 0)
    def _():
        cnt_ref[...] = jnp.zeros_like(cnt_ref)

    for rs in subs:
        cat_ref[rs, :dg] = ya_ref[rs, :]
        cat_ref[rs, dg:] = (_rms(ob_ref[rs, :]) * gattn_ref[...]).astype(BF16)
    ys = [jnp.dot(cat_ref[rs, :], w_ref[...], preferred_element_type=F32) for rs in subs]
    for rs, y in zip(subs, ys):
        x1 = x_ref[rs, :] + gate1_ref[...] * (_rms(y) * gpost_ref[...])
        x1_ref[rs, :] = x1
        h2_ref[rs, :] = (_rms(x1) * gpre_ref[...]) * (1.0 + sc2_ref[...]) + sh2_ref[...]

    @pl.when(pl.program_id(0) >= 0)
    def _():
        for k, rs in enumerate(subs):
            _route_rows(h2_ref[rs, :], wr_ref, br_ref, xl_ref, route_ref, cnt_ref, rs, k * sub, half, n_groups, epg)


def _route_rows(h2, wr_ref, br_ref, xl_ref, route_ref, cnt_ref, rs, row0, half, n_groups, epg):
    tm = h2.shape[0]
    hb = h2.astype(BF16)
    hb32 = hb.astype(F32)
    words = (lax.shift_right_logical(pltpu.bitcast(hb32[:, :half], jnp.uint32), jnp.uint32(16))
             | (pltpu.bitcast(hb32[:, half:], jnp.uint32) & jnp.uint32(0xFFFF0000)))
    _store_slabs(xl_ref, words, row0)

    lo = (h2 - hb32).astype(BF16)
    r = (jnp.dot(hb, wr_ref[...], preferred_element_type=F32)
         + jnp.dot(lo, wr_ref[...], preferred_element_type=F32))
    logits = r[:, :LANES] + r[:, LANES:] + br_ref[...]

    lane = lax.broadcasted_iota(jnp.int32, (tm, LANES), 1)
    lane_f = lane.astype(F32)
    big = float(LANES)
    lg = jnp.where(lane < n_groups, logits, NEG)
    mg = jnp.max(lg, axis=-1, keepdims=True)
    gi = jnp.min(jnp.where(lg == mg, lane_f, big), axis=-1, keepdims=True)
    gp = 1.0 / jnp.sum(jnp.exp(lg - mg), axis=-1, keepdims=True)
    e_lo = n_groups + gi * epg
    le = jnp.where(lane_f >= e_lo, jnp.where(lane_f < e_lo + epg, logits, NEG), NEG)
    m1 = jnp.max(le, axis=-1, keepdims=True)
    i1 = jnp.min(jnp.where(le == m1, lane_f, big), axis=-1, keepdims=True)
    le2 = jnp.where(lane_f == i1, NEG, le)
    m2 = jnp.max(le2, axis=-1, keepdims=True)
    i2 = jnp.min(jnp.where(le2 == m2, lane_f, big), axis=-1, keepdims=True)
    t = jnp.exp(m2 - m1)
    w1 = gp / (1.0 + t)
    w2 = gp * t / (1.0 + t)
    e1, e2 = i1 - n_groups, i2 - n_groups
    route_ref[rs, :] = jnp.where(lane == 0, e1,
                                 jnp.where(lane == 1, e2, jnp.where(lane == 2, w1, jnp.where(lane == 3, w2, 0.0))))
    cnt_ref[...] = cnt_ref[...] + jnp.sum(jnp.where(lane_f == e1, 1.0, jnp.where(lane_f == e2, 1.0, 0.0)),
                                          axis=0, keepdims=True)


def _outproj(ya_n, ob, x2, w_out_bf, g_attn, g_post, mod3, g_pre, wr, br, seq, n_groups, epg):
    N, D = x2.shape
    dg = ya_n.shape[1]
    tm = TM_OUTPROJ
    tpb = seq // tm
    assert D == 2 * ROW_SLAB * LANES
    row = lambda i: (i, 0)
    const = lambda i: (0, 0)
    modc = lambda k: pl.BlockSpec((None, 1, D), lambda i: (i // tpb, 0, k))
    resident = dict(pipeline_mode=pl.Buffered(1))
    kern = functools.partial(_outproj_kernel, n_groups=n_groups, epg=epg, sub=SUB_OUTPROJ)
    return pl.pallas_call(
        kern,
        grid=(N // tm,),
        in_specs=[pl.BlockSpec((tm, dg), row), pl.BlockSpec((tm, dg), row), pl.BlockSpec((tm, D), row),
                  pl.BlockSpec((D, D), const, **resident), pl.BlockSpec((1, dg), const), pl.BlockSpec((1, D), const),
                  modc(2), pl.BlockSpec((1, D), const), modc(4), modc(3),
                  pl.BlockSpec((D, 2 * LANES), const, **resident), pl.BlockSpec((1, LANES), const)],
        out_specs=[pl.BlockSpec((tm, D), row), pl.BlockSpec((tm * ROW_SLAB, LANES), row),
                   pl.BlockSpec((tm, LANES), row), pl.BlockSpec((1, LANES), const)],
        out_shape=[jax.ShapeDtypeStruct((N, D), F32), jax.ShapeDtypeStruct((N * ROW_SLAB, LANES), jnp.uint32),
                   jax.ShapeDtypeStruct((N, LANES), F32), jax.ShapeDtypeStruct((1, LANES), F32)],
        scratch_shapes=[pltpu.VMEM((tm, D), BF16), pltpu.VMEM((tm, D), F32)],
        compiler_params=_cp("arbitrary"),
        name="outproj",
    )(ya_n, ob, x2, w_out_bf, g_attn, g_post, mod3, g_pre, mod3, mod3, wr, br)


def _rank_kernel(route_ref, cnt_ref, dest_ref, carry_ref, *, tile):
    tr = route_ref.shape[0]

    @pl.when(pl.program_id(0) == 0)
    def _():
        counts = jnp.broadcast_to(cnt_ref[...], (ROW_SLAB, LANES))
        padded = jnp.floor((counts + (tile - 1)) * (1.0 / tile)) * tile
        lane8 = lax.broadcasted_iota(jnp.int32, (ROW_SLAB, LANES), 1)
        incl = padded
        k = 1
        while k < LANES:
            incl = incl + jnp.where(lane8 >= k, pltpu.roll(incl, k, axis=1), 0.0)
            k *= 2
        carry_ref[...] = (incl - padded)[0:1, :]

    lane = lax.broadcasted_iota(jnp.int32, (tr, LANES), 1)
    lane_f = lane.astype(F32)
    oh0 = lane_f == route_ref[:, 0:1]
    oh1 = lane_f == route_ref[:, 1:2]
    oh = jnp.where(oh0, 1.0, jnp.where(oh1, 1.0, 0.0))
    ri = lax.broadcasted_iota(jnp.int32, (tr, tr), 0)
    ci = lax.broadcasted_iota(jnp.int32, (tr, tr), 1)
    earlier = jnp.where(ci < ri, 1.0, 0.0).astype(BF16)
    row = jnp.dot(earlier, oh.astype(BF16), preferred_element_type=F32) + carry_ref[...]
    d0 = jnp.sum(jnp.where(oh0, row, 0.0), axis=-1, keepdims=True)
    d1 = jnp.sum(jnp.where(oh1, row, 0.0), axis=-1, keepdims=True)
    cols = jnp.where(lane == 0, d0, jnp.where(lane == 1, d1, 0.0))
    rows = cols.T
    dest_ref[...] = jnp.concatenate([rows[s:s + 1, :] for s in range(TOP_K)], axis=1).astype(jnp.int32)
    carry_ref[...] = carry_ref[...] + jnp.sum(oh, axis=0, keepdims=True)


def _rank(route, cnt, tile):
    N = route.shape[0]
    tr = TM_RANK
    return pl.pallas_call(
        functools.partial(_rank_kernel, tile=tile),
        grid=(N // tr,),
        in_specs=[pl.BlockSpec((tr, LANES), lambda i: (i, 0)), pl.BlockSpec((1, LANES), lambda i: (0, 0))],
        out_specs=pl.BlockSpec((None, 1, TOP_K * tr), lambda i: (i, 0, 0)),
        out_shape=jax.ShapeDtypeStruct((N // tr, 1, TOP_K * tr), jnp.int32),
        scratch_shapes=[pltpu.VMEM((1, LANES), F32)],
        compiler_params=_cp("arbitrary"),
        name="rank",
    )(route, cnt)


def _slab(i):
    return pl.ds(pl.multiple_of(i * ROW_SLAB, ROW_SLAB), ROW_SLAB)


def _dispatch_kernel(dest_ref, xl_ref, xs_hbm, sem):
    tm = xl_ref.shape[0] // ROW_SLAB

    def copy(t, d):
        return pltpu.make_async_copy(xl_ref.at[_slab(t)], xs_hbm.at[_slab(d)], sem)

    def start(t, c):
        for s in range(TOP_K):
            copy(t, dest_ref[0, s * tm + t]).start(priority=s % 2)
        return c

    def wait(t, c):
        for s in range(TOP_K):
            copy(t, dest_ref[0, s * tm + t]).wait()
        return c

    lax.fori_loop(0, tm, start, 0, unroll=8)
    lax.fori_loop(0, tm, wait, 0, unroll=8)


def _dispatch(xl, dest, n_rows):
    N = xl.shape[0] // ROW_SLAB
    tm = TM_DISPATCH
    return pl.pallas_call(
        _dispatch_kernel,
        grid=(N // tm,),
        in_specs=[pl.BlockSpec((None, 1, TOP_K * tm), lambda i: (i, 0, 0), memory_space=pltpu.SMEM),
                  pl.BlockSpec((tm * ROW_SLAB, LANES), lambda i: (i, 0))],
        out_specs=pl.BlockSpec(memory_space=pl.ANY),
        out_shape=jax.ShapeDtypeStruct((n_rows * ROW_SLAB, LANES), jnp.uint32),
        scratch_shapes=[pltpu.SemaphoreType.DMA(())],
        compiler_params=_cp("arbitrary"),
        name="dispatch",
    )(dest, xl)


def _ffn_kernel(te_ref, tv_ref, ti_ref, tf_ref, tn_ref, xs_ref, wg_hbm, wu_hbm, wd_hbm, ys_ref,
                x_ref, stage_g, stage_u, stage_d, wg_ref, wu_ref, wd_ref, sems):
    del ti_ref
    t = pl.program_id(0)
    nv = tv_ref[t]
    tm = x_ref.shape[0]
    half = x_ref.shape[1] // 2
    stages = ((wg_hbm, stage_g, wg_ref), (wu_hbm, stage_u, wu_ref), (wd_hbm, stage_d, wd_ref))

    def fetch(e):
        return [pltpu.make_async_copy(src.at[e], stage, sems.at[k]) for k, (src, stage, _) in enumerate(stages)]

    @pl.when(t == 0)
    def _():
        for c in fetch(te_ref[0]):
            c.start()

    @pl.when(tf_ref[t] == 1)
    def _():
        for c in fetch(0):
            c.wait()
        for _, stage, dst in stages:
            rows = stage.shape[0] // FFN_CAST_CHUNKS

            def cast(i, carry, stage=stage, dst=dst, rows=rows):
                rs = pl.ds(pl.multiple_of(i * rows, rows), rows)
                dst[rs, :] = stage[rs, :].astype(BF16)
                return carry

            lax.fori_loop(0, FFN_CAST_CHUNKS, cast, 0)

        @pl.when(tn_ref[t] >= 0)
        def _():
            for c in fetch(tn_ref[t]):
                c.start()

    @pl.when(nv > 0)
    def _():
        keep = lax.broadcasted_iota(jnp.int32, (tm, 1), 0) < nv
        for j in range(ROW_SLAB):
            w = jnp.where(keep, xs_ref[pl.ds(j, tm, stride=ROW_SLAB), :], jnp.uint32(0))
            x_ref[:, j * LANES:(j + 1) * LANES] = _unpack_lo(w).astype(BF16)
            x_ref[:, half + j * LANES:half + (j + 1) * LANES] = _unpack_hi(w).astype(BF16)
        x = x_ref[...]
        g = jnp.dot(x, wg_ref[...], preferred_element_type=F32)
        u = jnp.dot(x, wu_ref[...], preferred_element_type=F32)
        hm = ((g * _sigmoid(g)) * u).astype(BF16)
        _store_slabs(ys_ref, _pack_rows(jnp.dot(hm, wd_ref[...], preferred_element_type=F32)))

    @pl.when(nv == 0)
    def _():
        ys_ref[...] = jnp.zeros_like(ys_ref)


def _ffn(xs, tile_e, tile_valid, tile_in, tile_first, tile_next, wg, wu, wd):
    D, de = wg.shape[1], wg.shape[2]
    tm = TM_FFN
    nt = tile_e.shape[0]
    hbm = pl.BlockSpec(memory_space=pl.ANY)
    grid_spec = pltpu.PrefetchScalarGridSpec(
        num_scalar_prefetch=5,
        grid=(nt,),
        in_specs=[pl.BlockSpec((tm * ROW_SLAB, LANES), lambda t, te, tv, ti, tf, tn: (ti[t], 0)), hbm, hbm, hbm],
        out_specs=pl.BlockSpec((tm * ROW_SLAB, LANES), lambda t, te, tv, ti, tf, tn: (t, 0)),
        scratch_shapes=[pltpu.VMEM((tm, D), BF16),
                        pltpu.VMEM((D, de), F32), pltpu.VMEM((D, de), F32), pltpu.VMEM((de, D), F32),
                        pltpu.VMEM((D, de), BF16), pltpu.VMEM((D, de), BF16), pltpu.VMEM((de, D), BF16),
                        pltpu.SemaphoreType.DMA((3,))],
    )
    return pl.pallas_call(
        _ffn_kernel,
        grid_spec=grid_spec,
        out_shape=jax.ShapeDtypeStruct(xs.shape, jnp.uint32),
        compiler_params=_cp("arbitrary"),
        name="ffn",
    )(tile_e, tile_valid, tile_in, tile_first, tile_next, xs, wg, wu, wd)


def _combine_kernel(dcur_ref, dnext_ref, route_ref, x1_ref, gate2_ref, g_ref, ys_hbm, o_ref, buf, y_ref, sems):
    i = pl.program_id(0)
    nsteps = pl.num_programs(0)
    tm = x1_ref.shape[0]
    half = x1_ref.shape[1] // 2

    def copy(slot, t, s, d):
        return pltpu.make_async_copy(ys_hbm.at[_slab(d)], buf.at[slot, s, _slab(t)], sems.at[slot])

    span = dcur_ref.shape[1] // TOP_K

    def start_tile(slot, dref, tile):
        base = (tile % (span // tm)) * tm

        def body(t, c):
            for s in range(TOP_K):
                copy(slot, t, s, dref[0, s * span + base + t]).start(priority=s % 2)
            return c
        lax.fori_loop(0, tm, body, 0, unroll=8)

    def wait_tile(slot):
        def body(t, c):
            for s in range(TOP_K):
                copy(slot, t, s, 0).wait()
            return c
        lax.fori_loop(0, tm, body, 0, unroll=8)

    @pl.when(i == 0)
    def _():
        start_tile(0, dcur_ref, i)

    for slot in range(2):
        @pl.when(jnp.logical_and(i + 1 < nsteps, (i + 1) % 2 == slot))
        def _():
            start_tile(slot, dnext_ref, i + 1)

    for slot in range(2):
        @pl.when(i % 2 == slot)
        def _():
            wait_tile(slot)
            w0 = route_ref[:, 2:3]
            w1 = route_ref[:, 3:4]
            for j in range(ROW_SLAB):
                a = buf[slot, 0, pl.ds(j, tm, stride=ROW_SLAB), :]
                b = buf[slot, 1, pl.ds(j, tm, stride=ROW_SLAB), :]
                y_ref[:, j * LANES:(j + 1) * LANES] = _unpack_lo(a) * w0 + _unpack_lo(b) * w1
                y_ref[:, half + j * LANES:half + (j + 1) * LANES] = _unpack_hi(a) * w0 + _unpack_hi(b) * w1

    o_ref[...] = x1_ref[...] + gate2_ref[...] * (_rms(y_ref[...]) * g_ref[...])


def _combine(ys, dest, route, x1, mod3, g_post, seq):
    N, D = x1.shape
    tm = TM_COMBINE
    tpb = seq // tm
    nsteps = N // tm
    span = dest.shape[2] // TOP_K
    per = span // tm
    assert span % tm == 0
    return pl.pallas_call(
        _combine_kernel,
        grid=(nsteps,),
        in_specs=[pl.BlockSpec((None, 1, TOP_K * span), lambda i: (i // per, 0, 0), memory_space=pltpu.SMEM),
                  pl.BlockSpec((None, 1, TOP_K * span), lambda i: (jnp.minimum(i + 1, nsteps - 1) // per, 0, 0),
                               memory_space=pltpu.SMEM),
                  pl.BlockSpec((tm, LANES), lambda i: (i, 0)),
                  pl.BlockSpec((tm, D), lambda i: (i, 0)),
                  pl.BlockSpec((None, 1, D), lambda i: (i // tpb, 0, 5)),
                  pl.BlockSpec((1, D), lambda i: (0, 0)),
                  pl.BlockSpec(memory_space=pl.ANY)],
        out_specs=pl.BlockSpec((tm, D), lambda i: (i, 0)),
        out_shape=jax.ShapeDtypeStruct((N, D), F32),
        scratch_shapes=[pltpu.VMEM((2, TOP_K, tm * ROW_SLAB, LANES), jnp.uint32), pltpu.VMEM((tm, D), F32),
                        pltpu.SemaphoreType.DMA((2,))],
        compiler_params=_cp("arbitrary"),
        name="combine",
    )(dest, dest, route, x1, mod3, g_post, ys)


def _tile_tables(cnt, n_assign, n_experts, tm):
    counts = cnt[0, :n_experts].astype(jnp.int32)
    padded = ((counts + tm - 1) // tm) * tm
    pend = jnp.cumsum(padded)
    pstart = pend - padded
    nt = n_assign // tm + n_experts
    n_used = pend[-1] // tm
    tidx = jnp.arange(nt, dtype=jnp.int32)
    tstart = tidx * tm
    te = jnp.minimum(jnp.sum(pend[None, :] <= tstart[:, None], axis=1), n_experts - 1).astype(jnp.int32)
    tv = jnp.clip(counts[te] - (tstart - pstart[te]), 0, tm).astype(jnp.int32)
    used = tidx < n_used
    last = jnp.maximum(n_used - 1, 0)
    te = jnp.where(used, te, te[last]).astype(jnp.int32)
    ti = jnp.where(used, tidx, last).astype(jnp.int32)
    first = used & ((tidx == 0) | (te != jnp.roll(te, 1)))
    nxt = lax.cummin(jnp.where(first, tidx, nt)[::-1])[::-1]
    nxt = jnp.concatenate([nxt[1:], jnp.full((1,), nt, jnp.int32)])
    tn = jnp.where(first & (nxt < nt), te[jnp.minimum(nxt, nt - 1)], -1).astype(jnp.int32)
    return te, tv, ti, first.astype(jnp.int32), tn, nt


def kernel(x, c, w_mod, b_mod, g_pre_mix, g_post_mix, w_in, g_gmlp_v, w_spatial, b_spatial, g_out_gmlp,
           g_out_attn, w_out, g_pre_ffn, g_post_ffn, w_router_group, b_router_group, w_router_expert,
           b_router_expert, w_gate, w_up, w_down):
    B, S, D = x.shape
    N = B * S
    depth = w_mod.shape[0]
    d_gmlp = g_gmlp_v.shape[1]
    d_attn = g_out_attn.shape[1]
    heads = w_spatial.shape[1]
    n_groups, epg = b_router_expert.shape[1], b_router_expert.shape[2]
    n_experts = n_groups * epg
    assert d_gmlp == d_attn and w_in.shape[2] == 2 * d_gmlp + 3 * d_attn
    assert n_groups + n_experts <= LANES

    x2 = x.reshape(N, D)
    for l in range(depth):
        mod3 = _modulation(c, w_mod[l], b_mod[l]).reshape(B, 1, w_mod.shape[2])

        pa = _inproj(x2, mod3, g_pre_mix[l][None], w_in[l].astype(BF16), g_gmlp_v[l][None], S)
        ya_n = _gmlp(pa, w_spatial[l], b_spatial[l], g_out_gmlp[l][None])
        ob = _attention(pa.reshape(B, S, pa.shape[1]), heads, d_attn, (2 * d_gmlp) // d_attn).reshape(N, d_attn)

        wr32 = jnp.concatenate([w_router_group[l],
                                jnp.transpose(w_router_expert[l], (1, 0, 2)).reshape(D, n_experts)], axis=1)
        wr32 = jnp.pad(wr32, ((0, 0), (0, LANES - wr32.shape[1])))
        wr_hi = wr32.astype(BF16)
        wr = jnp.concatenate([wr_hi, (wr32 - wr_hi.astype(F32)).astype(BF16)], axis=1)
        br = jnp.pad(jnp.concatenate([b_router_group[l], b_router_expert[l].reshape(n_experts)]),
                     (0, LANES - n_groups - n_experts))[None]

        x1, xl, route, cnt = _outproj(ya_n, ob, x2, w_out[l].astype(BF16), g_out_attn[l][None],
                                      g_post_mix[l][None], mod3, g_pre_ffn[l][None], wr, br, S, n_groups, epg)

        dest = _rank(route, cnt, TM_FFN)
        te, tv, ti, tf, tn, nt = _tile_tables(cnt, N * TOP_K, n_experts, TM_FFN)
        xs = _dispatch(xl, dest, nt * TM_FFN)
        ys = _ffn(xs, te, tv, ti, tf, tn, w_gate[l], w_up[l], w_down[l])
        x2 = _combine(ys, dest, route, x1, mod3, g_post_ffn[l][None], S)
    return x2.reshape(B, S, D)
```

```python
import functools
import math

import jax
import jax.numpy as jnp
from jax import lax
from jax.experimental import pallas as pl
from jax.experimental.pallas import tpu as pltpu

F32 = jnp.float32
BF16 = jnp.bfloat16
EPS = 1e-6
NEG = -1e30

DILATION_PATTERNS = ((128, 1), (512, 4), (2048, 16))
ATTN_BLOCK = 128
TOP_K = 2
LANES = 128

TM_INPROJ = 512
SUB_INPROJ = 256
TM_GMLP = 512
ATTN_WINDOW = ATTN_BLOCK * max(d for _, d in DILATION_PATTERNS)
ATTN_GROUP = 256
ATTN_HEADS_PER_STEP = 2
ATTN_UNROLL = 16
TM_OUTPROJ = 512
SUB_OUTPROJ = 256
TM_RANK = 512
TM_DISPATCH = TM_RANK
TM_FFN = 256
FFN_CAST_CHUNKS = 8
TM_COMBINE = 256
TN_MOD = 1024
ROW_SLAB = 8
VMEM_LIMIT = 56 * 1024 * 1024


def _cp(*dims):
    return pltpu.CompilerParams(dimension_semantics=dims, vmem_limit_bytes=VMEM_LIMIT)


def _rms(x):
    return x * lax.rsqrt(jnp.mean(x * x, axis=-1, keepdims=True) + EPS)


def _gelu_tanh(x):
    c = math.sqrt(2.0 / math.pi)
    return x * (0.5 * (1.0 + jnp.tanh(c * (x + 0.044715 * (x * x * x)))))


def _sigmoid(x):
    return 1.0 / (1.0 + jnp.exp(-x))


def _pack_rows(x):
    half = x.shape[1] // 2
    lo = pltpu.bitcast(x[:, :half].astype(BF16).astype(F32), jnp.uint32)
    hi = pltpu.bitcast(x[:, half:].astype(BF16).astype(F32), jnp.uint32)
    return lax.shift_right_logical(lo, jnp.uint32(16)) | (hi & jnp.uint32(0xFFFF0000))


def _unpack_lo(w):
    return pltpu.bitcast(lax.shift_left(w, jnp.uint32(16)), F32)


def _unpack_hi(w):
    return pltpu.bitcast(w & jnp.uint32(0xFFFF0000), F32)


def _store_slabs(ref, words, row0=0):
    rows = words.shape[0]
    for j in range(ROW_SLAB):
        ref[pl.ds(row0 * ROW_SLAB + j, rows, stride=ROW_SLAB), :] = words[:, j * LANES:(j + 1) * LANES]


def _mod_kernel(ct_ref, w_ref, b_ref, o_ref):
    ct = ct_ref[...]
    at = ct * _sigmoid(ct)
    w = w_ref[...]
    for b in range(o_ref.shape[0]):
        o_ref[b:b + 1, :] = jnp.sum(w * at[:, b:b + 1], axis=0, keepdims=True) + b_ref[...]


def _modulation(c, w_mod, b_mod):
    B, D = c.shape
    n_out = w_mod.shape[1]
    return pl.pallas_call(
        _mod_kernel,
        grid=(n_out // TN_MOD,),
        in_specs=[pl.BlockSpec((D, B), lambda j: (0, 0)),
                  pl.BlockSpec((D, TN_MOD), lambda j: (0, j)),
                  pl.BlockSpec((1, TN_MOD), lambda j: (0, j))],
        out_specs=pl.BlockSpec((B, TN_MOD), lambda j: (0, j)),
        out_shape=jax.ShapeDtypeStruct((B, n_out), F32),
        compiler_params=_cp("arbitrary"),
        name="mod",
    )(c.T, w_mod, b_mod.reshape(1, n_out))


def _inproj_kernel(x_ref, g_ref, sc_ref, sh_ref, w_ref, gv_ref, o_ref, h_ref, *, sub):
    s = pl.program_id(0)
    last = pl.num_programs(0) - 1
    tm = x_ref.shape[0]
    tn = gv_ref.shape[1]

    def normalise():
        h = (_rms(x_ref[...]) * g_ref[...]) * (1.0 + sc_ref[...]) + sh_ref[...]
        h_ref[pl.ds(pl.multiple_of((s % 2) * tm, tm), tm), :] = h.astype(BF16)

    def project():
        base = pl.multiple_of(((s + 1) % 2) * tm, tm)
        for j in range(w_ref.shape[1] // tn):
            cs = slice(j * tn, (j + 1) * tn)
            for k in range(tm // sub):
                acc = jnp.dot(h_ref[pl.ds(base + k * sub, sub), :], w_ref[:, cs], preferred_element_type=F32)
                if j == 0:
                    acc = _gelu_tanh(acc)
                elif j == 1:
                    v = _gelu_tanh(acc)
                    vc = v - jnp.mean(v, axis=-1, keepdims=True)
                    acc = vc * lax.rsqrt(jnp.mean(vc * vc, axis=-1, keepdims=True) + EPS) * gv_ref[...]
                o_ref[k * sub:(k + 1) * sub, cs] = acc.astype(BF16)

    @pl.when(s == 0)
    def _():
        normalise()

    @pl.when(jnp.logical_and(s > 0, s < last))
    def _():
        normalise()
        project()

    @pl.when(s == last)
    def _():
        project()


def _inproj(x2, mod3, g_pre, w_in_bf, g_v, seq):
    N, D = x2.shape
    d_in = w_in_bf.shape[1]
    tn = g_v.shape[1]
    tm = TM_INPROJ
    tpb = seq // tm
    nt = N // tm
    cur = lambda s: jnp.minimum(s, nt - 1)
    return pl.pallas_call(
        functools.partial(_inproj_kernel, sub=SUB_INPROJ),
        grid=(nt + 1,),
        in_specs=[pl.BlockSpec((tm, D), lambda s: (cur(s), 0)),
                  pl.BlockSpec((1, D), lambda s: (0, 0)),
                  pl.BlockSpec((None, 1, D), lambda s: (cur(s) // tpb, 0, 1)),
                  pl.BlockSpec((None, 1, D), lambda s: (cur(s) // tpb, 0, 0)),
                  pl.BlockSpec((D, d_in), lambda s: (0, 0), pipeline_mode=pl.Buffered(1)),
                  pl.BlockSpec((1, tn), lambda s: (0, 0))],
        out_specs=pl.BlockSpec((tm, d_in), lambda s: (jnp.maximum(s - 1, 0), 0)),
        out_shape=jax.ShapeDtypeStruct((N, d_in), BF16),
        scratch_shapes=[pltpu.VMEM((2 * tm, D), BF16)],
        compiler_params=_cp("arbitrary"),
        name="inproj",
    )(x2, g_pre, mod3, mod3, w_in_bf, g_v)


def _gmlp_kernel(u_ref, v_ref, w_ref, bt_ref, g_ref, o_ref, ya_ref, *, heads, chunk, hd):
    tm = u_ref.shape[0]
    row = lax.broadcasted_iota(jnp.int32, (chunk, chunk), 0)
    col = lax.broadcasted_iota(jnp.int32, (chunk, chunk), 1)
    causal = col <= row
    for h in range(heads):
        wm = jnp.where(causal, w_ref[h], 0.0).astype(BF16)
        bcol = bt_ref[:, h:h + 1]
        cs = slice(h * hd, (h + 1) * hd)
        for c in range(tm // chunk):
            rs = slice(c * chunk, (c + 1) * chunk)
            sv = jnp.dot(wm, v_ref[rs, cs], preferred_element_type=F32) + bcol
            ya_ref[rs, cs] = u_ref[rs, cs].astype(F32) * sv
    o_ref[...] = (_rms(ya_ref[...]) * g_ref[...]).astype(BF16)


def _gmlp(pa, w_spatial, b_spatial, g_out):
    N = pa.shape[0]
    heads, chunk, _ = w_spatial.shape
    dg = g_out.shape[1]
    kern = functools.partial(_gmlp_kernel, heads=heads, chunk=chunk, hd=dg // heads)
    return pl.pallas_call(
        kern,
        grid=(N // TM_GMLP,),
        in_specs=[pl.BlockSpec((TM_GMLP, dg), lambda i: (i, 0)),
                  pl.BlockSpec((TM_GMLP, dg), lambda i: (i, 1)),
                  pl.BlockSpec((heads, chunk, chunk), lambda i: (0, 0, 0)),
                  pl.BlockSpec((chunk, heads), lambda i: (0, 0)),
                  pl.BlockSpec((1, dg), lambda i: (0, 0))],
        out_specs=pl.BlockSpec((TM_GMLP, dg), lambda i: (i, 0)),
        out_shape=jax.ShapeDtypeStruct((N, dg), BF16),
        scratch_shapes=[pltpu.VMEM((TM_GMLP, dg), F32)],
        compiler_params=_cp("arbitrary"),
        name="gmlp",
    )(pa, pa, w_spatial, b_spatial.T, g_out)


def _perm_matrix(d, n):
    i = jnp.arange(n)
    src = (i % (n // d)) * d + i // (n // d)
    return (src[:, None] == jnp.arange(n)[None, :]).astype(BF16)


def _attn_kernel(q_ref, kc_ref, vc_ref, p4_ref, p16_ref, o_ref,
                 qd_ref, kd_ref, va_ref, op_ref, lp_ref, bias_ref, ck_ref, cv_ref, *, hd, dils):
    n = pl.program_id(2)
    blk = ATTN_BLOCK
    grp = ATTN_GROUP
    W = q_ref.shape[0]
    nh = q_ref.shape[1] // hd
    scale = hd ** -0.5
    c2 = scale * math.log2(math.e)
    nt = (((1,), (1,)), ((), ()))

    ri = lax.broadcasted_iota(jnp.int32, (blk, 2 * blk), 0)
    ci = lax.broadcasted_iota(jnp.int32, (blk, 2 * blk), 1)
    band = jnp.where(ci >= ri, jnp.where(ci <= ri + blk, 0.0, NEG), NEG)
    bias_ref[0] = band
    bias_ref[1] = jnp.where(n > 0, band, jnp.where(ci < blk, NEG, band))

    def put_v(rows, x):
        ones = jnp.ones((x.shape[0], hd), BF16)
        for hh in range(nh):
            parts = [x[:, hh * hd:(hh + 1) * hd] if c == hh else ones for c in range(nh)]
            va_ref[hh, rows, :] = jnp.concatenate(parts, axis=1)

    def head_blocks(items):
        cnt = len(items)
        ss, ms, ps = [None] * cnt, [None] * cnt, [None] * cnt

        def scores(i):
            lq, lk, _, _, lb, _ = items[i]
            ss[i] = lax.dot_general(lq(), lk(), nt, preferred_element_type=F32) + lb()

        def probs(i):
            ms[i] = jnp.max(ss[i], axis=-1, keepdims=True)
            ps[i] = jnp.exp2((ss[i] - ms[i]) * c2).astype(BF16)

        def output(i):
            _, _, lva, hh, _, store = items[i]
            r = jnp.dot(ps[i], lva(), preferred_element_type=F32)
            oh = (hh + 1) % nh
            l = r[:, oh * hd:(oh + 1) * hd]
            store(r[:, hh * hd:(hh + 1) * hd] / l, ms[i] * scale + jnp.log(l))

        for stage in (scores, probs, output):
            for i in range(cnt):
                stage(i)

    def run_pattern(pi, d):
        ql = W // d
        kl = ql + blk
        nb = ql // blk
        unroll = ATTN_UNROLL
        assert nb % unroll == 0 or unroll % nb == 0

        def aligned(x):
            return x if isinstance(x, int) else pl.multiple_of(x, blk)

        def loop_body(it, carry):
            items = []
            for u in range(unroll):
                if nb >= unroll:
                    bodies_per_class = nb // unroll
                    r = 0 if d == 1 else it // bodies_per_class
                    jb = u if bodies_per_class == 1 else (it % bodies_per_class) * unroll + u
                else:
                    r = it * (unroll // nb) + u // nb
                    jb = u % nb
                qrow = aligned(r * ql + jb * blk)
                krow = aligned(r * kl + jb * blk)
                first_blk = jb == 0
                orow = aligned(jb * blk) if d == 1 else jb * (blk * d) + r
                for hh in range(nh):
                    cs = slice(hh * hd, (hh + 1) * hd)

                    def store(o, lse, hh=hh, orow=orow):
                        if d == 1:
                            rows = pl.ds(orow, blk)
                        else:
                            rows = pl.ds(orow, blk, stride=d)
                        op_ref[pi, hh, rows, :] = o
                        lp_ref[pi, hh, rows, :] = lse

                    items.append((
                        lambda qrow=qrow, cs=cs: (q_ref if d == 1 else qd_ref)[pl.ds(qrow, blk), cs],
                        lambda krow=krow, cs=cs: kd_ref[pl.ds(krow, 2 * blk), cs],
                        lambda krow=krow, hh=hh: va_ref[hh, pl.ds(krow, 2 * blk), :],
                        hh,
                        lambda first_blk=first_blk: bias_ref[
                            int(first_blk) if isinstance(first_blk, bool) else jnp.where(first_blk, 1, 0)],
                        store))
            head_blocks(items)
            return carry

        lax.fori_loop(0, (d * nb) // unroll, loop_body, 0)

    def deinterleave(d, p_ref):
        pc = grp // d
        ql = W // d
        kl = ql + blk
        ng = W // grp
        pm = p_ref[...]

        def split(x):
            return jnp.dot(pm, x, preferred_element_type=F32).astype(BF16)

        for g in range(ng):
            rows = slice(g * grp, (g + 1) * grp)
            yq, yk, yv = split(q_ref[rows, :]), split(kc_ref[rows, :]), split(vc_ref[rows, :])
            for r in range(d):
                piece = slice(r * pc, (r + 1) * pc)
                qd_ref[r * ql + g * pc:r * ql + (g + 1) * pc, :] = yq[piece]
                kd_ref[r * kl + blk + g * pc:r * kl + blk + (g + 1) * pc, :] = yk[piece]
                put_v(slice(r * kl + blk + g * pc, r * kl + blk + (g + 1) * pc), yv[piece])

    @pl.when(n == 0)
    def _():
        ck_ref[...] = jnp.zeros_like(ck_ref)
        cv_ref[...] = jnp.zeros_like(cv_ref)

    off = 0
    for pi, d in enumerate(dils):
        kl = W // d + blk
        for r in range(d):
            src = slice(off + r * blk, off + (r + 1) * blk)
            dst = slice(r * kl, r * kl + blk)
            kd_ref[dst, :] = ck_ref[src, :]
            for hh in range(nh):
                va_ref[hh, dst, :] = cv_ref[hh, src, :]
        if d == 1:
            kd_ref[blk:blk + W, :] = kc_ref[...]
            put_v(slice(blk, blk + W), vc_ref[...])
        else:
            deinterleave(d, p4_ref if d == 4 else p16_ref)
        for r in range(d):
            src = slice((r + 1) * kl - blk, (r + 1) * kl)
            dst = slice(off + r * blk, off + (r + 1) * blk)
            ck_ref[dst, :] = kd_ref[src, :]
            for hh in range(nh):
                cv_ref[hh, dst, :] = va_ref[hh, src, :]
        off += d * blk
        run_pattern(pi, d)

    fin = 2 * blk

    def fin_body(c, carry):
        rows = pl.ds(pl.multiple_of(c * fin, fin), fin)
        for hh in range(nh):
            ls = [lp_ref[pi, hh, rows, :] for pi in range(len(dils))]
            mx = functools.reduce(jnp.maximum, ls)
            ws = [jnp.exp(l - mx) for l in ls]
            num = sum(w * op_ref[pi, hh, rows, :] for pi, w in enumerate(ws))
            o_ref[rows, hh * hd:(hh + 1) * hd] = num / sum(ws)
        return carry

    lax.fori_loop(0, W // fin, fin_body, 0)


def _attention(pa3, heads, d_attn, col0):
    B, S, _ = pa3.shape
    hd = d_attn // heads
    dils = tuple(d for _, d in DILATION_PATTERNS)
    assert all(w == ATTN_BLOCK * d for w, d in DILATION_PATTERNS) and dils == (1, 4, 16)
    W = ATTN_WINDOW
    lanes = ATTN_HEADS_PER_STEP * hd
    cb = d_attn // lanes
    assert S % W == 0 and W % ATTN_GROUP == 0 and hd == LANES

    def cur(c):
        return pl.BlockSpec((None, W, lanes), lambda b, hp, n: (b, n, (col0 + c) * cb + hp))

    perm = pl.BlockSpec((ATTN_GROUP, ATTN_GROUP), lambda b, hp, n: (0, 0))
    kern = functools.partial(_attn_kernel, hd=hd, dils=dils)
    npat = len(dils)
    return pl.pallas_call(
        kern,
        grid=(B, cb, S // W),
        in_specs=[cur(0), cur(1), cur(2), perm, perm],
        out_specs=pl.BlockSpec((None, W, lanes), lambda b, hp, n: (b, n, hp)),
        out_shape=jax.ShapeDtypeStruct((B, S, d_attn), F32),
        scratch_shapes=[pltpu.VMEM((W, lanes), BF16),
                        pltpu.VMEM((2 * W, lanes), BF16),
                        pltpu.VMEM((ATTN_HEADS_PER_STEP, 2 * W, lanes), BF16),
                        pltpu.VMEM((npat, ATTN_HEADS_PER_STEP, W, hd), F32),
                        pltpu.VMEM((npat, ATTN_HEADS_PER_STEP, W, hd), F32),
                        pltpu.VMEM((2, ATTN_BLOCK, 2 * ATTN_BLOCK), F32),
                        pltpu.VMEM((sum(dils) * ATTN_BLOCK, lanes), BF16),
                        pltpu.VMEM((ATTN_HEADS_PER_STEP, sum(dils) * ATTN_BLOCK, lanes), BF16)],
        compiler_params=_cp("arbitrary", "arbitrary", "arbitrary"),
        name="attn",
    )(pa3, pa3, pa3, _perm_matrix(4, ATTN_GROUP), _perm_matrix(16, ATTN_GROUP))


def _outproj_kernel(ya_ref, ob_ref, x_ref, w_ref, gattn_ref, gpost_ref, gate1_ref, gpre_ref,
                    sc2_ref, sh2_ref, wr_ref, br_ref, x1_ref, xl_ref, route_ref, cnt_ref, cat_ref, h2_ref,
                    *, n_groups, epg, sub):
    dg = ya_ref.shape[1]
    half = x_ref.shape[1] // 2
    subs = [slice(k * sub, (k + 1) * sub) for k in range(ya_ref.shape[0] // sub)]

    @pl.when(pl.program_id(0) == 0)
    def _():
        cnt_ref[...] = jnp.zeros_like(cnt_ref)

    for rs in subs:
        cat_ref[rs, :dg] = ya_ref[rs, :]
        cat_ref[rs, dg:] = (_rms(ob_ref[rs, :]) * gattn_ref[...]).astype(BF16)
    ys = [jnp.dot(cat_ref[rs, :], w_ref[...], preferred_element_type=F32) for rs in subs]
    for rs, y in zip(subs, ys):
        x1 = x_ref[rs, :] + gate1_ref[...] * (_rms(y) * gpost_ref[...])
        x1_ref[rs, :] = x1
        h2_ref[rs, :] = (_rms(x1) * gpre_ref[...]) * (1.0 + sc2_ref[...]) + sh2_ref[...]

    @pl.when(pl.program_id(0) >= 0)
    def _():
        for k, rs in enumerate(subs):
            _route_rows(h2_ref[rs, :], wr_ref, br_ref, xl_ref, route_ref, cnt_ref, rs, k * sub, half, n_groups, epg)


def _route_rows(h2, wr_ref, br_ref, xl_ref, route_ref, cnt_ref, rs, row0, half, n_groups, epg):
    tm = h2.shape[0]
    hb = h2.astype(BF16)
    hb32 = hb.astype(F32)
    words = (lax.shift_right_logical(pltpu.bitcast(hb32[:, :half], jnp.uint32), jnp.uint32(16))
             | (pltpu.bitcast(hb32[:, half:], jnp.uint32) & jnp.uint32(0xFFFF0000)))
    _store_slabs(xl_ref, words, row0)

    lo = (h2 - hb32).astype(BF16)
    r = (jnp.dot(hb, wr_ref[...], preferred_element_type=F32)
         + jnp.dot(lo, wr_ref[...], preferred_element_type=F32))
    logits = r[:, :LANES] + r[:, LANES:] + br_ref[...]

    lane = lax.broadcasted_iota(jnp.int32, (tm, LANES), 1)
    lane_f = lane.astype(F32)
    big = float(LANES)
    lg = jnp.where(lane < n_groups, logits, NEG)
    mg = jnp.max(lg, axis=-1, keepdims=True)
    gi = jnp.min(jnp.where(lg == mg, lane_f, big), axis=-1, keepdims=True)
    gp = 1.0 / jnp.sum(jnp.exp(lg - mg), axis=-1, keepdims=True)
    e_lo = n_groups + gi * epg
    le = jnp.where(lane_f >= e_lo, jnp.where(lane_f < e_lo + epg, logits, NEG), NEG)
    m1 = jnp.max(le, axis=-1, keepdims=True)
    i1 = jnp.min(jnp.where(le == m1, lane_f, big), axis=-1, keepdims=True)
    le2 = jnp.where(lane_f == i1, NEG, le)
    m2 = jnp.max(le2, axis=-1, keepdims=True)
    i2 = jnp.min(jnp.where(le2 == m2, lane_f, big), axis=-1, keepdims=True)
    t = jnp.exp(m2 - m1)
    w1 = gp / (1.0 + t)
    w2 = gp * t / (1.0 + t)
    e1, e2 = i1 - n_groups, i2 - n_groups
    route_ref[rs, :] = jnp.where(lane == 0, e1,
                                 jnp.where(lane == 1, e2, jnp.where(lane == 2, w1, jnp.where(lane == 3, w2, 0.0))))
    cnt_ref[...] = cnt_ref[...] + jnp.sum(jnp.where(lane_f == e1, 1.0, jnp.where(lane_f == e2, 1.0, 0.0)),
                                          axis=0, keepdims=True)


def _outproj(ya_n, ob, x2, w_out_bf, g_attn, g_post, mod3, g_pre, wr, br, seq, n_groups, epg):
    N, D = x2.shape
    dg = ya_n.shape[1]
    tm = TM_OUTPROJ
    tpb = seq // tm
    assert D == 2 * ROW_SLAB * LANES
    row = lambda i: (i, 0)
    const = lambda i: (0, 0)
    modc = lambda k: pl.BlockSpec((None, 1, D), lambda i: (i // tpb, 0, k))
    resident = dict(pipeline_mode=pl.Buffered(1))
    kern = functools.partial(_outproj_kernel, n_groups=n_groups, epg=epg, sub=SUB_OUTPROJ)
    return pl.pallas_call(
        kern,
        grid=(N // tm,),
        in_specs=[pl.BlockSpec((tm, dg), row), pl.BlockSpec((tm, dg), row), pl.BlockSpec((tm, D), row),
                  pl.BlockSpec((D, D), const, **resident), pl.BlockSpec((1, dg), const), pl.BlockSpec((1, D), const),
                  modc(2), pl.BlockSpec((1, D), const), modc(4), modc(3),
                  pl.BlockSpec((D, 2 * LANES), const, **resident), pl.BlockSpec((1, LANES), const)],
        out_specs=[pl.BlockSpec((tm, D), row), pl.BlockSpec((tm * ROW_SLAB, LANES), row),
                   pl.BlockSpec((tm, LANES), row), pl.BlockSpec((1, LANES), const)],
        out_shape=[jax.ShapeDtypeStruct((N, D), F32), jax.ShapeDtypeStruct((N * ROW_SLAB, LANES), jnp.uint32),
                   jax.ShapeDtypeStruct((N, LANES), F32), jax.ShapeDtypeStruct((1, LANES), F32)],
        scratch_shapes=[pltpu.VMEM((tm, D), BF16), pltpu.VMEM((tm, D), F32)],
        compiler_params=_cp("arbitrary"),
        name="outproj",
    )(ya_n, ob, x2, w_out_bf, g_attn, g_post, mod3, g_pre, mod3, mod3, wr, br)


def _rank_kernel(route_ref, cnt_ref, dest_ref, carry_ref, *, tile):
    tr = route_ref.shape[0]

    @pl.when(pl.program_id(0) == 0)
    def _():
        counts = jnp.broadcast_to(cnt_ref[...], (ROW_SLAB, LANES))
        padded = jnp.floor((counts + (tile - 1)) * (1.0 / tile)) * tile
        lane8 = lax.broadcasted_iota(jnp.int32, (ROW_SLAB, LANES), 1)
        incl = padded
        k = 1
        while k < LANES:
            incl = incl + jnp.where(lane8 >= k, pltpu.roll(incl, k, axis=1), 0.0)
            k *= 2
        carry_ref[...] = (incl - padded)[0:1, :]

    lane = lax.broadcasted_iota(jnp.int32, (tr, LANES), 1)
    lane_f = lane.astype(F32)
    oh0 = lane_f == route_ref[:, 0:1]
    oh1 = lane_f == route_ref[:, 1:2]
    oh = jnp.where(oh0, 1.0, jnp.where(oh1, 1.0, 0.0))
    ri = lax.broadcasted_iota(jnp.int32, (tr, tr), 0)
    ci = lax.broadcasted_iota(jnp.int32, (tr, tr), 1)
    earlier = jnp.where(ci < ri, 1.0, 0.0).astype(BF16)
    row = jnp.dot(earlier, oh.astype(BF16), preferred_element_type=F32) + carry_ref[...]
    d0 = jnp.sum(jnp.where(oh0, row, 0.0), axis=-1, keepdims=True)
    d1 = jnp.sum(jnp.where(oh1, row, 0.0), axis=-1, keepdims=True)
    cols = jnp.where(lane == 0, d0, jnp.where(lane == 1, d1, 0.0))
    rows = cols.T
    dest_ref[...] = jnp.concatenate([rows[s:s + 1, :] for s in range(TOP_K)], axis=1).astype(jnp.int32)
    carry_ref[...] = carry_ref[...] + jnp.sum(oh, axis=0, keepdims=True)


def _rank(route, cnt, tile):
    N = route.shape[0]
    tr = TM_RANK
    return pl.pallas_call(
        functools.partial(_rank_kernel, tile=tile),
        grid=(N // tr,),
        in_specs=[pl.BlockSpec((tr, LANES), lambda i: (i, 0)), pl.BlockSpec((1, LANES), lambda i: (0, 0))],
        out_specs=pl.BlockSpec((None, 1, TOP_K * tr), lambda i: (i, 0, 0)),
        out_shape=jax.ShapeDtypeStruct((N // tr, 1, TOP_K * tr), jnp.int32),
        scratch_shapes=[pltpu.VMEM((1, LANES), F32)],
        compiler_params=_cp("arbitrary"),
        name="rank",
    )(route, cnt)


def _slab(i):
    return pl.ds(pl.multiple_of(i * ROW_SLAB, ROW_SLAB), ROW_SLAB)


def _dispatch_kernel(dest_ref, xl_ref, xs_hbm, sem):
    tm = xl_ref.shape[0] // ROW_SLAB

    def copy(t, d):
        return pltpu.make_async_copy(xl_ref.at[_slab(t)], xs_hbm.at[_slab(d)], sem)

    def start(t, c):
        for s in range(TOP_K):
            copy(t, dest_ref[0, s * tm + t]).start(priority=s % 2)
        return c

    def wait(t, c):
        for s in range(TOP_K):
            copy(t, dest_ref[0, s * tm + t]).wait()
        return c

    lax.fori_loop(0, tm, start, 0, unroll=8)
    lax.fori_loop(0, tm, wait, 0, unroll=8)


def _dispatch(xl, dest, n_rows):
    N = xl.shape[0] // ROW_SLAB
    tm = TM_DISPATCH
    return pl.pallas_call(
        _dispatch_kernel,
        grid=(N // tm,),
        in_specs=[pl.BlockSpec((None, 1, TOP_K * tm), lambda i: (i, 0, 0), memory_space=pltpu.SMEM),
                  pl.BlockSpec((tm * ROW_SLAB, LANES), lambda i: (i, 0))],
        out_specs=pl.BlockSpec(memory_space=pl.ANY),
        out_shape=jax.ShapeDtypeStruct((n_rows * ROW_SLAB, LANES), jnp.uint32),
        scratch_shapes=[pltpu.SemaphoreType.DMA(())],
        compiler_params=_cp("arbitrary"),
        name="dispatch",
    )(dest, xl)


def _ffn_kernel(te_ref, tv_ref, ti_ref, tf_ref, tn_ref, xs_ref, wg_hbm, wu_hbm, wd_hbm, ys_ref,
                x_ref, stage_g, stage_u, stage_d, wg_ref, wu_ref, wd_ref, sems):
    del ti_ref
    t = pl.program_id(0)
    nv = tv_ref[t]
    tm = x_ref.shape[0]
    half = x_ref.shape[1] // 2
    stages = ((wg_hbm, stage_g, wg_ref), (wu_hbm, stage_u, wu_ref), (wd_hbm, stage_d, wd_ref))

    def fetch(e):
        return [pltpu.make_async_copy(src.at[e], stage, sems.at[k]) for k, (src, stage, _) in enumerate(stages)]

    @pl.when(t == 0)
    def _():
        for c in fetch(te_ref[0]):
            c.start()

    @pl.when(tf_ref[t] == 1)
    def _():
        for c in fetch(0):
            c.wait()
        for _, stage, dst in stages:
            rows = stage.shape[0] // FFN_CAST_CHUNKS

            def cast(i, carry, stage=stage, dst=dst, rows=rows):
                rs = pl.ds(pl.multiple_of(i * rows, rows), rows)
                dst[rs, :] = stage[rs, :].astype(BF16)
                return carry

            lax.fori_loop(0, FFN_CAST_CHUNKS, cast, 0)

        @pl.when(tn_ref[t] >= 0)
        def _():
            for c in fetch(tn_ref[t]):
                c.start()

    @pl.when(nv > 0)
    def _():
        keep = lax.broadcasted_iota(jnp.int32, (tm, 1), 0) < nv
        for j in range(ROW_SLAB):
            w = jnp.where(keep, xs_ref[pl.ds(j, tm, stride=ROW_SLAB), :], jnp.uint32(0))
            x_ref[:, j * LANES:(j + 1) * LANES] = _unpack_lo(w).astype(BF16)
            x_ref[:, half + j * LANES:half + (j + 1) * LANES] = _unpack_hi(w).astype(BF16)
        x = x_ref[...]
        g = jnp.dot(x, wg_ref[...], preferred_element_type=F32)
        u = jnp.dot(x, wu_ref[...], preferred_element_type=F32)
        hm = ((g * _sigmoid(g)) * u).astype(BF16)
        _store_slabs(ys_ref, _pack_rows(jnp.dot(hm, wd_ref[...], preferred_element_type=F32)))

    @pl.when(nv == 0)
    def _():
        ys_ref[...] = jnp.zeros_like(ys_ref)


def _ffn(xs, tile_e, tile_valid, tile_in, tile_first, tile_next, wg, wu, wd):
    D, de = wg.shape[1], wg.shape[2]
    tm = TM_FFN
    nt = tile_e.shape[0]
    hbm = pl.BlockSpec(memory_space=pl.ANY)
    grid_spec = pltpu.PrefetchScalarGridSpec(
        num_scalar_prefetch=5,
        grid=(nt,),
        in_specs=[pl.BlockSpec((tm * ROW_SLAB, LANES), lambda t, te, tv, ti, tf, tn: (ti[t], 0)), hbm, hbm, hbm],
        out_specs=pl.BlockSpec((tm * ROW_SLAB, LANES), lambda t, te, tv, ti, tf, tn: (t, 0)),
        scratch_shapes=[pltpu.VMEM((tm, D), BF16),
                        pltpu.VMEM((D, de), F32), pltpu.VMEM((D, de), F32), pltpu.VMEM((de, D), F32),
                        pltpu.VMEM((D, de), BF16), pltpu.VMEM((D, de), BF16), pltpu.VMEM((de, D), BF16),
                        pltpu.SemaphoreType.DMA((3,))],
    )
    return pl.pallas_call(
        _ffn_kernel,
        grid_spec=grid_spec,
        out_shape=jax.ShapeDtypeStruct(xs.shape, jnp.uint32),
        compiler_params=_cp("arbitrary"),
        name="ffn",
    )(tile_e, tile_valid, tile_in, tile_first, tile_next, xs, wg, wu, wd)


def _combine_kernel(dcur_ref, dnext_ref, route_ref, x1_ref, gate2_ref, g_ref, ys_hbm, o_ref, buf, y_ref, sems):
    i = pl.program_id(0)
    nsteps = pl.num_programs(0)
    tm = x1_ref.shape[0]
    half = x1_ref.shape[1] // 2

    def copy(slot, t, s, d):
        return pltpu.make_async_copy(ys_hbm.at[_slab(d)], buf.at[slot, s, _slab(t)], sems.at[slot])

    span = dcur_ref.shape[1] // TOP_K

    def start_tile(slot, dref, tile):
        base = (tile % (span // tm)) * tm

        def body(t, c):
            for s in range(TOP_K):
                copy(slot, t, s, dref[0, s * span + base + t]).start(priority=s % 2)
            return c
        lax.fori_loop(0, tm, body, 0, unroll=8)

    def wait_tile(slot):
        def body(t, c):
            for s in range(TOP_K):
                copy(slot, t, s, 0).wait()
            return c
        lax.fori_loop(0, tm, body, 0, unroll=8)

    @pl.when(i == 0)
    def _():
        start_tile(0, dcur_ref, i)

    for slot in range(2):
        @pl.when(jnp.logical_and(i + 1 < nsteps, (i + 1) % 2 == slot))
        def _():
            start_tile(slot, dnext_ref, i + 1)

    for slot in range(2):
        @pl.when(i % 2 == slot)
        def _():
            wait_tile(slot)
            w0 = route_ref[:, 2:3]
            w1 = route_ref[:, 3:4]
            for j in range(ROW_SLAB):
                a = buf[slot, 0, pl.ds(j, tm, stride=ROW_SLAB), :]
                b = buf[slot, 1, pl.ds(j, tm, stride=ROW_SLAB), :]
                y_ref[:, j * LANES:(j + 1) * LANES] = _unpack_lo(a) * w0 + _unpack_lo(b) * w1
                y_ref[:, half + j * LANES:half + (j + 1) * LANES] = _unpack_hi(a) * w0 + _unpack_hi(b) * w1

    o_ref[...] = x1_ref[...] + gate2_ref[...] * (_rms(y_ref[...]) * g_ref[...])


def _combine(ys, dest, route, x1, mod3, g_post, seq):
    N, D = x1.shape
    tm = TM_COMBINE
    tpb = seq // tm
    nsteps = N // tm
    span = dest.shape[2] // TOP_K
    per = span // tm
    assert span % tm == 0
    return pl.pallas_call(
        _combine_kernel,
        grid=(nsteps,),
        in_specs=[pl.BlockSpec((None, 1, TOP_K * span), lambda i: (i // per, 0, 0), memory_space=pltpu.SMEM),
                  pl.BlockSpec((None, 1, TOP_K * span), lambda i: (jnp.minimum(i + 1, nsteps - 1) // per, 0, 0),
                               memory_space=pltpu.SMEM),
                  pl.BlockSpec((tm, LANES), lambda i: (i, 0)),
                  pl.BlockSpec((tm, D), lambda i: (i, 0)),
                  pl.BlockSpec((None, 1, D), lambda i: (i // tpb, 0, 5)),
                  pl.BlockSpec((1, D), lambda i: (0, 0)),
                  pl.BlockSpec(memory_space=pl.ANY)],
        out_specs=pl.BlockSpec((tm, D), lambda i: (i, 0)),
        out_shape=jax.ShapeDtypeStruct((N, D), F32),
        scratch_shapes=[pltpu.VMEM((2, TOP_K, tm * ROW_SLAB, LANES), jnp.uint32), pltpu.VMEM((tm, D), F32),
                        pltpu.SemaphoreType.DMA((2,))],
        compiler_params=_cp("arbitrary"),
        name="combine",
    )(dest, dest, route, x1, mod3, g_post, ys)


def _tile_tables(cnt, n_assign, n_experts, tm):
    counts = cnt[0, :n_experts].astype(jnp.int32)
    padded = ((counts + tm - 1) // tm) * tm
    pend = jnp.cumsum(padded)
    pstart = pend - padded
    nt = n_assign // tm + n_experts
    n_used = pend[-1] // tm
    tidx = jnp.arange(nt, dtype=jnp.int32)
    tstart = tidx * tm
    te = jnp.minimum(jnp.sum(pend[None, :] <= tstart[:, None], axis=1), n_experts - 1).astype(jnp.int32)
    tv = jnp.clip(counts[te] - (tstart - pstart[te]), 0, tm).astype(jnp.int32)
    used = tidx < n_used
    last = jnp.maximum(n_used - 1, 0)
    te = jnp.where(used, te, te[last]).astype(jnp.int32)
    ti = jnp.where(used, tidx, last).astype(jnp.int32)
    first = used & ((tidx == 0) | (te != jnp.roll(te, 1)))
    nxt = lax.cummin(jnp.where(first, tidx, nt)[::-1])[::-1]
    nxt = jnp.concatenate([nxt[1:], jnp.full((1,), nt, jnp.int32)])
    tn = jnp.where(first & (nxt < nt), te[jnp.minimum(nxt, nt - 1)], -1).astype(jnp.int32)
    return te, tv, ti, first.astype(jnp.int32), tn, nt


def kernel(x, c, w_mod, b_mod, g_pre_mix, g_post_mix, w_in, g_gmlp_v, w_spatial, b_spatial, g_out_gmlp,
           g_out_attn, w_out, g_pre_ffn, g_post_ffn, w_router_group, b_router_group, w_router_expert,
           b_router_expert, w_gate, w_up, w_down):
    B, S, D = x.shape
    N = B * S
    depth = w_mod.shape[0]
    d_gmlp = g_gmlp_v.shape[1]
    d_attn = g_out_attn.shape[1]
    heads = w_spatial.shape[1]
    n_groups, epg = b_router_expert.shape[1], b_router_expert.shape[2]
    n_experts = n_groups * epg
    assert d_gmlp == d_attn and w_in.shape[2] == 2 * d_gmlp + 3 * d_attn
    assert n_groups + n_experts <= LANES

    x2 = x.reshape(N, D)
    for l in range(depth):
        mod3 = _modulation(c, w_mod[l], b_mod[l]).reshape(B, 1, w_mod.shape[2])

        pa = _inproj(x2, mod3, g_pre_mix[l][None], w_in[l].astype(BF16), g_gmlp_v[l][None], S)
        ya_n = _gmlp(pa, w_spatial[l], b_spatial[l], g_out_gmlp[l][None])
        ob = _attention(pa.reshape(B, S, pa.shape[1]), heads, d_attn, (2 * d_gmlp) // d_attn).reshape(N, d_attn)

        wr32 = jnp.concatenate([w_router_group[l],
                                jnp.transpose(w_router_expert[l], (1, 0, 2)).reshape(D, n_experts)], axis=1)
        wr32 = jnp.pad(wr32, ((0, 0), (0, LANES - wr32.shape[1])))
        wr_hi = wr32.astype(BF16)
        wr = jnp.concatenate([wr_hi, (wr32 - wr_hi.astype(F32)).astype(BF16)], axis=1)
        br = jnp.pad(jnp.concatenate([b_router_group[l], b_router_expert[l].reshape(n_experts)]),
                     (0, LANES - n_groups - n_experts))[None]

        x1, xl, route, cnt = _outproj(ya_n, ob, x2, w_out[l].astype(BF16), g_out_attn[l][None],
                                      g_post_mix[l][None], mod3, g_pre_ffn[l][None], wr, br, S, n_groups, epg)

        dest = _rank(route, cnt, TM_FFN)
        te, tv, ti, tf, tn, nt = _tile_tables(cnt, N * TOP_K, n_experts, TM_FFN)
        xs = _dispatch(xl, dest, nt * TM_FFN)
        ys = _ffn(xs, te, tv, ti, tf, tn, w_gate[l], w_up[l], w_down[l])
        x2 = _combine(ys, dest, route, x1, mod3, g_post_ffn[l][None], S)
    return x2.reshape(B, S, D)
```
